```python
import math
import jax, jax.numpy as jnp
from jax import lax
import numpy as np

D_MODEL = 1024
BATCH = 4
SEQ = 4096
DEPTH = 1

ATTN_HEADS = 8
HEAD_DIM = 64
ATTN_WIDTH = ATTN_HEADS * HEAD_DIM
IDX_HEADS = 8
IDX_DIM = 64
TOPK_MAX = 256
Q_BLOCK = 128
POOL_WINDOWS = (2, 4, 8, 16)
POOL_GROUPS = len(POOL_WINDOWS)
POOL_WIDTH = D_MODEL - ATTN_WIDTH
POOL_GROUP_DIM = POOL_WIDTH // POOL_GROUPS
REL_BUCKETS = 32
REL_MAX_DIST = 128
N_GROUPS = 4
EXPERTS_PER_GROUP = 8
N_EXPERTS = N_GROUPS * EXPERTS_PER_GROUP
TOP_K_INNER = 2
D_EXPERT = 256
LN_EPS = 1e-5
DEEPNORM_ALPHA = (2 * DEPTH) ** 0.25
DEEPNORM_BETA = (8 * DEPTH) ** -0.25
Q_OFF = 0
K_OFF = Q_OFF + ATTN_WIDTH
V_OFF = K_OFF + ATTN_WIDTH
IQ_OFF = V_OFF + ATTN_WIDTH
IK_OFF = IQ_OFF + IDX_HEADS * IDX_DIM
IW_OFF = IK_OFF + IDX_DIM
P_OFF = IW_OFF + IDX_HEADS
IN_WIDTH = P_OFF + POOL_WIDTH

kernel_name = 'hybrid_dsa_pool_hmoe_deepnorm'


def layer_norm(x, g, b):
    xf = x.astype(jnp.float32)
    mu = jnp.mean(xf, axis=-1, keepdims=True)
    var = jnp.mean(jnp.square(xf - mu), axis=-1, keepdims=True)
    y = (xf - mu) * lax.rsqrt(var + LN_EPS) * g.astype(jnp.float32) + b.astype(jnp.float32)
    return y.astype(x.dtype)


def rel_bucket(dist):
    max_exact = REL_BUCKETS // 2
    n = jnp.maximum(dist, 0)
    nf = jnp.maximum(n, 1).astype(jnp.float32)
    large = max_exact + (jnp.log(nf / max_exact) / math.log(REL_MAX_DIST / max_exact)
                         * (REL_BUCKETS - max_exact)).astype(jnp.int32)
    large = jnp.minimum(large, REL_BUCKETS - 1)
    return jnp.where(n < max_exact, n, large)


def dsa_attention(q, k, v, iq, ik, iw, rel_bias):
    B, S = q.shape[0], q.shape[1]
    k_top = min(TOPK_MAX, S // 4)
    n_blocks = S // Q_BLOCK
    key_pos = jnp.arange(S)

    def block(i):
        start = i * Q_BLOCK
        qb = lax.dynamic_slice_in_dim(q, start, Q_BLOCK, axis=1)
        iqb = lax.dynamic_slice_in_dim(iq, start, Q_BLOCK, axis=1)
        iwb = lax.dynamic_slice_in_dim(iw, start, Q_BLOCK, axis=1)
        q_pos = start + jnp.arange(Q_BLOCK)
        dots = jnp.einsum('bqjd,bsd->bqjs', iqb, ik).astype(jnp.float32) * (IDX_DIM ** -0.5)
        score = jnp.einsum('bqj,bqjs->bqs', iwb.astype(jnp.float32), jax.nn.relu(dots))
        causal = key_pos[None, :] <= q_pos[:, None]
        score = jnp.where(causal[None], score, -jnp.inf)
        _, sel = lax.top_k(score, k_top)
        valid = sel <= q_pos[None, :, None]
        k_sel = jax.vmap(lambda kb, ib: kb[ib])(k, sel).reshape(B, Q_BLOCK, k_top, ATTN_HEADS, HEAD_DIM)
        v_sel = jax.vmap(lambda vb, ib: vb[ib])(v, sel).reshape(B, Q_BLOCK, k_top, ATTN_HEADS, HEAD_DIM)
        logits = jnp.einsum('bqhd,bqkhd->bqhk', qb, k_sel).astype(jnp.float32) * (HEAD_DIM ** -0.5)
        bias = rel_bias[rel_bucket(q_pos[None, :, None] - sel)]
        logits = logits + jnp.moveaxis(bias, -1, 2).astype(jnp.float32)
        logits = jnp.where(valid[:, :, None, :], logits, -jnp.inf)
        p = jax.nn.softmax(logits, axis=-1).astype(v.dtype)
        return jnp.einsum('bqhk,bqkhd->bqhd', p, v_sel)

    out = lax.map(block, jnp.arange(n_blocks))
    return jnp.moveaxis(out, 0, 1).reshape(B, S, ATTN_WIDTH)


def pool_mixer(u, w_pool, pool_scale):
    B, S = u.shape[0], u.shape[1]
    ug = u.reshape(B, S, POOL_GROUPS, POOL_GROUP_DIM)
    pos = jnp.arange(S)
    outs = []
    for g, w in enumerate(POOL_WINDOWS):
        ch = ug[:, :, g].astype(jnp.float32)
        c = lax.cumsum(ch, axis=1)
        c_shift = jnp.pad(c, ((0, 0), (w, 0), (0, 0)))[:, :S]
        count = jnp.minimum(pos + 1, w).astype(jnp.float32)[None, :, None]
        outs.append((c - c_shift) / count - ch)
    pooled = jnp.stack(outs, axis=2).astype(u.dtype)
    mixed = jnp.einsum('bsgc,gcd->bsgd', pooled, w_pool)
    return mixed.reshape(B, S, POOL_WIDTH) * pool_scale


def hier_moe(h, w_r1, b_r1, w_r2, b_r2, w_gate, w_up, w_down):
    B, S, D = h.shape
    t = h.reshape(-1, D)
    pg = jax.nn.softmax((t @ w_r1).astype(jnp.float32) + b_r1.astype(jnp.float32), axis=-1)
    pg_top, g = lax.top_k(pg, 1)
    lf = ((t @ w_r2).astype(jnp.float32) + b_r2.astype(jnp.float32)).reshape(-1, N_GROUPS, EXPERTS_PER_GROUP)
    lf_g = jnp.take_along_axis(lf, g[:, :, None], axis=1)[:, 0]
    pf = jax.nn.softmax(lf_g, axis=-1)
    pe, e_local = lax.top_k(pf, TOP_K_INNER)
    weights = pg_top * pe / jnp.sum(pe, axis=-1, keepdims=True)
    eid = g * EXPERTS_PER_GROUP + e_local
    combine = jnp.einsum('tk,tke->te', weights,
                         jax.nn.one_hot(eid, N_EXPERTS, dtype=jnp.float32)).astype(h.dtype)
    y = jnp.zeros_like(t)
    for e in range(N_EXPERTS):
        a = jax.nn.silu(t @ w_gate[e]) * (t @ w_up[e])
        y = y + combine[:, e:e + 1] * (a @ w_down[e])
    return y.reshape(B, S, D)


def setup_inputs(seed: int = 0) -> dict:
    key = jax.random.key(seed)
    ks = jax.random.split(key, 17)
    nrm = lambda k, shape, s: jax.random.normal(k, shape, jnp.float32) * s
    col_scale = jnp.ones((IN_WIDTH,), jnp.float32).at[V_OFF:IQ_OFF].set(DEEPNORM_BETA)
    return {
        'x': nrm(ks[0], (BATCH, SEQ, D_MODEL), 1.0),
        'w_in': nrm(ks[1], (DEPTH, D_MODEL, IN_WIDTH), D_MODEL ** -0.5) * col_scale,
        'w_pool': nrm(ks[2], (DEPTH, POOL_GROUPS, POOL_GROUP_DIM, POOL_GROUP_DIM), POOL_GROUP_DIM ** -0.5),
        'pool_scale': 1.0 + nrm(ks[3], (DEPTH, POOL_WIDTH), 0.05),
        'w_out': nrm(ks[4], (DEPTH, D_MODEL, D_MODEL), D_MODEL ** -0.5) * DEEPNORM_BETA,
        'rel_bias': nrm(ks[5], (REL_BUCKETS, ATTN_HEADS), 0.5),
        'ln1_g': 1.0 + nrm(ks[6], (DEPTH, D_MODEL), 0.05),
        'ln1_b': nrm(ks[7], (DEPTH, D_MODEL), 0.02),
        'w_r1': nrm(ks[8], (DEPTH, D_MODEL, N_GROUPS), D_MODEL ** -0.5),
        'b_r1': nrm(ks[9], (DEPTH, N_GROUPS), 0.01),
        'w_r2': nrm(ks[10], (DEPTH, D_MODEL, N_EXPERTS), D_MODEL ** -0.5),
        'b_r2': nrm(ks[11], (DEPTH, N_EXPERTS), 0.01),
        'w_gate': nrm(ks[12], (DEPTH, N_EXPERTS, D_MODEL, D_EXPERT), D_MODEL ** -0.5),
        'w_up': nrm(ks[13], (DEPTH, N_EXPERTS, D_MODEL, D_EXPERT), D_MODEL ** -0.5),
        'w_down': nrm(ks[14], (DEPTH, N_EXPERTS, D_EXPERT, D_MODEL), D_EXPERT ** -0.5) * DEEPNORM_BETA,
        'ln2_g': 1.0 + nrm(ks[15], (DEPTH, D_MODEL), 0.05),
        'ln2_b': nrm(ks[16], (DEPTH, D_MODEL), 0.02),
    }


def reference(x, w_in, w_pool, pool_scale, w_out, rel_bias, ln1_g, ln1_b, w_r1, b_r1, w_r2, b_r2,
              w_gate, w_up, w_down, ln2_g, ln2_b):
    B, S = x.shape[0], x.shape[1]
    h = x
    for l in range(DEPTH):
        proj = h @ w_in[l]
        q = proj[..., Q_OFF:K_OFF].reshape(B, S, ATTN_HEADS, HEAD_DIM)
        k = proj[..., K_OFF:V_OFF]
        v = proj[..., V_OFF:IQ_OFF]
        iq = proj[..., IQ_OFF:IK_OFF].reshape(B, S, IDX_HEADS, IDX_DIM)
        ik = proj[..., IK_OFF:IW_OFF]
        iw = proj[..., IW_OFF:P_OFF] * (IDX_HEADS ** -0.5)
        u = proj[..., P_OFF:IN_WIDTH]
        attn = dsa_attention(q, k, v, iq, ik, iw, rel_bias)
        pooled = pool_mixer(u, w_pool[l], pool_scale[l])
        mix = jnp.concatenate([attn, pooled], axis=-1) @ w_out[l]
        h = layer_norm(DEEPNORM_ALPHA * h + mix, ln1_g[l], ln1_b[l])
        ffn = hier_moe(h, w_r1[l], b_r1[l], w_r2[l], b_r2[l], w_gate[l], w_up[l], w_down[l])
        h = layer_norm(DEEPNORM_ALPHA * h + ffn, ln2_g[l], ln2_b[l])
    return h
```

```python
import functools
import math

import numpy as np
import jax
import jax.numpy as jnp
from jax import lax
from jax.experimental import pallas as pl
from jax.experimental.pallas import tpu as pltpu

F32 = jnp.float32
BF16 = jnp.bfloat16
I32 = jnp.int32

ATTN_HEADS = 8
HEAD_DIM = 64
ATTN_WIDTH = ATTN_HEADS * HEAD_DIM
IDX_HEADS = 8
IDX_DIM = 64
TOPK_MAX = 256
POOL_WINDOWS = (2, 4, 8, 16)
POOL_GROUP_DIM = 128
POOL_WIDTH = len(POOL_WINDOWS) * POOL_GROUP_DIM
POOL_HALO = 16
REL_BUCKETS = 32
REL_MAX_DIST = 128
N_GROUPS = 4
EXPERTS_PER_GROUP = 8
N_EXPERTS = N_GROUPS * EXPERTS_PER_GROUP
LN_EPS = 1e-5
DEPTH = 1
DEEPNORM_ALPHA = (2 * DEPTH) ** 0.25

LANES = 128
CHUNK = 256
INT_MIN = -2 ** 31
NEG_BIG = -1e30
VMEM_LIMIT = 56 * 1024 * 1024


def _rel_bucket_table(n):
    max_exact = REL_BUCKETS // 2
    d = np.arange(n)
    nf = np.maximum(d, 1).astype(np.float32)
    ratio = np.log(nf / np.float32(max_exact)) / np.float32(math.log(REL_MAX_DIST / max_exact))
    large = max_exact + (ratio * np.float32(REL_BUCKETS - max_exact)).astype(np.int32)
    large = np.minimum(large, REL_BUCKETS - 1)
    return np.where(d < max_exact, d, large).astype(np.int32)


def _near_bucket_tiles():
    tbl = _rel_bucket_table(2 * CHUNK)
    a = np.arange(CHUNK)[:, None]
    b = np.arange(CHUNK)[None, :]
    tiles = [tbl[np.maximum(delta * CHUNK + b - a, 0)] for delta in (0, 1)]
    return np.stack(tiles).astype(np.int32)


FAR_BUCKET = REL_BUCKETS - 1
assert int(_rel_bucket_table(2 * CHUNK)[CHUNK + 1:].min()) == FAR_BUCKET


def _bias_kernel(rb_ref, bucket_ref, o_ref):
    h = pl.program_id(1)
    bk = bucket_ref[0]
    far = rb_ref[FAR_BUCKET, h]
    acc = jnp.zeros(bk.shape, F32)
    for n in range(REL_BUCKETS):
        acc = jnp.where(bk == n, rb_ref[n, h] - far, acc)
    o_ref[0, 0] = acc


def _bias_tiles(rel_bias):
    buckets = jnp.asarray(_near_bucket_tiles())
    return pl.pallas_call(
        _bias_kernel,
        out_shape=jax.ShapeDtypeStruct((2, ATTN_HEADS, CHUNK, CHUNK), F32),
        grid=(2, ATTN_HEADS),
        in_specs=[pl.BlockSpec(memory_space=pltpu.SMEM),
                  pl.BlockSpec((1, CHUNK, CHUNK), lambda d, h: (d, 0, 0))],
        out_specs=pl.BlockSpec((1, 1, CHUNK, CHUNK), lambda d, h: (d, h, 0, 0)),
        name="bias_tiles",
    )(rel_bias, buckets)


Q2_ROWS = 2 * ATTN_WIDTH
WT_Q2, WT_V, WT_IQ, WT_IW = 0, Q2_ROWS, Q2_ROWS + ATTN_WIDTH, Q2_ROWS + 2 * ATTN_WIDTH
WT_ROWS = WT_IW + 16
WN_K, WN_U, WN_IK = 0, ATTN_WIDTH, ATTN_WIDTH + POOL_WIDTH
WN_COLS = WN_IK + LANES


def _proj_kernel(x_ref, wt_ref, wn_ref, q2t_ref, vt_ref, iqt_ref, iwt_ref, k_ref, ik_ref, u_ref, *, tm):
    xb = x_ref[0].astype(BF16)
    t = lax.dot_general(wt_ref[...], xb, (((1,), (1,)), ((), ())),
                        preferred_element_type=F32)
    q2t_ref[0] = (t[WT_Q2:WT_V] * (HEAD_DIM ** -0.5)).astype(BF16)
    iqt_ref[0] = (t[WT_IQ:WT_IW] * (IDX_DIM ** -0.5)).astype(BF16)
    iwt_ref[0] = t[WT_IW:WT_IW + IDX_HEADS] * (IDX_HEADS ** -0.5)
    for j in range(tm // CHUNK):
        vt_ref[0, j] = t[WT_V:WT_IQ, j * CHUNK:(j + 1) * CHUNK].astype(BF16)
    n = jnp.dot(xb, wn_ref[...], preferred_element_type=F32)
    k_ref[0] = n[:, WN_K:WN_U].astype(BF16)
    u_ref[0] = n[:, WN_U:WN_IK]
    ik_ref[0] = n[:, WN_IK:WN_IK + IDX_DIM].astype(BF16)


def _in_proj(x, w_in, *, tm=512):
    B, S, D = x.shape
    q_off, k_off, v_off = 0, ATTN_WIDTH, 2 * ATTN_WIDTH
    iq_off = 3 * ATTN_WIDTH
    ik_off = iq_off + IDX_HEADS * IDX_DIM
    iw_off = ik_off + IDX_DIM
    p_off = iw_off + IDX_HEADS
    wq = w_in[:, q_off:k_off].T.reshape(ATTN_HEADS // 2, 2, HEAD_DIM, D)
    zq = jnp.zeros((ATTN_HEADS // 2, HEAD_DIM, D), w_in.dtype)
    wq2 = jnp.stack([wq[:, 0], zq, zq, wq[:, 1]], axis=1).reshape(Q2_ROWS, D)
    wt = jnp.concatenate([
        wq2, w_in[:, v_off:iq_off].T, w_in[:, iq_off:ik_off].T, w_in[:, iw_off:p_off].T,
        jnp.zeros((WT_ROWS - WT_IW - IDX_HEADS, D), w_in.dtype)], axis=0).astype(BF16)
    wn = jnp.concatenate([
        w_in[:, k_off:v_off], w_in[:, p_off:], w_in[:, ik_off:iw_off],
        jnp.zeros((D, WN_COLS - WN_IK - IDX_DIM), w_in.dtype)], axis=1).astype(BF16)
    nt = S // tm
    cpt = tm // CHUNK
    outs = pl.pallas_call(
        functools.partial(_proj_kernel, tm=tm),
        out_shape=(
            jax.ShapeDtypeStruct((B, Q2_ROWS, S), BF16),
            jax.ShapeDtypeStruct((B, S // CHUNK, ATTN_WIDTH, CHUNK), BF16),
            jax.ShapeDtypeStruct((B, IDX_HEADS * IDX_DIM, S), BF16),
            jax.ShapeDtypeStruct((B, IDX_HEADS, S), F32),
            jax.ShapeDtypeStruct((B, S, ATTN_WIDTH), BF16),
            jax.ShapeDtypeStruct((B, S, IDX_DIM), BF16),
            jax.ShapeDtypeStruct((B, S, POOL_WIDTH), F32),
        ),
        grid=(B, nt),
        in_specs=[pl.BlockSpec((1, tm, D), lambda b, i: (b, i, 0)),
                  pl.BlockSpec((WT_ROWS, D), lambda b, i: (0, 0)),
                  pl.BlockSpec((D, WN_COLS), lambda b, i: (0, 0))],
        out_specs=(
            pl.BlockSpec((1, Q2_ROWS, tm), lambda b, i: (b, 0, i)),
            pl.BlockSpec((1, cpt, ATTN_WIDTH, CHUNK), lambda b, i: (b, i, 0, 0)),
            pl.BlockSpec((1, IDX_HEADS * IDX_DIM, tm), lambda b, i: (b, 0, i)),
            pl.BlockSpec((1, IDX_HEADS, tm), lambda b, i: (b, 0, i)),
            pl.BlockSpec((1, tm, ATTN_WIDTH), lambda b, i: (b, i, 0)),
            pl.BlockSpec((1, tm, IDX_DIM), lambda b, i: (b, i, 0)),
            pl.BlockSpec((1, tm, POOL_WIDTH), lambda b, i: (b, i, 0)),
        ),
        compiler_params=pltpu.CompilerParams(
            dimension_semantics=("arbitrary", "arbitrary"), vmem_limit_bytes=VMEM_LIMIT),
        name="in_proj",
    )(x, wt, wn)
    return outs


def _attn_kernel(q2t_ref, iqt_ref, iwt_ref, k_ref, vt_ref, ik_ref, bias_ref, o_ref,
                 key_ref, m_ref, s_ref, acc_ref, *, topk, idx_bits):
    i = pl.program_id(1)
    nch = i + 1
    C = CHUNK
    row = lax.broadcasted_iota(I32, (C, C), 0)
    col = lax.broadcasted_iota(I32, (C, C), 1)

    def score_body(c, carry):
        ikc = ik_ref[0, c]
        s = jnp.zeros((C, C), F32)
        for j in range(IDX_HEADS):
            d = jnp.dot(ikc, iqt_ref[0, j * IDX_DIM:(j + 1) * IDX_DIM, :],
                        preferred_element_type=F32)
            s = s + iwt_ref[0, j:j + 1, :] * jnp.maximum(d, 0.0)
        s = s + 0.0
        bits = pltpu.bitcast(s, I32)
        key = bits ^ ((bits >> 31) & 0x7FFFFFFF)
        causal = (c * C + row) <= (i * C + col)
        key_ref[c] = jnp.where(causal, key, INT_MIN)
        return carry

    lax.fori_loop(0, nch, score_body, 0)

    pos = i * C + lax.broadcasted_iota(I32, (1, C), 1)
    k_eff = jnp.minimum(pos + 1, topk).astype(F32)

    def count(indicator):
        def body(c, acc):
            return acc + jnp.sum(indicator(key_ref[c], c), axis=0, keepdims=True)
        return lax.fori_loop(0, nch, body, jnp.zeros((1, C), F32))

    def bit_body(b, th):
        cand = th + lax.shift_left(jnp.int32(1), 31 - b)
        cnt = count(lambda kc, c: jnp.where(kc >= cand, 1.0, 0.0))
        return jnp.where(cnt >= k_eff, cand, th)

    th = lax.fori_loop(0, 32, bit_body, jnp.full((1, C), INT_MIN, I32))

    surplus = count(lambda kc, c: jnp.where(kc >= th, 1.0, 0.0)) - k_eff

    @pl.when(jnp.max(surplus) > 0.0)
    def _():
        need = k_eff - count(lambda kc, c: jnp.where(kc > th, 1.0, 0.0))

        def idx_body(b, lim):
            cand = lim + lax.shift_left(jnp.int32(1), idx_bits - 1 - b)
            cnt = count(lambda kc, c: jnp.where(kc == th, jnp.where(c * C + row < cand, 1.0, 0.0), 0.0))
            return jnp.where(cnt < need, cand, lim)

        last = lax.fori_loop(0, idx_bits, idx_body, jnp.zeros((1, C), I32))

        def drop_body(c, carry):
            kc = key_ref[c]
            key_ref[c] = jnp.where(kc == th, jnp.where(c * C + row > last, INT_MIN, kc), kc)
            return carry

        lax.fori_loop(0, nch, drop_body, 0)

    m_ref[...] = jnp.full(m_ref.shape, NEG_BIG, F32)
    s_ref[...] = jnp.zeros(s_ref.shape, F32)
    acc_ref[...] = jnp.zeros(acc_ref.shape, F32)

    def attend(c, near):
        sel = key_ref[c] >= th
        for h in range(ATTN_HEADS):
            p2 = (h // 2) * 2 * HEAD_DIM
            lg = jnp.dot(k_ref[0, c, :, p2:p2 + 2 * HEAD_DIM],
                         q2t_ref[0, 2 * HEAD_DIM * h:2 * HEAD_DIM * (h + 1), :],
                         preferred_element_type=F32)
            if near is not None:
                lg = lg + bias_ref[near, h]
            lg = jnp.where(sel, lg, -jnp.inf)
            m_old = m_ref[h:h + 1, :]
            m_new = jnp.maximum(m_old, jnp.max(lg, axis=0, keepdims=True))
            alpha = jnp.exp(m_old - m_new)
            p = jnp.exp(lg - m_new)
            s_ref[h:h + 1, :] = alpha * s_ref[h:h + 1, :] + jnp.sum(p, axis=0, keepdims=True)
            pv = jnp.dot(vt_ref[0, c, h * HEAD_DIM:(h + 1) * HEAD_DIM, :], p.astype(BF16),
                         preferred_element_type=F32)
            hs = slice(h * HEAD_DIM, (h + 1) * HEAD_DIM)
            acc_ref[hs, :] = alpha * acc_ref[hs, :] + pv
            m_ref[h:h + 1, :] = m_new

    def far_body(c, carry):
        attend(c, None)
        return carry

    lax.fori_loop(0, jnp.maximum(i - 1, 0), far_body, 0)

    @pl.when(i >= 1)
    def _():
        attend(i - 1, 1)

    attend(i, 0)

    for h in range(ATTN_HEADS):
        hs = slice(h * HEAD_DIM, (h + 1) * HEAD_DIM)
        acc_ref[hs, :] = acc_ref[hs, :] / s_ref[h:h + 1, :]
    o_ref[0] = acc_ref[...].T.astype(o_ref.dtype)


def _dsa_attention(q2t, iqt, iwt, k, vt, ik, bias):
    B, S, _ = k.shape
    nq = S // CHUNK
    topk = min(TOPK_MAX, S // 4)
    k4 = k.reshape(B, nq, CHUNK, ATTN_WIDTH)
    ik4 = ik.reshape(B, nq, CHUNK, IDX_DIM)
    return pl.pallas_call(
        functools.partial(_attn_kernel, topk=topk, idx_bits=(S - 1).bit_length()),
        out_shape=jax.ShapeDtypeStruct((B, S, ATTN_WIDTH), BF16),
        grid=(B, nq),
        in_specs=[
            pl.BlockSpec((1, Q2_ROWS, CHUNK), lambda b, i: (b, 0, i)),
            pl.BlockSpec((1, IDX_HEADS * IDX_DIM, CHUNK), lambda b, i: (b, 0, i)),
            pl.BlockSpec((1, IDX_HEADS, CHUNK), lambda b, i: (b, 0, i)),
            pl.BlockSpec((1, nq, CHUNK, ATTN_WIDTH), lambda b, i: (b, 0, 0, 0)),
            pl.BlockSpec((1, nq, ATTN_WIDTH, CHUNK), lambda b, i: (b, 0, 0, 0)),
            pl.BlockSpec((1, nq, CHUNK, IDX_DIM), lambda b, i: (b, 0, 0, 0)),
            pl.BlockSpec((2, ATTN_HEADS, CHUNK, CHUNK), lambda b, i: (0, 0, 0, 0)),
        ],
        out_specs=pl.BlockSpec((1, CHUNK, ATTN_WIDTH), lambda b, i: (b, i, 0)),
        scratch_shapes=[
            pltpu.VMEM((nq, CHUNK, CHUNK), I32),
            pltpu.VMEM((ATTN_HEADS, CHUNK), F32),
            pltpu.VMEM((ATTN_HEADS, CHUNK), F32),
            pltpu.VMEM((ATTN_WIDTH, CHUNK), F32),
        ],
        compiler_params=pltpu.CompilerParams(
            dimension_semantics=("arbitrary", "arbitrary"), vmem_limit_bytes=VMEM_LIMIT),
        name="dsa_attn",
    )(q2t, iqt, iwt, k4, vt, ik4, bias)


ROUTE_LANES = LANES


def _layer_norm(y, g, b):
    mu = jnp.mean(y, axis=-1, keepdims=True)
    yc = y - mu
    var = jnp.mean(yc * yc, axis=-1, keepdims=True)
    return yc * lax.rsqrt(var + LN_EPS) * g + b


def _mix_kernel(attn_ref, u_ref, halo_ref, x_ref, wpool_ref, pscale_ref, wout_ref, g_ref, b_ref,
                wr_ref, br_ref, h_ref, hb_ref, comb_ref, *, tm):
    i = pl.program_id(1)
    halo = jnp.where(i > 0, halo_ref[0], 0.0)
    ue = jnp.concatenate([halo, u_ref[0]], axis=0)
    pos = i * tm + lax.broadcasted_iota(I32, (tm, 1), 0)
    mixed = []
    for g, w in enumerate(POOL_WINDOWS):
        gs = slice(g * POOL_GROUP_DIM, (g + 1) * POOL_GROUP_DIM)
        ch = ue[:, gs]
        win = ch
        step = 1
        while step < w:
            win = win + pltpu.roll(win, step, axis=0)
            step *= 2
        cnt = jnp.minimum(pos + 1, w).astype(F32)
        pooled = win[POOL_HALO:] / cnt - ch[POOL_HALO:]
        mg = jnp.dot(pooled.astype(BF16), wpool_ref[g], preferred_element_type=F32)
        mixed.append((mg * pscale_ref[:, gs]).astype(BF16))
    cat = jnp.concatenate([attn_ref[0]] + mixed, axis=-1)
    mix = jnp.dot(cat, wout_ref[...], preferred_element_type=F32)
    h = _layer_norm(DEEPNORM_ALPHA * x_ref[0] + mix, g_ref[...], b_ref[...])
    h_ref[0] = h
    hb = h.astype(BF16)
    hb_ref[0] = hb

    lg = jnp.dot(hb, wr_ref[...], preferred_element_type=F32) + br_ref[...]
    lane = lax.broadcasted_iota(I32, (tm, ROUTE_LANES), 1)
    gl = jnp.where(lane >= N_EXPERTS, jnp.where(lane < N_EXPERTS + N_GROUPS, lg, -jnp.inf), -jnp.inf)
    ge = jnp.exp(gl - jnp.max(gl, axis=-1, keepdims=True))
    pg = ge / jnp.sum(ge, axis=-1, keepdims=True)
    pg_top = jnp.max(pg, axis=-1, keepdims=True)
    g_lane = jnp.min(jnp.where(pg == pg_top, lane, ROUTE_LANES), axis=-1, keepdims=True)
    e_lo = (g_lane - N_EXPERTS) * EXPERTS_PER_GROUP
    fl = jnp.where(lane >= e_lo, jnp.where(lane < e_lo + EXPERTS_PER_GROUP, lg, -jnp.inf), -jnp.inf)
    fe = jnp.exp(fl - jnp.max(fl, axis=-1, keepdims=True))
    pf = fe / jnp.sum(fe, axis=-1, keepdims=True)
    p1 = jnp.max(pf, axis=-1, keepdims=True)
    i1 = jnp.min(jnp.where(pf == p1, lane, ROUTE_LANES), axis=-1, keepdims=True)
    pr = jnp.where(lane == i1, -1.0, jnp.where(fl == -jnp.inf, -1.0, pf))
    p2 = jnp.max(pr, axis=-1, keepdims=True)
    i2 = jnp.min(jnp.where(pr == p2, lane, ROUTE_LANES), axis=-1, keepdims=True)
    psum = p1 + p2
    w1 = pg_top * p1 / psum
    w2 = pg_top * p2 / psum
    comb = jnp.where(lane == i1, w1, jnp.where(lane == i2, w2, 0.0))
    comb_ref[0] = comb[:, :N_EXPERTS]


def _mix_norm_route(attn, u, x, w_pool, pool_scale, w_out, ln_g, ln_b, w_r1, b_r1, w_r2, b_r2, *, tm=256):
    B, S, D = x.shape
    wr = jnp.concatenate([w_r2, w_r1, jnp.zeros((D, ROUTE_LANES - N_EXPERTS - N_GROUPS), w_r1.dtype)],
                         axis=1).astype(BF16)
    br = jnp.concatenate([b_r2, b_r1, jnp.zeros((ROUTE_LANES - N_EXPERTS - N_GROUPS,), b_r1.dtype)])[None, :]
    hpt = tm // POOL_HALO
    return pl.pallas_call(
        functools.partial(_mix_kernel, tm=tm),
        out_shape=(jax.ShapeDtypeStruct((B, S, D), F32),
                   jax.ShapeDtypeStruct((B, S, D), BF16),
                   jax.ShapeDtypeStruct((B, S, N_EXPERTS), F32)),
        grid=(B, S // tm),
        in_specs=[
            pl.BlockSpec((1, tm, ATTN_WIDTH), lambda b, i: (b, i, 0)),
            pl.BlockSpec((1, tm, POOL_WIDTH), lambda b, i: (b, i, 0)),
            pl.BlockSpec((1, POOL_HALO, POOL_WIDTH), lambda b, i: (b, jnp.maximum(i * hpt - 1, 0), 0)),
            pl.BlockSpec((1, tm, D), lambda b, i: (b, i, 0)),
            pl.BlockSpec((len(POOL_WINDOWS), POOL_GROUP_DIM, POOL_GROUP_DIM), lambda b, i: (0, 0, 0)),
            pl.BlockSpec((1, POOL_WIDTH), lambda b, i: (0, 0)),
            pl.BlockSpec((D, D), lambda b, i: (0, 0)),
            pl.BlockSpec((1, D), lambda b, i: (0, 0)),
            pl.BlockSpec((1, D), lambda b, i: (0, 0)),
            pl.BlockSpec((D, ROUTE_LANES), lambda b, i: (0, 0)),
            pl.BlockSpec((1, ROUTE_LANES), lambda b, i: (0, 0)),
        ],
        out_specs=(pl.BlockSpec((1, tm, D), lambda b, i: (b, i, 0)),
                   pl.BlockSpec((1, tm, D), lambda b, i: (b, i, 0)),
                   pl.BlockSpec((1, tm, N_EXPERTS), lambda b, i: (b, i, 0))),
        compiler_params=pltpu.CompilerParams(
            dimension_semantics=("arbitrary", "arbitrary"), vmem_limit_bytes=VMEM_LIMIT),
        name="mix_norm",
    )(attn, u, u, x, w_pool.astype(BF16), pool_scale[None, :], w_out.astype(BF16),
      ln_g[None, :], ln_b[None, :], wr, br)


def _moe_kernel(hb_ref, h_ref, comb_ref, wgu_ref, wd_ref, g_ref, b_ref, o_ref, acc_ref, *, d_expert):
    e = pl.program_id(1)

    @pl.when(e == 0)
    def _():
        acc_ref[...] = jnp.zeros(acc_ref.shape, F32)

    gu = jnp.dot(hb_ref[...], wgu_ref[0], preferred_element_type=F32)
    gate = gu[:, :d_expert]
    a = gate * jax.nn.sigmoid(gate) * gu[:, d_expert:]
    y = jnp.dot(a.astype(BF16), wd_ref[0], preferred_element_type=F32)
    comb = comb_ref[...]
    lane = lax.broadcasted_iota(I32, comb.shape, 1)
    ce = jnp.sum(jnp.where(lane == e, comb, 0.0), axis=-1, keepdims=True)
    acc_ref[...] += ce * y

    @pl.when(e == pl.num_programs(1) - 1)
    def _():
        o_ref[...] = _layer_norm(DEEPNORM_ALPHA * h_ref[...] + acc_ref[...], g_ref[...], b_ref[...])


def _moe_norm(hb, h, comb, w_gate, w_up, w_down, ln_g, ln_b, *, tm=1024):
    T, D = h.shape
    E, _, F = w_gate.shape
    wgu = jnp.concatenate([w_gate, w_up], axis=-1).astype(BF16)
    wd = w_down.astype(BF16)
    return pl.pallas_call(
        functools.partial(_moe_kernel, d_expert=F),
        out_shape=jax.ShapeDtypeStruct((T, D), F32),
        grid=(T // tm, E),
        in_specs=[
            pl.BlockSpec((tm, D), lambda t, e: (t, 0)),
            pl.BlockSpec((tm, D), lambda t, e: (t, 0)),
            pl.BlockSpec((tm, E), lambda t, e: (t, 0)),
            pl.BlockSpec((1, D, 2 * F), lambda t, e: (e, 0, 0)),
            pl.BlockSpec((1, F, D), lambda t, e: (e, 0, 0)),
            pl.BlockSpec((1, D), lambda t, e: (0, 0)),
            pl.BlockSpec((1, D), lambda t, e: (0, 0)),
        ],
        out_specs=pl.BlockSpec((tm, D), lambda t, e: (t, 0)),
        scratch_shapes=[pltpu.VMEM((tm, D), F32)],
        compiler_params=pltpu.CompilerParams(
            dimension_semantics=("arbitrary", "arbitrary"), vmem_limit_bytes=VMEM_LIMIT),
        name="moe_norm",
    )(hb, h, comb, wgu, wd, ln_g[None, :], ln_b[None, :])


def kernel(x, w_in, w_pool, pool_scale, w_out, rel_bias, ln1_g, ln1_b, w_r1, b_r1, w_r2, b_r2,
           w_gate, w_up, w_down, ln2_g, ln2_b):
    B, S, D = x.shape
    assert w_in.shape[0] == DEPTH == 1 and S % CHUNK == 0
    bias = _bias_tiles(rel_bias)
    q2t, vt, iqt, iwt, k, ik, u = _in_proj(x, w_in[0])
    attn = _dsa_attention(q2t, iqt, iwt, k, vt, ik, bias)
    h, hb, comb = _mix_norm_route(attn, u, x, w_pool[0], pool_scale[0], w_out[0], ln1_g[0], ln1_b[0],
                                  w_r1[0], b_r1[0], w_r2[0], b_r2[0])
    out = _moe_norm(hb.reshape(B * S, D), h.reshape(B * S, D), comb.reshape(B * S, N_EXPERTS),
                    w_gate[0], w_up[0], w_down[0], ln2_g[0], ln2_b[0])
    return out.reshape(B, S, D)
```

```python
import functools
import math

import numpy as np
import jax
import jax.numpy as jnp
from jax import lax
from jax.experimental import pallas as pl
from jax.experimental.pallas import tpu as pltpu

F32 = jnp.float32
BF16 = jnp.bfloat16
I32 = jnp.int32

ATTN_HEADS = 8
HEAD_DIM = 64
ATTN_WIDTH = ATTN_HEADS * HEAD_DIM
IDX_HEADS = 8
IDX_DIM = 64
TOPK_MAX = 256
POOL_WINDOWS = (2, 4, 8, 16)
POOL_GROUP_DIM = 128
POOL_WIDTH = len(POOL_WINDOWS) * POOL_GROUP_DIM
POOL_HALO = 16
REL_BUCKETS = 32
REL_MAX_DIST = 128
N_GROUPS = 4
EXPERTS_PER_GROUP = 8
N_EXPERTS = N_GROUPS * EXPERTS_PER_GROUP
LN_EPS = 1e-5
DEPTH = 1
DEEPNORM_ALPHA = (2 * DEPTH) ** 0.25

LANES = 128
CHUNK = 256
INT_MIN = -2 ** 31
NEG_BIG = -1e30
VMEM_LIMIT = 56 * 1024 * 1024


def _rel_bucket_table(n):
    max_exact = REL_BUCKETS // 2
    d = np.arange(n)
    nf = np.maximum(d, 1).astype(np.float32)
    ratio = np.log(nf / np.float32(max_exact)) / np.float32(math.log(REL_MAX_DIST / max_exact))
    large = max_exact + (ratio * np.float32(REL_BUCKETS - max_exact)).astype(np.int32)
    large = np.minimum(large, REL_BUCKETS - 1)
    return np.where(d < max_exact, d, large).astype(np.int32)


def _near_bucket_tiles():
    tbl = _rel_bucket_table(2 * CHUNK)
    a = np.arange(CHUNK)[:, None]
    b = np.arange(CHUNK)[None, :]
    tiles = [tbl[np.maximum(delta * CHUNK + b - a, 0)] for delta in (0, 1)]
    return np.stack(tiles).astype(np.int32)


FAR_BUCKET = REL_BUCKETS - 1
assert int(_rel_bucket_table(2 * CHUNK)[CHUNK + 1:].min()) == FAR_BUCKET


def _bias_kernel(rb_ref, bucket_ref, o_ref):
    h = pl.program_id(1)
    bk = bucket_ref[0]
    far = rb_ref[FAR_BUCKET, h]
    acc = jnp.zeros(bk.shape, F32)
    for n in range(REL_BUCKETS):
        acc = jnp.where(bk == n, rb_ref[n, h] - far, acc)
    o_ref[0, 0] = acc


def _bias_tiles(rel_bias):
    buckets = jnp.asarray(_near_bucket_tiles())
    return pl.pallas_call(
        _bias_kernel,
        out_shape=jax.ShapeDtypeStruct((2, ATTN_HEADS, CHUNK, CHUNK), F32),
        grid=(2, ATTN_HEADS),
        in_specs=[pl.BlockSpec(memory_space=pltpu.SMEM),
                  pl.BlockSpec((1, CHUNK, CHUNK), lambda d, h: (d, 0, 0))],
        out_specs=pl.BlockSpec((1, 1, CHUNK, CHUNK), lambda d, h: (d, h, 0, 0)),
        name="bias_tiles",
    )(rel_bias, buckets)


Q2_ROWS = 2 * ATTN_WIDTH
WT_Q2, WT_V, WT_IQ, WT_IW = 0, Q2_ROWS, Q2_ROWS + ATTN_WIDTH, Q2_ROWS + 2 * ATTN_WIDTH
WT_ROWS = WT_IW + 16
WN_K, WN_U, WN_IK = 0, ATTN_WIDTH, ATTN_WIDTH + POOL_WIDTH
WN_COLS = WN_IK + LANES


def _proj_kernel(x_ref, wt_ref, wn_ref, q2t_ref, vt_ref, iqt_ref, iwt_ref, k_ref, ik_ref, u_ref, *, tm):
    xb = x_ref[0].astype(BF16)
    t = lax.dot_general(wt_ref[...], xb, (((1,), (1,)), ((), ())),
                        preferred_element_type=F32)
    q2t_ref[0] = (t[WT_Q2:WT_V] * (HEAD_DIM ** -0.5)).astype(BF16)
    iqt_ref[0] = (t[WT_IQ:WT_IW] * (IDX_DIM ** -0.5)).astype(BF16)
    iwt_ref[0] = t[WT_IW:WT_IW + IDX_HEADS] * (IDX_HEADS ** -0.5)
    for j in range(tm // CHUNK):
        vt_ref[0, j] = t[WT_V:WT_IQ, j * CHUNK:(j + 1) * CHUNK].astype(BF16)
    n = jnp.dot(xb, wn_ref[...], preferred_element_type=F32)
    k_ref[0] = n[:, WN_K:WN_U].astype(BF16)
    u_ref[0] = n[:, WN_U:WN_IK]
    ik_ref[0] = n[:, WN_IK:WN_IK + IDX_DIM].astype(BF16)


def _in_proj(x, w_in, *, tm=512):
    B, S, D = x.shape
    q_off, k_off, v_off = 0, ATTN_WIDTH, 2 * ATTN_WIDTH
    iq_off = 3 * ATTN_WIDTH
    ik_off = iq_off + IDX_HEADS * IDX_DIM
    iw_off = ik_off + IDX_DIM
    p_off = iw_off + IDX_HEADS
    wq = w_in[:, q_off:k_off].T.reshape(ATTN_HEADS // 2, 2, HEAD_DIM, D)
    zq = jnp.zeros((ATTN_HEADS // 2, HEAD_DIM, D), w_in.dtype)
    wq2 = jnp.stack([wq[:, 0], zq, zq, wq[:, 1]], axis=1).reshape(Q2_ROWS, D)
    wt = jnp.concatenate([
        wq2, w_in[:, v_off:iq_off].T, w_in[:, iq_off:ik_off].T, w_in[:, iw_off:p_off].T,
        jnp.zeros((WT_ROWS - WT_IW - IDX_HEADS, D), w_in.dtype)], axis=0).astype(BF16)
    wn = jnp.concatenate([
        w_in[:, k_off:v_off], w_in[:, p_off:], w_in[:, ik_off:iw_off],
        jnp.zeros((D, WN_COLS - WN_IK - IDX_DIM), w_in.dtype)], axis=1).astype(BF16)
    nt = S // tm
    cpt = tm // CHUNK
    outs = pl.pallas_call(
        functools.partial(_proj_kernel, tm=tm),
        out_shape=(
            jax.ShapeDtypeStruct((B, Q2_ROWS, S), BF16),
            jax.ShapeDtypeStruct((B, S // CHUNK, ATTN_WIDTH, CHUNK), BF16),
            jax.ShapeDtypeStruct((B, IDX_HEADS * IDX_DIM, S), BF16),
            jax.ShapeDtypeStruct((B, IDX_HEADS, S), F32),
            jax.ShapeDtypeStruct((B, S, ATTN_WIDTH), BF16),
            jax.ShapeDtypeStruct((B, S, IDX_DIM), BF16),
            jax.ShapeDtypeStruct((B, S, POOL_WIDTH), F32),
        ),
        grid=(B, nt),
        in_specs=[pl.BlockSpec((1, tm, D), lambda b, i: (b, i, 0)),
                  pl.BlockSpec((WT_ROWS, D), lambda b, i: (0, 0)),
                  pl.BlockSpec((D, WN_COLS), lambda b, i: (0, 0))],
        out_specs=(
            pl.BlockSpec((1, Q2_ROWS, tm), lambda b, i: (b, 0, i)),
            pl.BlockSpec((1, cpt, ATTN_WIDTH, CHUNK), lambda b, i: (b, i, 0, 0)),
            pl.BlockSpec((1, IDX_HEADS * IDX_DIM, tm), lambda b, i: (b, 0, i)),
            pl.BlockSpec((1, IDX_HEADS, tm), lambda b, i: (b, 0, i)),
            pl.BlockSpec((1, tm, ATTN_WIDTH), lambda b, i: (b, i, 0)),
            pl.BlockSpec((1, tm, IDX_DIM), lambda b, i: (b, i, 0)),
            pl.BlockSpec((1, tm, POOL_WIDTH), lambda b, i: (b, i, 0)),
        ),
        compiler_params=pltpu.CompilerParams(
            dimension_semantics=("arbitrary", "arbitrary"), vmem_limit_bytes=VMEM_LIMIT),
        name="in_proj",
    )(x, wt, wn)
    return outs


VALUE_STEPS = 40
MAX_STEPS = VALUE_STEPS + 33


def _order_key(bits):
    return bits ^ ((bits >> 31) & 0x7FFFFFFF)


def _attn_kernel(q2t_ref, iqt_ref, iwt_ref, k_ref, vt_ref, ik_ref, bias_ref, o_ref,
                 key_ref, neg_ref, lg_ref, p_ref, m_ref, al_ref, s_ref, acc_ref, *, topk, idx_bits):
    i = pl.program_id(1)
    nch = i + 1
    C = CHUNK
    row = lax.broadcasted_iota(I32, (C, C), 0)
    col = lax.broadcasted_iota(I32, (C, C), 1)

    def fold_rows(x):
        return jnp.sum(x.reshape(C // 8, 8, C), axis=0)

    def score_body(c, carry):
        smin, smax = carry
        ikc = ik_ref[0, c]
        s = jnp.zeros((C, C), F32)
        for j in range(IDX_HEADS):
            d = jnp.dot(ikc, iqt_ref[0, j * IDX_DIM:(j + 1) * IDX_DIM, :],
                        preferred_element_type=F32)
            s = s + iwt_ref[0, j:j + 1, :] * jnp.maximum(d, 0.0)
        s = s + 0.0
        key = _order_key(pltpu.bitcast(s, I32))
        causal = (c * C + row) <= (i * C + col)
        key_ref[c] = jnp.where(causal, key, INT_MIN)
        smin = jnp.minimum(smin, jnp.min(jnp.where(causal, s, jnp.inf), axis=0, keepdims=True))
        smax = jnp.maximum(smax, jnp.max(jnp.where(causal, s, -jnp.inf), axis=0, keepdims=True))
        return smin, smax

    smin, smax = lax.fori_loop(0, nch, score_body,
                               (jnp.full((1, C), jnp.inf, F32), jnp.full((1, C), -jnp.inf, F32)))
    kmin = _order_key(pltpu.bitcast(smin, I32))
    kmax = _order_key(pltpu.bitcast(smax, I32))

    pos = i * C + lax.broadcasted_iota(I32, (1, C), 1)
    n_keys = (pos + 1).astype(F32)
    k_eff = jnp.minimum(pos + 1, topk).astype(F32)

    def count(indicator):
        def body(c, acc):
            return acc + fold_rows(indicator(key_ref[c], c))
        part = lax.fori_loop(0, nch, body, jnp.zeros((8, C), F32))
        return jnp.sum(part, axis=0, keepdims=True)

    def open_cols(lo, hi, cnt_lo):
        return jnp.where(cnt_lo > k_eff, jnp.where(hi - 1 > lo, 1.0, 0.0), 0.0)

    def search_cond(st):
        it, lo, hi, cnt_lo = st
        return jnp.logical_and(it < MAX_STEPS, jnp.max(open_cols(lo, hi, cnt_lo)) > 0.0)

    def search_body(st):
        it, lo, hi, cnt_lo = st
        f_mid = 0.5 * pltpu.bitcast(_order_key(lo), F32) + 0.5 * pltpu.bitcast(_order_key(hi), F32)
        v_mid = _order_key(pltpu.bitcast(f_mid, I32))
        i_mid = (lo >> 1) + (hi >> 1) + (lo & hi & 1)
        mid = jnp.where(it < VALUE_STEPS, v_mid, i_mid)
        mid = jnp.minimum(jnp.maximum(mid, lo + 1), hi - 1)
        cnt = count(lambda kc, c: jnp.where(kc >= mid, 1.0, 0.0))
        is_open = open_cols(lo, hi, cnt_lo) > 0.0
        up = jnp.where(is_open, jnp.where(cnt >= k_eff, 1, 0), 0)
        dn = jnp.where(is_open, jnp.where(cnt >= k_eff, 0, 1), 0)
        return (it + 1, jnp.where(up == 1, mid, lo), jnp.where(dn == 1, mid, hi),
                jnp.where(up == 1, cnt, cnt_lo))

    hi0 = jnp.where(kmax == 2 ** 31 - 1, kmax, kmax + 1)
    _, th, _, cnt_th = lax.while_loop(search_cond, search_body, (jnp.int32(0), kmin, hi0, n_keys))

    @pl.when(jnp.max(cnt_th - k_eff) > 0.0)
    def _():
        need = k_eff - count(lambda kc, c: jnp.where(kc > th, 1.0, 0.0))

        def idx_body(b, lim):
            cand = lim + lax.shift_left(jnp.int32(1), idx_bits - 1 - b)
            cnt = count(lambda kc, c: jnp.where(kc == th, jnp.where(c * C + row < cand, 1.0, 0.0), 0.0))
            return jnp.where(cnt < need, cand, lim)

        last = lax.fori_loop(0, idx_bits, idx_body, jnp.zeros((1, C), I32))

        def drop_body(c, carry):
            kc = key_ref[c]
            key_ref[c] = jnp.where(kc == th, jnp.where(c * C + row > last, INT_MIN, kc), kc)
            return carry

        lax.fori_loop(0, nch, drop_body, 0)

    m_ref[...] = jnp.full(m_ref.shape, NEG_BIG, F32)
    s_ref[...] = jnp.zeros(s_ref.shape, F32)
    acc_ref[...] = jnp.zeros(acc_ref.shape, F32)

    def attend(c, near):
        neg_ref[...] = jnp.where(key_ref[c] >= th, 0.0, -jnp.inf)
        for h in range(ATTN_HEADS):
            p2 = (h // 2) * 2 * HEAD_DIM
            lg = jnp.dot(k_ref[0, c, :, p2:p2 + 2 * HEAD_DIM],
                         q2t_ref[0, 2 * HEAD_DIM * h:2 * HEAD_DIM * (h + 1), :],
                         preferred_element_type=F32)
            if near is not None:
                lg = lg + bias_ref[near, h]
            lg = lg + neg_ref[...]
            lg_ref[h] = lg
            m_old = m_ref[h:h + 1, :]
            m_new = jnp.maximum(m_old, jnp.max(lg, axis=0, keepdims=True))
            al_ref[h:h + 1, :] = jnp.exp(m_old - m_new)
            m_ref[h:h + 1, :] = m_new
        for h in range(ATTN_HEADS):
            p = jnp.exp(lg_ref[h] - m_ref[h:h + 1, :])
            s_ref[h:h + 1, :] = al_ref[h:h + 1, :] * s_ref[h:h + 1, :] + jnp.sum(p, axis=0, keepdims=True)
            p_ref[h] = p.astype(BF16)
        for h in range(ATTN_HEADS):
            hs = slice(h * HEAD_DIM, (h + 1) * HEAD_DIM)
            pv = jnp.dot(vt_ref[0, c, hs, :], p_ref[h], preferred_element_type=F32)
            acc_ref[hs, :] = al_ref[h:h + 1, :] * acc_ref[hs, :] + pv

    def far_body(c, carry):
        attend(c, None)
        return carry

    lax.fori_loop(0, jnp.maximum(i - 1, 0), far_body, 0)

    @pl.when(i >= 1)
    def _():
        attend(i - 1, 1)

    attend(i, 0)

    for h in range(ATTN_HEADS):
        hs = slice(h * HEAD_DIM, (h + 1) * HEAD_DIM)
        acc_ref[hs, :] = acc_ref[hs, :] / s_ref[h:h + 1, :]
    o_ref[0] = acc_ref[...].T.astype(o_ref.dtype)


def _dsa_attention(q2t, iqt, iwt, k, vt, ik, bias):
    B, S, _ = k.shape
    nq = S // CHUNK
    topk = min(TOPK_MAX, S // 4)
    k4 = k.reshape(B, nq, CHUNK, ATTN_WIDTH)
    ik4 = ik.reshape(B, nq, CHUNK, IDX_DIM)
    return pl.pallas_call(
        functools.partial(_attn_kernel, topk=topk, idx_bits=(S - 1).bit_length()),
        out_shape=jax.ShapeDtypeStruct((B, S, ATTN_WIDTH), BF16),
        grid=(B, nq),
        in_specs=[
            pl.BlockSpec((1, Q2_ROWS, CHUNK), lambda b, i: (b, 0, i)),
            pl.BlockSpec((1, IDX_HEADS * IDX_DIM, CHUNK), lambda b, i: (b, 0, i)),
            pl.BlockSpec((1, IDX_HEADS, CHUNK), lambda b, i: (b, 0, i)),
            pl.BlockSpec((1, nq, CHUNK, ATTN_WIDTH), lambda b, i: (b, 0, 0, 0)),
            pl.BlockSpec((1, nq, ATTN_WIDTH, CHUNK), lambda b, i: (b, 0, 0, 0)),
            pl.BlockSpec((1, nq, CHUNK, IDX_DIM), lambda b, i: (b, 0, 0, 0)),
            pl.BlockSpec((2, ATTN_HEADS, CHUNK, CHUNK), lambda b, i: (0, 0, 0, 0)),
        ],
        out_specs=pl.BlockSpec((1, CHUNK, ATTN_WIDTH), lambda b, i: (b, i, 0)),
        scratch_shapes=[
            pltpu.VMEM((nq, CHUNK, CHUNK), I32),
            pltpu.VMEM((CHUNK, CHUNK), F32),
            pltpu.VMEM((ATTN_HEADS, CHUNK, CHUNK), F32),
            pltpu.VMEM((ATTN_HEADS, CHUNK, CHUNK), BF16),
            pltpu.VMEM((ATTN_HEADS, CHUNK), F32),
            pltpu.VMEM((ATTN_HEADS, CHUNK), F32),
            pltpu.VMEM((ATTN_HEADS, CHUNK), F32),
            pltpu.VMEM((ATTN_WIDTH, CHUNK), F32),
        ],
        compiler_params=pltpu.CompilerParams(
            dimension_semantics=("arbitrary", "arbitrary"), vmem_limit_bytes=VMEM_LIMIT),
        name="dsa_attn",
    )(q2t, iqt, iwt, k4, vt, ik4, bias)


ROUTE_LANES = LANES


def _layer_norm(y, g, b):
    mu = jnp.mean(y, axis=-1, keepdims=True)
    yc = y - mu
    var = jnp.mean(yc * yc, axis=-1, keepdims=True)
    return yc * lax.rsqrt(var + LN_EPS) * g + b


def _mix_kernel(attn_ref, u_ref, halo_ref, x_ref, wpool_ref, pscale_ref, wout_ref, g_ref, b_ref,
                wr_ref, br_ref, h_ref, hb_ref, comb_ref, *, tm):
    i = pl.program_id(1)
    halo = jnp.where(i > 0, halo_ref[0], 0.0)
    ue = jnp.concatenate([halo, u_ref[0]], axis=0)
    pos = i * tm + lax.broadcasted_iota(I32, (tm, 1), 0)
    mixed = []
    for g, w in enumerate(POOL_WINDOWS):
        gs = slice(g * POOL_GROUP_DIM, (g + 1) * POOL_GROUP_DIM)
        ch = ue[:, gs]
        win = ch
        step = 1
        while step < w:
            win = win + pltpu.roll(win, step, axis=0)
            step *= 2
        cnt = jnp.minimum(pos + 1, w).astype(F32)
        pooled = win[POOL_HALO:] / cnt - ch[POOL_HALO:]
        mg = jnp.dot(pooled.astype(BF16), wpool_ref[g], preferred_element_type=F32)
        mixed.append((mg * pscale_ref[:, gs]).astype(BF16))
    cat = jnp.concatenate([attn_ref[0]] + mixed, axis=-1)
    mix = jnp.dot(cat, wout_ref[...], preferred_element_type=F32)
    h = _layer_norm(DEEPNORM_ALPHA * x_ref[0] + mix, g_ref[...], b_ref[...])
    h_ref[0] = h
    hb = h.astype(BF16)
    hb_ref[0] = hb

    lg = jnp.dot(hb, wr_ref[...], preferred_element_type=F32) + br_ref[...]
    lane = lax.broadcasted_iota(I32, (tm, ROUTE_LANES), 1)
    gl = jnp.where(lane >= N_EXPERTS, jnp.where(lane < N_EXPERTS + N_GROUPS, lg, -jnp.inf), -jnp.inf)
    ge = jnp.exp(gl - jnp.max(gl, axis=-1, keepdims=True))
    pg = ge / jnp.sum(ge, axis=-1, keepdims=True)
    pg_top = jnp.max(pg, axis=-1, keepdims=True)
    g_lane = jnp.min(jnp.where(pg == pg_top, lane, ROUTE_LANES), axis=-1, keepdims=True)
    e_lo = (g_lane - N_EXPERTS) * EXPERTS_PER_GROUP
    fl = jnp.where(lane >= e_lo, jnp.where(lane < e_lo + EXPERTS_PER_GROUP, lg, -jnp.inf), -jnp.inf)
    fe = jnp.exp(fl - jnp.max(fl, axis=-1, keepdims=True))
    pf = fe / jnp.sum(fe, axis=-1, keepdims=True)
    p1 = jnp.max(pf, axis=-1, keepdims=True)
    i1 = jnp.min(jnp.where(pf == p1, lane, ROUTE_LANES), axis=-1, keepdims=True)
    pr = jnp.where(lane == i1, -1.0, jnp.where(fl == -jnp.inf, -1.0, pf))
    p2 = jnp.max(pr, axis=-1, keepdims=True)
    i2 = jnp.min(jnp.where(pr == p2, lane, ROUTE_LANES), axis=-1, keepdims=True)
    psum = p1 + p2
    w1 = pg_top * p1 / psum
    w2 = pg_top * p2 / psum
    comb = jnp.where(lane == i1, w1, jnp.where(lane == i2, w2, 0.0))
    comb_ref[0] = comb[:, :N_EXPERTS]


def _mix_norm_route(attn, u, x, w_pool, pool_scale, w_out, ln_g, ln_b, w_r1, b_r1, w_r2, b_r2, *, tm=256):
    B, S, D = x.shape
    wr = jnp.concatenate([w_r2, w_r1, jnp.zeros((D, ROUTE_LANES - N_EXPERTS - N_GROUPS), w_r1.dtype)],
                         axis=1).astype(BF16)
    br = jnp.concatenate([b_r2, b_r1, jnp.zeros((ROUTE_LANES - N_EXPERTS - N_GROUPS,), b_r1.dtype)])[None, :]
    hpt = tm // POOL_HALO
    return pl.pallas_call(
        functools.partial(_mix_kernel, tm=tm),
        out_shape=(jax.ShapeDtypeStruct((B, S, D), F32),
                   jax.ShapeDtypeStruct((B, S, D), BF16),
                   jax.ShapeDtypeStruct((B, S, N_EXPERTS), F32)),
        grid=(B, S // tm),
        in_specs=[
            pl.BlockSpec((1, tm, ATTN_WIDTH), lambda b, i: (b, i, 0)),
            pl.BlockSpec((1, tm, POOL_WIDTH), lambda b, i: (b, i, 0)),
            pl.BlockSpec((1, POOL_HALO, POOL_WIDTH), lambda b, i: (b, jnp.maximum(i * hpt - 1, 0), 0)),
            pl.BlockSpec((1, tm, D), lambda b, i: (b, i, 0)),
            pl.BlockSpec((len(POOL_WINDOWS), POOL_GROUP_DIM, POOL_GROUP_DIM), lambda b, i: (0, 0, 0)),
            pl.BlockSpec((1, POOL_WIDTH), lambda b, i: (0, 0)),
            pl.BlockSpec((D, D), lambda b, i: (0, 0)),
            pl.BlockSpec((1, D), lambda b, i: (0, 0)),
            pl.BlockSpec((1, D), lambda b, i: (0, 0)),
            pl.BlockSpec((D, ROUTE_LANES), lambda b, i: (0, 0)),
            pl.BlockSpec((1, ROUTE_LANES), lambda b, i: (0, 0)),
        ],
        out_specs=(pl.BlockSpec((1, tm, D), lambda b, i: (b, i, 0)),
                   pl.BlockSpec((1, tm, D), lambda b, i: (b, i, 0)),
                   pl.BlockSpec((1, tm, N_EXPERTS), lambda b, i: (b, i, 0))),
        compiler_params=pltpu.CompilerParams(
            dimension_semantics=("arbitrary", "arbitrary"), vmem_limit_bytes=VMEM_LIMIT),
        name="mix_norm",
    )(attn, u, u, x, w_pool.astype(BF16), pool_scale[None, :], w_out.astype(BF16),
      ln_g[None, :], ln_b[None, :], wr, br)


def _moe_kernel(hb_ref, h_ref, comb_ref, wgu_ref, wd_ref, g_ref, b_ref, o_ref, acc_ref, *, d_expert):
    e = pl.program_id(1)

    @pl.when(e == 0)
    def _():
        acc_ref[...] = jnp.zeros(acc_ref.shape, F32)

    gu = jnp.dot(hb_ref[...], wgu_ref[0], preferred_element_type=F32)
    gate = gu[:, :d_expert]
    a = gate * jax.nn.sigmoid(gate) * gu[:, d_expert:]
    y = jnp.dot(a.astype(BF16), wd_ref[0], preferred_element_type=F32)
    comb = comb_ref[...]
    lane = lax.broadcasted_iota(I32, comb.shape, 1)
    ce = jnp.sum(jnp.where(lane == e, comb, 0.0), axis=-1, keepdims=True)
    acc_ref[...] += ce * y

    @pl.when(e == pl.num_programs(1) - 1)
    def _():
        o_ref[...] = _layer_norm(DEEPNORM_ALPHA * h_ref[...] + acc_ref[...], g_ref[...], b_ref[...])


def _moe_norm(hb, h, comb, w_gate, w_up, w_down, ln_g, ln_b, *, tm=1024):
    T, D = h.shape
    E, _, F = w_gate.shape
    wgu = jnp.concatenate([w_gate, w_up], axis=-1).astype(BF16)
    wd = w_down.astype(BF16)
    return pl.pallas_call(
        functools.partial(_moe_kernel, d_expert=F),
        out_shape=jax.ShapeDtypeStruct((T, D), F32),
        grid=(T // tm, E),
        in_specs=[
            pl.BlockSpec((tm, D), lambda t, e: (t, 0)),
            pl.BlockSpec((tm, D), lambda t, e: (t, 0)),
            pl.BlockSpec((tm, E), lambda t, e: (t, 0)),
            pl.BlockSpec((1, D, 2 * F), lambda t, e: (e, 0, 0)),
            pl.BlockSpec((1, F, D), lambda t, e: (e, 0, 0)),
            pl.BlockSpec((1, D), lambda t, e: (0, 0)),
            pl.BlockSpec((1, D), lambda t, e: (0, 0)),
        ],
        out_specs=pl.BlockSpec((tm, D), lambda t, e: (t, 0)),
        scratch_shapes=[pltpu.VMEM((tm, D), F32)],
        compiler_params=pltpu.CompilerParams(
            dimension_semantics=("arbitrary", "arbitrary"), vmem_limit_bytes=VMEM_LIMIT),
        name="moe_norm",
    )(hb, h, comb, wgu, wd, ln_g[None, :], ln_b[None, :])


def kernel(x, w_in, w_pool, pool_scale, w_out, rel_bias, ln1_g, ln1_b, w_r1, b_r1, w_r2, b_r2,
           w_gate, w_up, w_down, ln2_g, ln2_b):
    B, S, D = x.shape
    assert w_in.shape[0] == DEPTH == 1 and S % CHUNK == 0
    bias = _bias_tiles(rel_bias)
    q2t, vt, iqt, iwt, k, ik, u = _in_proj(x, w_in[0])
    attn = _dsa_attention(q2t, iqt, iwt, k, vt, ik, bias)
    h, hb, comb = _mix_norm_route(attn, u, x, w_pool[0], pool_scale[0], w_out[0], ln1_g[0], ln1_b[0],
                                  w_r1[0], b_r1[0], w_r2[0], b_r2[0])
    out = _moe_norm(hb.reshape(B * S, D), h.reshape(B * S, D), comb.reshape(B * S, N_EXPERTS),
                    w_gate[0], w_up[0], w_down[0], ln2_g[0], ln2_b[0])
    return out.reshape(B, S, D)
```

```python
import functools
import math

import numpy as np
import jax
import jax.numpy as jnp
from jax import lax
from jax.experimental import pallas as pl
from jax.experimental.pallas import tpu as pltpu

F32 = jnp.float32
BF16 = jnp.bfloat16
I32 = jnp.int32

ATTN_HEADS = 8
HEAD_DIM = 64
ATTN_WIDTH = ATTN_HEADS * HEAD_DIM
IDX_HEADS = 8
IDX_DIM = 64
TOPK_MAX = 256
POOL_WINDOWS = (2, 4, 8, 16)
POOL_GROUP_DIM = 128
POOL_WIDTH = len(POOL_WINDOWS) * POOL_GROUP_DIM
POOL_HALO = 16
REL_BUCKETS = 32
REL_MAX_DIST = 128
N_GROUPS = 4
EXPERTS_PER_GROUP = 8
N_EXPERTS = N_GROUPS * EXPERTS_PER_GROUP
LN_EPS = 1e-5
DEPTH = 1
DEEPNORM_ALPHA = (2 * DEPTH) ** 0.25

LANES = 128
CHUNK = 256
INT_MIN = -2 ** 31
NEG_BIG = -1e30
VMEM_LIMIT = 56 * 1024 * 1024


def _rel_bucket_table(n):
    max_exact = REL_BUCKETS // 2
    d = np.arange(n)
    nf = np.maximum(d, 1).astype(np.float32)
    ratio = np.log(nf / np.float32(max_exact)) / np.float32(math.log(REL_MAX_DIST / max_exact))
    large = max_exact + (ratio * np.float32(REL_BUCKETS - max_exact)).astype(np.int32)
    large = np.minimum(large, REL_BUCKETS - 1)
    return np.where(d < max_exact, d, large).astype(np.int32)


def _near_bucket_tiles():
    tbl = _rel_bucket_table(2 * CHUNK)
    a = np.arange(CHUNK)[:, None]
    b = np.arange(CHUNK)[None, :]
    tiles = [tbl[np.maximum(delta * CHUNK + b - a, 0)] for delta in (0, 1)]
    return np.stack(tiles).astype(np.int32)


FAR_BUCKET = REL_BUCKETS - 1
assert int(_rel_bucket_table(2 * CHUNK)[CHUNK + 1:].min()) == FAR_BUCKET


def _bias_kernel(rb_ref, bucket_ref, o_ref):
    h = pl.program_id(1)
    bk = bucket_ref[0]
    far = rb_ref[FAR_BUCKET, h]
    acc = jnp.zeros(bk.shape, F32)
    for n in range(REL_BUCKETS):
        acc = jnp.where(bk == n, rb_ref[n, h] - far, acc)
    o_ref[0, 0] = acc


def _bias_tiles(rel_bias):
    buckets = jnp.asarray(_near_bucket_tiles())
    return pl.pallas_call(
        _bias_kernel,
        out_shape=jax.ShapeDtypeStruct((2, ATTN_HEADS, CHUNK, CHUNK), F32),
        grid=(2, ATTN_HEADS),
        in_specs=[pl.BlockSpec(memory_space=pltpu.SMEM),
                  pl.BlockSpec((1, CHUNK, CHUNK), lambda d, h: (d, 0, 0))],
        out_specs=pl.BlockSpec((1, 1, CHUNK, CHUNK), lambda d, h: (d, h, 0, 0)),
        name="bias_tiles",
    )(rel_bias, buckets)


Q2_ROWS = 2 * ATTN_WIDTH
WT_Q2, WT_V, WT_IQ, WT_IW = 0, Q2_ROWS, Q2_ROWS + ATTN_WIDTH, Q2_ROWS + 2 * ATTN_WIDTH
WT_ROWS = WT_IW + 16
WN_K, WN_U, WN_IK = 0, ATTN_WIDTH, ATTN_WIDTH + POOL_WIDTH
WN_COLS = WN_IK + LANES


def _proj_kernel(x_ref, wt_ref, wn_ref, q2t_ref, vt_ref, iqt_ref, iwt_ref, k_ref, ik_ref, u_ref, *, tm):
    xb = x_ref[0].astype(BF16)
    t = lax.dot_general(wt_ref[...], xb, (((1,), (1,)), ((), ())),
                        preferred_element_type=F32)
    q2t_ref[0] = (t[WT_Q2:WT_V] * (HEAD_DIM ** -0.5)).astype(BF16)
    iqt_ref[0] = (t[WT_IQ:WT_IW] * (IDX_DIM ** -0.5)).astype(BF16)
    iwt_ref[0] = t[WT_IW:WT_IW + IDX_HEADS] * (IDX_HEADS ** -0.5)
    for j in range(tm // CHUNK):
        vt_ref[0, j] = t[WT_V:WT_IQ, j * CHUNK:(j + 1) * CHUNK].astype(BF16)
    n = jnp.dot(xb, wn_ref[...], preferred_element_type=F32)
    k_ref[0] = n[:, WN_K:WN_U].astype(BF16)
    u_ref[0] = n[:, WN_U:WN_IK]
    ik_ref[0] = n[:, WN_IK:WN_IK + IDX_DIM].astype(BF16)


def _in_proj(x, w_in, *, tm=512):
    B, S, D = x.shape
    q_off, k_off, v_off = 0, ATTN_WIDTH, 2 * ATTN_WIDTH
    iq_off = 3 * ATTN_WIDTH
    ik_off = iq_off + IDX_HEADS * IDX_DIM
    iw_off = ik_off + IDX_DIM
    p_off = iw_off + IDX_HEADS
    wq = w_in[:, q_off:k_off].T.reshape(ATTN_HEADS // 2, 2, HEAD_DIM, D)
    zq = jnp.zeros((ATTN_HEADS // 2, HEAD_DIM, D), w_in.dtype)
    wq2 = jnp.stack([wq[:, 0], zq, zq, wq[:, 1]], axis=1).reshape(Q2_ROWS, D)
    wt = jnp.concatenate([
        wq2, w_in[:, v_off:iq_off].T, w_in[:, iq_off:ik_off].T, w_in[:, iw_off:p_off].T,
        jnp.zeros((WT_ROWS - WT_IW - IDX_HEADS, D), w_in.dtype)], axis=0).astype(BF16)
    wn = jnp.concatenate([
        w_in[:, k_off:v_off], w_in[:, p_off:], w_in[:, ik_off:iw_off],
        jnp.zeros((D, WN_COLS - WN_IK - IDX_DIM), w_in.dtype)], axis=1).astype(BF16)
    nt = S // tm
    cpt = tm // CHUNK
    outs = pl.pallas_call(
        functools.partial(_proj_kernel, tm=tm),
        out_shape=(
            jax.ShapeDtypeStruct((B, Q2_ROWS, S), BF16),
            jax.ShapeDtypeStruct((B, S // CHUNK, ATTN_WIDTH, CHUNK), BF16),
            jax.ShapeDtypeStruct((B, IDX_HEADS * IDX_DIM, S), BF16),
            jax.ShapeDtypeStruct((B, IDX_HEADS, S), F32),
            jax.ShapeDtypeStruct((B, S, ATTN_WIDTH), BF16),
            jax.ShapeDtypeStruct((B, S, IDX_DIM), BF16),
            jax.ShapeDtypeStruct((B, S, POOL_WIDTH), F32),
        ),
        grid=(B, nt),
        in_specs=[pl.BlockSpec((1, tm, D), lambda b, i: (b, i, 0)),
                  pl.BlockSpec((WT_ROWS, D), lambda b, i: (0, 0)),
                  pl.BlockSpec((D, WN_COLS), lambda b, i: (0, 0))],
        out_specs=(
            pl.BlockSpec((1, Q2_ROWS, tm), lambda b, i: (b, 0, i)),
            pl.BlockSpec((1, cpt, ATTN_WIDTH, CHUNK), lambda b, i: (b, i, 0, 0)),
            pl.BlockSpec((1, IDX_HEADS * IDX_DIM, tm), lambda b, i: (b, 0, i)),
            pl.BlockSpec((1, IDX_HEADS, tm), lambda b, i: (b, 0, i)),
            pl.BlockSpec((1, tm, ATTN_WIDTH), lambda b, i: (b, i, 0)),
            pl.BlockSpec((1, tm, IDX_DIM), lambda b, i: (b, i, 0)),
            pl.BlockSpec((1, tm, POOL_WIDTH), lambda b, i: (b, i, 0)),
        ),
        compiler_params=pltpu.CompilerParams(
            dimension_semantics=("arbitrary", "arbitrary"), vmem_limit_bytes=VMEM_LIMIT),
        name="in_proj",
    )(x, wt, wn)
    return outs


SNAP_FIRST = 14
SNAP_EVERY = 5
MAX_STEPS = SNAP_FIRST + 5 * 17
ROW_LO, ROW_HI, ROW_CNT, ROW_PROBE = 0, 1, 2, 3


def _order_key(bits):
    return bits ^ ((bits >> 31) & 0x7FFFFFFF)


def _attn_kernel(q2t_ref, iqt_ref, iwt_ref, k_ref, vt_ref, ik_ref, bias_ref, o_ref,
                 sc_ref, st_ref, neg_ref, lg_ref, p_ref, m_ref, al_ref, s_ref, acc_ref,
                 *, topk, idx_bits):
    i = pl.program_id(1)
    nch = i + 1
    C = CHUNK
    row = lax.broadcasted_iota(I32, (C, C), 0)
    col = lax.broadcasted_iota(I32, (C, C), 1)

    def fold_rows(op, x):
        return op(x.reshape(C // 8, 8, C), axis=0)

    def score_body(c, carry):
        smin, smax = carry
        ikc = ik_ref[0, c]
        s = jnp.zeros((C, C), F32)
        for j in range(IDX_HEADS):
            d = jnp.dot(ikc, iqt_ref[0, j * IDX_DIM:(j + 1) * IDX_DIM, :],
                        preferred_element_type=F32)
            s = s + iwt_ref[0, j:j + 1, :] * jnp.maximum(d, 0.0)
        causal = (c * C + row) <= (i * C + col)
        sc_ref[c] = jnp.where(causal, s, -jnp.inf)
        smin = jnp.minimum(smin, fold_rows(jnp.min, jnp.where(causal, s, jnp.inf)))
        smax = jnp.maximum(smax, fold_rows(jnp.max, jnp.where(causal, s, -jnp.inf)))
        return smin, smax

    smin, smax = lax.fori_loop(0, nch, score_body,
                               (jnp.full((8, C), jnp.inf, F32), jnp.full((8, C), -jnp.inf, F32)))
    smin = jnp.min(smin, axis=0, keepdims=True)
    smax = jnp.max(smax, axis=0, keepdims=True)

    pos = i * C + lax.broadcasted_iota(I32, (1, C), 1)
    n_keys = (pos + 1).astype(F32)
    k_eff = jnp.minimum(pos + 1, topk).astype(F32)

    def count(indicator):
        def body(c, acc):
            return acc + fold_rows(jnp.sum, indicator(sc_ref[c], c))
        part = lax.fori_loop(0, nch, body, jnp.zeros((8, C), F32))
        return jnp.sum(part, axis=0, keepdims=True)

    def load_state():
        return (st_ref[ROW_LO:ROW_LO + 1, :], st_ref[ROW_HI:ROW_HI + 1, :],
                st_ref[ROW_CNT:ROW_CNT + 1, :], st_ref[ROW_PROBE:ROW_PROBE + 1, :])

    def store_state(lo, hi, cnt_lo, probe):
        st_ref[ROW_LO:ROW_LO + 1, :] = lo
        st_ref[ROW_HI:ROW_HI + 1, :] = hi
        st_ref[ROW_CNT:ROW_CNT + 1, :] = cnt_lo
        st_ref[ROW_PROBE:ROW_PROBE + 1, :] = probe

    def open_cols(lo, hi, cnt_lo):
        return jnp.where(cnt_lo > k_eff, jnp.where(hi > lo, 1.0, 0.0), 0.0)

    def bisect(it, lo, hi, cnt_lo, probe):
        klo = _order_key(pltpu.bitcast(lo, I32))
        khi = _order_key(pltpu.bitcast(hi, I32))
        i_mid = pltpu.bitcast(_order_key((klo >> 1) + (khi >> 1) + (klo & khi & 1)), F32)
        mid = jnp.where(jnp.logical_and(it > SNAP_FIRST, it % 2 == 1), i_mid, 0.5 * lo + 0.5 * hi)
        probing = probe > lo
        mid = jnp.where(probing, probe, mid)
        ok = open_cols(lo, hi, cnt_lo) * jnp.where(mid > lo, jnp.where(mid < hi, 1.0, 0.0), 0.0)
        cnt = count(lambda sc, c: jnp.where(sc >= mid, 1.0, 0.0))
        up = ok * jnp.where(cnt >= k_eff, 1.0, 0.0)
        dn = ok - up
        new_hi = jnp.where(dn + up * jnp.where(probing, 1.0, 0.0) > 0.0, mid, hi)
        return (jnp.where(up > 0.0, mid, lo), new_hi, jnp.where(up > 0.0, cnt, cnt_lo),
                jnp.full((1, C), -jnp.inf, F32))

    def snap(it, lo, hi, cnt_lo, probe):
        def body(c, carry):
            vmin, vmax = carry
            sc = sc_ref[c]
            vmin = jnp.minimum(vmin, fold_rows(jnp.min, jnp.where(sc >= lo, sc, jnp.inf)))
            vmax = jnp.maximum(vmax, fold_rows(jnp.max, jnp.where(sc < hi, sc, -jnp.inf)))
            return vmin, vmax
        vmin, vmax = lax.fori_loop(0, nch, body, (jnp.full((8, C), jnp.inf, F32),
                                                  jnp.full((8, C), -jnp.inf, F32)))
        vmin = jnp.min(vmin, axis=0, keepdims=True)
        vmax = jnp.max(vmax, axis=0, keepdims=True)
        is_open = open_cols(lo, hi, cnt_lo) > 0.0
        single = vmin >= vmax
        return (jnp.where(is_open, vmin, lo),
                jnp.where(is_open, jnp.where(single, vmin, hi), hi),
                cnt_lo,
                jnp.where(is_open, jnp.where(single, -jnp.inf, vmax), -jnp.inf))

    def search_cond(st):
        it, n_open = st
        return jnp.logical_and(it < MAX_STEPS, n_open > 0.0)

    def search_body(st):
        it, _ = st
        do_snap = jnp.logical_and(it >= SNAP_FIRST, (it - SNAP_FIRST) % SNAP_EVERY == 0)

        @pl.when(do_snap)
        def _():
            store_state(*snap(it, *load_state()))

        @pl.when(jnp.logical_not(do_snap))
        def _():
            store_state(*bisect(it, *load_state()))

        lo, hi, cnt_lo, _ = load_state()
        return it + 1, jnp.max(open_cols(lo, hi, cnt_lo))

    kmax = _order_key(pltpu.bitcast(smax, I32))
    hi0 = pltpu.bitcast(_order_key(jnp.where(kmax == 2 ** 31 - 1, kmax, kmax + 1)), F32)
    store_state(smin, hi0, n_keys, jnp.full((1, C), -jnp.inf, F32))
    lax.while_loop(search_cond, search_body, (jnp.int32(0), jnp.max(open_cols(smin, hi0, n_keys))))
    th, _, cnt_th, _ = load_state()

    @pl.when(jnp.max(cnt_th - k_eff) > 0.0)
    def _():
        need = k_eff - count(lambda sc, c: jnp.where(sc > th, 1.0, 0.0))

        def idx_body(b, lim):
            cand = lim + lax.shift_left(jnp.int32(1), idx_bits - 1 - b)
            cnt = count(lambda sc, c: jnp.where(sc == th, jnp.where(c * C + row < cand, 1.0, 0.0), 0.0))
            return jnp.where(cnt < need, cand, lim)

        last = lax.fori_loop(0, idx_bits, idx_body, jnp.zeros((1, C), I32))

        def drop_body(c, carry):
            sc = sc_ref[c]
            sc_ref[c] = jnp.where(sc == th, jnp.where(c * C + row > last, -jnp.inf, sc), sc)
            return carry

        lax.fori_loop(0, nch, drop_body, 0)

    m_ref[...] = jnp.full(m_ref.shape, NEG_BIG, F32)
    s_ref[...] = jnp.zeros(s_ref.shape, F32)
    acc_ref[...] = jnp.zeros(acc_ref.shape, F32)

    def attend(c, near):
        neg = jnp.where(sc_ref[c] >= th, 0.0, -jnp.inf)
        if near == 0:
            neg = jnp.where(row <= col, neg, -jnp.inf)
        neg_ref[...] = neg
        for h in range(ATTN_HEADS):
            p2 = (h // 2) * 2 * HEAD_DIM
            lg = jnp.dot(k_ref[0, c, :, p2:p2 + 2 * HEAD_DIM],
                         q2t_ref[0, 2 * HEAD_DIM * h:2 * HEAD_DIM * (h + 1), :],
                         preferred_element_type=F32)
            if near is not None:
                lg = lg + bias_ref[near, h]
            lg = lg + neg_ref[...]
            lg_ref[h] = lg
            m_old = m_ref[h:h + 1, :]
            m_new = jnp.maximum(m_old, jnp.max(lg, axis=0, keepdims=True))
            al_ref[h:h + 1, :] = jnp.exp(m_old - m_new)
            m_ref[h:h + 1, :] = m_new
        for h in range(ATTN_HEADS):
            p = jnp.exp(lg_ref[h] - m_ref[h:h + 1, :])
            s_ref[h:h + 1, :] = al_ref[h:h + 1, :] * s_ref[h:h + 1, :] + jnp.sum(p, axis=0, keepdims=True)
            p_ref[h] = p.astype(BF16)
        for h in range(ATTN_HEADS):
            hs = slice(h * HEAD_DIM, (h + 1) * HEAD_DIM)
            pv = jnp.dot(vt_ref[0, c, hs, :], p_ref[h], preferred_element_type=F32)
            acc_ref[hs, :] = al_ref[h:h + 1, :] * acc_ref[hs, :] + pv

    def far_body(c, carry):
        attend(c, None)
        return carry

    lax.fori_loop(0, jnp.maximum(i - 1, 0), far_body, 0)

    @pl.when(i >= 1)
    def _():
        attend(i - 1, 1)

    attend(i, 0)

    for h in range(ATTN_HEADS):
        hs = slice(h * HEAD_DIM, (h + 1) * HEAD_DIM)
        acc_ref[hs, :] = acc_ref[hs, :] / s_ref[h:h + 1, :]
    o_ref[0] = acc_ref[...].T.astype(o_ref.dtype)


def _dsa_attention(q2t, iqt, iwt, k, vt, ik, bias):
    B, S, _ = k.shape
    nq = S // CHUNK
    topk = min(TOPK_MAX, S // 4)
    k4 = k.reshape(B, nq, CHUNK, ATTN_WIDTH)
    ik4 = ik.reshape(B, nq, CHUNK, IDX_DIM)
    return pl.pallas_call(
        functools.partial(_attn_kernel, topk=topk, idx_bits=(S - 1).bit_length()),
        out_shape=jax.ShapeDtypeStruct((B, S, ATTN_WIDTH), BF16),
        grid=(B, nq),
        in_specs=[
            pl.BlockSpec((1, Q2_ROWS, CHUNK), lambda b, i: (b, 0, i)),
            pl.BlockSpec((1, IDX_HEADS * IDX_DIM, CHUNK), lambda b, i: (b, 0, i)),
            pl.BlockSpec((1, IDX_HEADS, CHUNK), lambda b, i: (b, 0, i)),
            pl.BlockSpec((1, nq, CHUNK, ATTN_WIDTH), lambda b, i: (b, 0, 0, 0)),
            pl.BlockSpec((1, nq, ATTN_WIDTH, CHUNK), lambda b, i: (b, 0, 0, 0)),
            pl.BlockSpec((1, nq, CHUNK, IDX_DIM), lambda b, i: (b, 0, 0, 0)),
            pl.BlockSpec((2, ATTN_HEADS, CHUNK, CHUNK), lambda b, i: (0, 0, 0, 0)),
        ],
        out_specs=pl.BlockSpec((1, CHUNK, ATTN_WIDTH), lambda b, i: (b, i, 0)),
        scratch_shapes=[
            pltpu.VMEM((nq, CHUNK, CHUNK), F32),
            pltpu.VMEM((8, CHUNK), F32),
            pltpu.VMEM((CHUNK, CHUNK), F32),
            pltpu.VMEM((ATTN_HEADS, CHUNK, CHUNK), F32),
            pltpu.VMEM((ATTN_HEADS, CHUNK, CHUNK), BF16),
            pltpu.VMEM((ATTN_HEADS, CHUNK), F32),
            pltpu.VMEM((ATTN_HEADS, CHUNK), F32),
            pltpu.VMEM((ATTN_HEADS, CHUNK), F32),
            pltpu.VMEM((ATTN_WIDTH, CHUNK), F32),
        ],
        compiler_params=pltpu.CompilerParams(
            dimension_semantics=("arbitrary", "arbitrary"), vmem_limit_bytes=VMEM_LIMIT),
        name="dsa_attn",
    )(q2t, iqt, iwt, k4, vt, ik4, bias)


ROUTE_LANES = LANES


def _layer_norm(y, g, b):
    mu = jnp.mean(y, axis=-1, keepdims=True)
    yc = y - mu
    var = jnp.mean(yc * yc, axis=-1, keepdims=True)
    return yc * lax.rsqrt(var + LN_EPS) * g + b


def _mix_kernel(attn_ref, u_ref, halo_ref, x_ref, wpool_ref, pscale_ref, wout_ref, g_ref, b_ref,
                wr_ref, br_ref, h_ref, hb_ref, comb_ref, *, tm):
    i = pl.program_id(1)
    halo = jnp.where(i > 0, halo_ref[0], 0.0)
    ue = jnp.concatenate([halo, u_ref[0]], axis=0)
    pos = i * tm + lax.broadcasted_iota(I32, (tm, 1), 0)
    mixed = []
    for g, w in enumerate(POOL_WINDOWS):
        gs = slice(g * POOL_GROUP_DIM, (g + 1) * POOL_GROUP_DIM)
        ch = ue[:, gs]
        win = ch
        step = 1
        while step < w:
            win = win + pltpu.roll(win, step, axis=0)
            step *= 2
        cnt = jnp.minimum(pos + 1, w).astype(F32)
        pooled = win[POOL_HALO:] / cnt - ch[POOL_HALO:]
        mg = jnp.dot(pooled.astype(BF16), wpool_ref[g], preferred_element_type=F32)
        mixed.append((mg * pscale_ref[:, gs]).astype(BF16))
    cat = jnp.concatenate([attn_ref[0]] + mixed, axis=-1)
    mix = jnp.dot(cat, wout_ref[...], preferred_element_type=F32)
    h = _layer_norm(DEEPNORM_ALPHA * x_ref[0] + mix, g_ref[...], b_ref[...])
    h_ref[0] = h
    hb = h.astype(BF16)
    hb_ref[0] = hb

    lg = jnp.dot(hb, wr_ref[...], preferred_element_type=F32) + br_ref[...]
    lane = lax.broadcasted_iota(I32, (tm, ROUTE_LANES), 1)
    gl = jnp.where(lane >= N_EXPERTS, jnp.where(lane < N_EXPERTS + N_GROUPS, lg, -jnp.inf), -jnp.inf)
    ge = jnp.exp(gl - jnp.max(gl, axis=-1, keepdims=True))
    pg = ge / jnp.sum(ge, axis=-1, keepdims=True)
    pg_top = jnp.max(pg, axis=-1, keepdims=True)
    g_lane = jnp.min(jnp.where(pg == pg_top, lane, ROUTE_LANES), axis=-1, keepdims=True)
    e_lo = (g_lane - N_EXPERTS) * EXPERTS_PER_GROUP
    fl = jnp.where(lane >= e_lo, jnp.where(lane < e_lo + EXPERTS_PER_GROUP, lg, -jnp.inf), -jnp.inf)
    fe = jnp.exp(fl - jnp.max(fl, axis=-1, keepdims=True))
    pf = fe / jnp.sum(fe, axis=-1, keepdims=True)
    p1 = jnp.max(pf, axis=-1, keepdims=True)
    i1 = jnp.min(jnp.where(pf == p1, lane, ROUTE_LANES), axis=-1, keepdims=True)
    pr = jnp.where(lane == i1, -1.0, jnp.where(fl == -jnp.inf, -1.0, pf))
    p2 = jnp.max(pr, axis=-1, keepdims=True)
    i2 = jnp.min(jnp.where(pr == p2, lane, ROUTE_LANES), axis=-1, keepdims=True)
    psum = p1 + p2
    w1 = pg_top * p1 / psum
    w2 = pg_top * p2 / psum
    comb = jnp.where(lane == i1, w1, jnp.where(lane == i2, w2, 0.0))
    comb_ref[0] = comb[:, :N_EXPERTS]


def _mix_norm_route(attn, u, x, w_pool, pool_scale, w_out, ln_g, ln_b, w_r1, b_r1, w_r2, b_r2, *, tm=256):
    B, S, D = x.shape
    wr = jnp.concatenate([w_r2, w_r1, jnp.zeros((D, ROUTE_LANES - N_EXPERTS - N_GROUPS), w_r1.dtype)],
                         axis=1).astype(BF16)
    br = jnp.concatenate([b_r2, b_r1, jnp.zeros((ROUTE_LANES - N_EXPERTS - N_GROUPS,), b_r1.dtype)])[None, :]
    hpt = tm // POOL_HALO
    return pl.pallas_call(
        functools.partial(_mix_kernel, tm=tm),
        out_shape=(jax.ShapeDtypeStruct((B, S, D), F32),
                   jax.ShapeDtypeStruct((B, S, D), BF16),
                   jax.ShapeDtypeStruct((B, S, N_EXPERTS), F32)),
        grid=(B, S // tm),
        in_specs=[
            pl.BlockSpec((1, tm, ATTN_WIDTH), lambda b, i: (b, i, 0)),
            pl.BlockSpec((1, tm, POOL_WIDTH), lambda b, i: (b, i, 0)),
            pl.BlockSpec((1, POOL_HALO, POOL_WIDTH), lambda b, i: (b, jnp.maximum(i * hpt - 1, 0), 0)),
            pl.BlockSpec((1, tm, D), lambda b, i: (b, i, 0)),
            pl.BlockSpec((len(POOL_WINDOWS), POOL_GROUP_DIM, POOL_GROUP_DIM), lambda b, i: (0, 0, 0)),
            pl.BlockSpec((1, POOL_WIDTH), lambda b, i: (0, 0)),
            pl.BlockSpec((D, D), lambda b, i: (0, 0)),
            pl.BlockSpec((1, D), lambda b, i: (0, 0)),
            pl.BlockSpec((1, D), lambda b, i: (0, 0)),
            pl.BlockSpec((D, ROUTE_LANES), lambda b, i: (0, 0)),
            pl.BlockSpec((1, ROUTE_LANES), lambda b, i: (0, 0)),
        ],
        out_specs=(pl.BlockSpec((1, tm, D), lambda b, i: (b, i, 0)),
                   pl.BlockSpec((1, tm, D), lambda b, i: (b, i, 0)),
                   pl.BlockSpec((1, tm, N_EXPERTS), lambda b, i: (b, i, 0))),
        compiler_params=pltpu.CompilerParams(
            dimension_semantics=("arbitrary", "arbitrary"), vmem_limit_bytes=VMEM_LIMIT),
        name="mix_norm",
    )(attn, u, u, x, w_pool.astype(BF16), pool_scale[None, :], w_out.astype(BF16),
      ln_g[None, :], ln_b[None, :], wr, br)


def _moe_kernel(hb_ref, h_ref, comb_ref, wgu_ref, wd_ref, g_ref, b_ref, o_ref, acc_ref, *, d_expert):
    e = pl.program_id(1)

    @pl.when(e == 0)
    def _():
        acc_ref[...] = jnp.zeros(acc_ref.shape, F32)

    gu = jnp.dot(hb_ref[...], wgu_ref[0], preferred_element_type=F32)
    gate = gu[:, :d_expert]
    a = gate * jax.nn.sigmoid(gate) * gu[:, d_expert:]
    y = jnp.dot(a.astype(BF16), wd_ref[0], preferred_element_type=F32)
    comb = comb_ref[...]
    lane = lax.broadcasted_iota(I32, comb.shape, 1)
    ce = jnp.sum(jnp.where(lane == e, comb, 0.0), axis=-1, keepdims=True)
    acc_ref[...] += ce * y

    @pl.when(e == pl.num_programs(1) - 1)
    def _():
        o_ref[...] = _layer_norm(DEEPNORM_ALPHA * h_ref[...] + acc_ref[...], g_ref[...], b_ref[...])


def _moe_norm(hb, h, comb, w_gate, w_up, w_down, ln_g, ln_b, *, tm=1024):
    T, D = h.shape
    E, _, F = w_gate.shape
    wgu = jnp.concatenate([w_gate, w_up], axis=-1).astype(BF16)
    wd = w_down.astype(BF16)
    return pl.pallas_call(
        functools.partial(_moe_kernel, d_expert=F),
        out_shape=jax.ShapeDtypeStruct((T, D), F32),
        grid=(T // tm, E),
        in_specs=[
            pl.BlockSpec((tm, D), lambda t, e: (t, 0)),
            pl.BlockSpec((tm, D), lambda t, e: (t, 0)),
            pl.BlockSpec((tm, E), lambda t, e: (t, 0)),
            pl.BlockSpec((1, D, 2 * F), lambda t, e: (e, 0, 0)),
            pl.BlockSpec((1, F, D), lambda t, e: (e, 0, 0)),
            pl.BlockSpec((1, D), lambda t, e: (0, 0)),
            pl.BlockSpec((1, D), lambda t, e: (0, 0)),
        ],
        out_specs=pl.BlockSpec((tm, D), lambda t, e: (t, 0)),
        scratch_shapes=[pltpu.VMEM((tm, D), F32)],
        compiler_params=pltpu.CompilerParams(
            dimension_semantics=("arbitrary", "arbitrary"), vmem_limit_bytes=VMEM_LIMIT),
        name="moe_norm",
    )(hb, h, comb, wgu, wd, ln_g[None, :], ln_b[None, :])


def kernel(x, w_in, w_pool, pool_scale, w_out, rel_bias, ln1_g, ln1_b, w_r1, b_r1, w_r2, b_r2,
           w_gate, w_up, w_down, ln2_g, ln2_b):
    B, S, D = x.shape
    assert w_in.shape[0] == DEPTH == 1 and S % CHUNK == 0
    bias = _bias_tiles(rel_bias)
    q2t, vt, iqt, iwt, k, ik, u = _in_proj(x, w_in[0])
    attn = _dsa_attention(q2t, iqt, iwt, k, vt, ik, bias)
    h, hb, comb = _mix_norm_route(attn, u, x, w_pool[0], pool_scale[0], w_out[0], ln1_g[0], ln1_b[0],
                                  w_r1[0], b_r1[0], w_r2[0], b_r2[0])
    out = _moe_norm(hb.reshape(B * S, D), h.reshape(B * S, D), comb.reshape(B * S, N_EXPERTS),
                    w_gate[0], w_up[0], w_down[0], ln2_g[0], ln2_b[0])
    return out.reshape(B, S, D)
```

```python
import functools
import math

import numpy as np
import jax
import jax.numpy as jnp
from jax import lax
from jax.experimental import pallas as pl
from jax.experimental.pallas import tpu as pltpu

F32 = jnp.float32
BF16 = jnp.bfloat16
I32 = jnp.int32

ATTN_HEADS = 8
HEAD_DIM = 64
ATTN_WIDTH = ATTN_HEADS * HEAD_DIM
IDX_HEADS = 8
IDX_DIM = 64
TOPK_MAX = 256
POOL_WINDOWS = (2, 4, 8, 16)
POOL_GROUP_DIM = 128
POOL_WIDTH = len(POOL_WINDOWS) * POOL_GROUP_DIM
POOL_HALO = 16
REL_BUCKETS = 32
REL_MAX_DIST = 128
N_GROUPS = 4
EXPERTS_PER_GROUP = 8
N_EXPERTS = N_GROUPS * EXPERTS_PER_GROUP
LN_EPS = 1e-5
DEPTH = 1
DEEPNORM_ALPHA = (2 * DEPTH) ** 0.25

LANES = 128
CHUNK = 256
INT_MIN = -2 ** 31
NEG_BIG = -1e30
VMEM_LIMIT = 56 * 1024 * 1024


def _rel_bucket_table(n):
    max_exact = REL_BUCKETS // 2
    d = np.arange(n)
    nf = np.maximum(d, 1).astype(np.float32)
    ratio = np.log(nf / np.float32(max_exact)) / np.float32(math.log(REL_MAX_DIST / max_exact))
    large = max_exact + (ratio * np.float32(REL_BUCKETS - max_exact)).astype(np.int32)
    large = np.minimum(large, REL_BUCKETS - 1)
    return np.where(d < max_exact, d, large).astype(np.int32)


def _near_bucket_tiles():
    tbl = _rel_bucket_table(2 * CHUNK)
    a = np.arange(CHUNK)[:, None]
    b = np.arange(CHUNK)[None, :]
    tiles = [tbl[np.maximum(delta * CHUNK + b - a, 0)] for delta in (0, 1)]
    return np.stack(tiles).astype(np.int32)


FAR_BUCKET = REL_BUCKETS - 1
assert int(_rel_bucket_table(2 * CHUNK)[CHUNK + 1:].min()) == FAR_BUCKET


def _bias_kernel(rb_ref, bucket_ref, o_ref):
    h = pl.program_id(1)
    bk = bucket_ref[0]
    far = rb_ref[FAR_BUCKET, h]
    acc = jnp.zeros(bk.shape, F32)
    for n in range(REL_BUCKETS):
        acc = jnp.where(bk == n, rb_ref[n, h] - far, acc)
    o_ref[0, 0] = acc


def _bias_tiles(rel_bias):
    buckets = jnp.asarray(_near_bucket_tiles())
    return pl.pallas_call(
        _bias_kernel,
        out_shape=jax.ShapeDtypeStruct((2, ATTN_HEADS, CHUNK, CHUNK), F32),
        grid=(2, ATTN_HEADS),
        in_specs=[pl.BlockSpec(memory_space=pltpu.SMEM),
                  pl.BlockSpec((1, CHUNK, CHUNK), lambda d, h: (d, 0, 0))],
        out_specs=pl.BlockSpec((1, 1, CHUNK, CHUNK), lambda d, h: (d, h, 0, 0)),
        name="bias_tiles",
    )(rel_bias, buckets)


Q2_ROWS = 2 * ATTN_WIDTH
WT_Q2, WT_V, WT_IQ, WT_IW = 0, Q2_ROWS, Q2_ROWS + ATTN_WIDTH, Q2_ROWS + 2 * ATTN_WIDTH
WT_ROWS = WT_IW + 16
WN_K, WN_U, WN_IK = 0, ATTN_WIDTH, ATTN_WIDTH + POOL_WIDTH
WN_COLS = WN_IK + LANES


def _proj_kernel(x_ref, wt_ref, wn_ref, q2t_ref, vt_ref, iqt_ref, iwt_ref, k_ref, ik_ref, u_ref, *, tm):
    xb = x_ref[0].astype(BF16)
    t = lax.dot_general(wt_ref[...], xb, (((1,), (1,)), ((), ())),
                        preferred_element_type=F32)
    q2t_ref[0] = (t[WT_Q2:WT_V] * (HEAD_DIM ** -0.5)).astype(BF16)
    iqt_ref[0] = (t[WT_IQ:WT_IW] * (IDX_DIM ** -0.5)).astype(BF16)
    iwt_ref[0] = t[WT_IW:WT_IW + IDX_HEADS] * (IDX_HEADS ** -0.5)
    for j in range(tm // CHUNK):
        vt_ref[0, j] = t[WT_V:WT_IQ, j * CHUNK:(j + 1) * CHUNK].astype(BF16)
    n = jnp.dot(xb, wn_ref[...], preferred_element_type=F32)
    k_ref[0] = n[:, WN_K:WN_U].astype(BF16)
    u_ref[0] = n[:, WN_U:WN_IK]
    ik_ref[0] = n[:, WN_IK:WN_IK + IDX_DIM].astype(BF16)


def _in_proj(x, w_in, *, tm=512):
    B, S, D = x.shape
    q_off, k_off, v_off = 0, ATTN_WIDTH, 2 * ATTN_WIDTH
    iq_off = 3 * ATTN_WIDTH
    ik_off = iq_off + IDX_HEADS * IDX_DIM
    iw_off = ik_off + IDX_DIM
    p_off = iw_off + IDX_HEADS
    wq = w_in[:, q_off:k_off].T.reshape(ATTN_HEADS // 2, 2, HEAD_DIM, D)
    zq = jnp.zeros((ATTN_HEADS // 2, HEAD_DIM, D), w_in.dtype)
    wq2 = jnp.stack([wq[:, 0], zq, zq, wq[:, 1]], axis=1).reshape(Q2_ROWS, D)
    wt = jnp.concatenate([
        wq2, w_in[:, v_off:iq_off].T, w_in[:, iq_off:ik_off].T, w_in[:, iw_off:p_off].T,
        jnp.zeros((WT_ROWS - WT_IW - IDX_HEADS, D), w_in.dtype)], axis=0).astype(BF16)
    wn = jnp.concatenate([
        w_in[:, k_off:v_off], w_in[:, p_off:], w_in[:, ik_off:iw_off],
        jnp.zeros((D, WN_COLS - WN_IK - IDX_DIM), w_in.dtype)], axis=1).astype(BF16)
    nt = S // tm
    cpt = tm // CHUNK
    outs = pl.pallas_call(
        functools.partial(_proj_kernel, tm=tm),
        out_shape=(
            jax.ShapeDtypeStruct((B, Q2_ROWS, S), BF16),
            jax.ShapeDtypeStruct((B, S // CHUNK, ATTN_WIDTH, CHUNK), BF16),
            jax.ShapeDtypeStruct((B, IDX_HEADS * IDX_DIM, S), BF16),
            jax.ShapeDtypeStruct((B, IDX_HEADS, S), F32),
            jax.ShapeDtypeStruct((B, S, ATTN_WIDTH), BF16),
            jax.ShapeDtypeStruct((B, S, IDX_DIM), BF16),
            jax.ShapeDtypeStruct((B, S, POOL_WIDTH), F32),
        ),
        grid=(B, nt),
        in_specs=[pl.BlockSpec((1, tm, D), lambda b, i: (b, i, 0)),
                  pl.BlockSpec((WT_ROWS, D), lambda b, i: (0, 0)),
                  pl.BlockSpec((D, WN_COLS), lambda b, i: (0, 0))],
        out_specs=(
            pl.BlockSpec((1, Q2_ROWS, tm), lambda b, i: (b, 0, i)),
            pl.BlockSpec((1, cpt, ATTN_WIDTH, CHUNK), lambda b, i: (b, i, 0, 0)),
            pl.BlockSpec((1, IDX_HEADS * IDX_DIM, tm), lambda b, i: (b, 0, i)),
            pl.BlockSpec((1, IDX_HEADS, tm), lambda b, i: (b, 0, i)),
            pl.BlockSpec((1, tm, ATTN_WIDTH), lambda b, i: (b, i, 0)),
            pl.BlockSpec((1, tm, IDX_DIM), lambda b, i: (b, i, 0)),
            pl.BlockSpec((1, tm, POOL_WIDTH), lambda b, i: (b, i, 0)),
        ),
        compiler_params=pltpu.CompilerParams(
            dimension_semantics=("arbitrary", "arbitrary"), vmem_limit_bytes=VMEM_LIMIT),
        name="in_proj",
    )(x, wt, wn)
    return outs


SNAP_FIRST = 14
SNAP_EVERY = 5
MAX_STEPS = SNAP_FIRST + 5 * 17
ROW_LO, ROW_HI, ROW_CNT, ROW_PROBE = 0, 1, 2, 3


def _order_key(bits):
    return bits ^ ((bits >> 31) & 0x7FFFFFFF)


def _attn_kernel(q2t_ref, iqt_ref, iwt_ref, k_ref, vt_ref, ik_ref, bias_ref, o_ref,
                 sc_ref, st_ref, neg_ref, lg_ref, p_ref, m_ref, al_ref, s_ref, acc_ref,
                 *, topk, idx_bits):
    i = pl.program_id(1)
    nch = i + 1
    C = CHUNK
    row = lax.broadcasted_iota(I32, (C, C), 0)
    col = lax.broadcasted_iota(I32, (C, C), 1)

    def fold_rows(op, x):
        return op(x.reshape(C // 8, 8, C), axis=0)

    def score_body(c, carry):
        smin, smax = carry
        ikc = ik_ref[0, c]
        s = jnp.zeros((C, C), F32)
        for j in range(IDX_HEADS):
            d = jnp.dot(ikc, iqt_ref[0, j * IDX_DIM:(j + 1) * IDX_DIM, :],
                        preferred_element_type=F32)
            s = s + iwt_ref[0, j:j + 1, :] * jnp.maximum(d, 0.0)
        causal = (c * C + row) <= (i * C + col)
        sc_ref[c] = jnp.where(causal, s, -jnp.inf)
        smin = jnp.minimum(smin, fold_rows(jnp.min, jnp.where(causal, s, jnp.inf)))
        smax = jnp.maximum(smax, fold_rows(jnp.max, jnp.where(causal, s, -jnp.inf)))
        return smin, smax

    smin, smax = lax.fori_loop(0, nch, score_body,
                               (jnp.full((8, C), jnp.inf, F32), jnp.full((8, C), -jnp.inf, F32)))
    smin = jnp.min(smin, axis=0, keepdims=True)
    smax = jnp.max(smax, axis=0, keepdims=True)

    pos = i * C + lax.broadcasted_iota(I32, (1, C), 1)
    n_keys = (pos + 1).astype(F32)
    k_eff = jnp.minimum(pos + 1, topk).astype(F32)

    def count(indicator):
        def body(c, acc):
            return acc + fold_rows(jnp.sum, indicator(sc_ref[c], c))
        part = lax.fori_loop(0, nch, body, jnp.zeros((8, C), F32))
        return jnp.sum(part, axis=0, keepdims=True)

    def load_state():
        return (st_ref[ROW_LO:ROW_LO + 1, :], st_ref[ROW_HI:ROW_HI + 1, :],
                st_ref[ROW_CNT:ROW_CNT + 1, :], st_ref[ROW_PROBE:ROW_PROBE + 1, :])

    def store_state(lo, hi, cnt_lo, probe):
        st_ref[ROW_LO:ROW_LO + 1, :] = lo
        st_ref[ROW_HI:ROW_HI + 1, :] = hi
        st_ref[ROW_CNT:ROW_CNT + 1, :] = cnt_lo
        st_ref[ROW_PROBE:ROW_PROBE + 1, :] = probe

    def open_cols(lo, hi, cnt_lo):
        return jnp.where(cnt_lo > k_eff, jnp.where(hi > lo, 1.0, 0.0), 0.0)

    def bisect(it, lo, hi, cnt_lo, probe):
        klo = _order_key(pltpu.bitcast(lo, I32))
        khi = _order_key(pltpu.bitcast(hi, I32))
        i_mid = pltpu.bitcast(_order_key((klo >> 1) + (khi >> 1) + (klo & khi & 1)), F32)
        mid = jnp.where(jnp.logical_and(it > SNAP_FIRST, it % 2 == 1), i_mid, 0.5 * lo + 0.5 * hi)
        probing = probe > lo
        mid = jnp.where(probing, probe, mid)
        ok = open_cols(lo, hi, cnt_lo) * jnp.where(mid > lo, jnp.where(mid < hi, 1.0, 0.0), 0.0)
        cnt = count(lambda sc, c: jnp.where(sc >= mid, 1.0, 0.0))
        up = ok * jnp.where(cnt >= k_eff, 1.0, 0.0)
        dn = ok - up
        new_hi = jnp.where(dn + up * jnp.where(probing, 1.0, 0.0) > 0.0, mid, hi)
        return (jnp.where(up > 0.0, mid, lo), new_hi, jnp.where(up > 0.0, cnt, cnt_lo),
                jnp.full((1, C), -jnp.inf, F32))

    def snap(it, lo, hi, cnt_lo, probe):
        def body(c, carry):
            vmin, vmax = carry
            sc = sc_ref[c]
            vmin = jnp.minimum(vmin, fold_rows(jnp.min, jnp.where(sc >= lo, sc, jnp.inf)))
            vmax = jnp.maximum(vmax, fold_rows(jnp.max, jnp.where(sc < hi, sc, -jnp.inf)))
            return vmin, vmax
        vmin, vmax = lax.fori_loop(0, nch, body, (jnp.full((8, C), jnp.inf, F32),
                                                  jnp.full((8, C), -jnp.inf, F32)))
        vmin = jnp.min(vmin, axis=0, keepdims=True)
        vmax = jnp.max(vmax, axis=0, keepdims=True)
        is_open = open_cols(lo, hi, cnt_lo) > 0.0
        single = vmin >= vmax
        return (jnp.where(is_open, vmin, lo),
                jnp.where(is_open, jnp.where(single, vmin, hi), hi),
                cnt_lo,
                jnp.where(is_open, jnp.where(single, -jnp.inf, vmax), -jnp.inf))

    def search_cond(st):
        it, n_open = st
        return jnp.logical_and(it < MAX_STEPS, n_open > 0.0)

    def search_body(st):
        it, _ = st
        do_snap = jnp.logical_and(it >= SNAP_FIRST, (it - SNAP_FIRST) % SNAP_EVERY == 0)

        @pl.when(do_snap)
        def _():
            store_state(*snap(it, *load_state()))

        @pl.when(jnp.logical_not(do_snap))
        def _():
            store_state(*bisect(it, *load_state()))

        lo, hi, cnt_lo, _ = load_state()
        return it + 1, jnp.max(open_cols(lo, hi, cnt_lo))

    kmax = _order_key(pltpu.bitcast(smax, I32))
    hi0 = pltpu.bitcast(_order_key(jnp.where(kmax == 2 ** 31 - 1, kmax, kmax + 1)), F32)
    store_state(smin, hi0, n_keys, jnp.full((1, C), -jnp.inf, F32))
    lax.while_loop(search_cond, search_body, (jnp.int32(0), jnp.max(open_cols(smin, hi0, n_keys))))
    th, _, cnt_th, _ = load_state()

    surplus = cnt_th - k_eff
    max_surplus = jnp.max(surplus)
    rowf = row.astype(F32)

    def discard_from(cut):
        def drop_body(c, carry):
            sc = sc_ref[c]
            idx = rowf + (c * C).astype(F32)
            sc_ref[c] = jnp.where(sc == th, jnp.where(idx >= cut, -jnp.inf, sc), sc)
            return carry
        lax.fori_loop(0, nch, drop_body, 0)

    @pl.when(jnp.logical_and(max_surplus > 0.0, max_surplus <= float(idx_bits)))
    def _():
        def peel(j, cut):
            def body(c, best):
                sc = sc_ref[c]
                idx = rowf + (c * C).astype(F32)
                tied_below = jnp.where(sc == th, jnp.where(idx < cut, idx, -1.0), -1.0)
                return jnp.maximum(best, fold_rows(jnp.max, tied_below))
            best = lax.fori_loop(0, nch, body, jnp.full((8, C), -1.0, F32))
            best = jnp.max(best, axis=0, keepdims=True)
            return jnp.where(surplus > j.astype(F32), best, cut)
        discard_from(lax.fori_loop(0, max_surplus.astype(I32), peel, jnp.full((1, C), jnp.inf, F32)))

    @pl.when(max_surplus > float(idx_bits))
    def _():
        need = k_eff - count(lambda sc, c: jnp.where(sc > th, 1.0, 0.0))

        def idx_body(b, lim):
            cand = lim + lax.shift_left(jnp.int32(1), idx_bits - 1 - b)
            cnt = count(lambda sc, c: jnp.where(sc == th, jnp.where(c * C + row < cand, 1.0, 0.0), 0.0))
            return jnp.where(cnt < need, cand, lim)

        last = lax.fori_loop(0, idx_bits, idx_body, jnp.zeros((1, C), I32))
        discard_from((last + 1).astype(F32))

    m_ref[...] = jnp.full(m_ref.shape, NEG_BIG, F32)
    s_ref[...] = jnp.zeros(s_ref.shape, F32)
    acc_ref[...] = jnp.zeros(acc_ref.shape, F32)

    def attend(c, near):
        neg = jnp.where(sc_ref[c] >= th, 0.0, -jnp.inf)
        if near == 0:
            neg = jnp.where(row <= col, neg, -jnp.inf)
        neg_ref[...] = neg
        for h in range(ATTN_HEADS):
            p2 = (h // 2) * 2 * HEAD_DIM
            lg = jnp.dot(k_ref[0, c, :, p2:p2 + 2 * HEAD_DIM],
                         q2t_ref[0, 2 * HEAD_DIM * h:2 * HEAD_DIM * (h + 1), :],
                         preferred_element_type=F32)
            if near is not None:
                lg = lg + bias_ref[near, h]
            lg = lg + neg_ref[...]
            lg_ref[h] = lg
            m_old = m_ref[h:h + 1, :]
            m_new = jnp.maximum(m_old, jnp.max(lg, axis=0, keepdims=True))
            al_ref[h:h + 1, :] = jnp.exp(m_old - m_new)
            m_ref[h:h + 1, :] = m_new
        for h in range(ATTN_HEADS):
            p = jnp.exp(lg_ref[h] - m_ref[h:h + 1, :])
            s_ref[h:h + 1, :] = al_ref[h:h + 1, :] * s_ref[h:h + 1, :] + jnp.sum(p, axis=0, keepdims=True)
            p_ref[h] = p.astype(BF16)
        for h in range(ATTN_HEADS):
            hs = slice(h * HEAD_DIM, (h + 1) * HEAD_DIM)
            pv = jnp.dot(vt_ref[0, c, hs, :], p_ref[h], preferred_element_type=F32)
            acc_ref[hs, :] = al_ref[h:h + 1, :] * acc_ref[hs, :] + pv

    def far_body(c, carry):
        attend(c, None)
        return carry

    lax.fori_loop(0, jnp.maximum(i - 1, 0), far_body, 0)

    @pl.when(i >= 1)
    def _():
        attend(i - 1, 1)

    attend(i, 0)

    for h in range(ATTN_HEADS):
        hs = slice(h * HEAD_DIM, (h + 1) * HEAD_DIM)
        acc_ref[hs, :] = acc_ref[hs, :] / s_ref[h:h + 1, :]
    o_ref[0] = acc_ref[...].T.astype(o_ref.dtype)


def _dsa_attention(q2t, iqt, iwt, k, vt, ik, bias):
    B, S, _ = k.shape
    nq = S // CHUNK
    topk = min(TOPK_MAX, S // 4)
    k4 = k.reshape(B, nq, CHUNK, ATTN_WIDTH)
    ik4 = ik.reshape(B, nq, CHUNK, IDX_DIM)
    return pl.pallas_call(
        functools.partial(_attn_kernel, topk=topk, idx_bits=(S - 1).bit_length()),
        out_shape=jax.ShapeDtypeStruct((B, S, ATTN_WIDTH), BF16),
        grid=(B, nq),
        in_specs=[
            pl.BlockSpec((1, Q2_ROWS, CHUNK), lambda b, i: (b, 0, i)),
            pl.BlockSpec((1, IDX_HEADS * IDX_DIM, CHUNK), lambda b, i: (b, 0, i)),
            pl.BlockSpec((1, IDX_HEADS, CHUNK), lambda b, i: (b, 0, i)),
            pl.BlockSpec((1, nq, CHUNK, ATTN_WIDTH), lambda b, i: (b, 0, 0, 0)),
            pl.BlockSpec((1, nq, ATTN_WIDTH, CHUNK), lambda b, i: (b, 0, 0, 0)),
            pl.BlockSpec((1, nq, CHUNK, IDX_DIM), lambda b, i: (b, 0, 0, 0)),
            pl.BlockSpec((2, ATTN_HEADS, CHUNK, CHUNK), lambda b, i: (0, 0, 0, 0)),
        ],
        out_specs=pl.BlockSpec((1, CHUNK, ATTN_WIDTH), lambda b, i: (b, i, 0)),
        scratch_shapes=[
            pltpu.VMEM((nq, CHUNK, CHUNK), F32),
            pltpu.VMEM((8, CHUNK), F32),
            pltpu.VMEM((CHUNK, CHUNK), F32),
            pltpu.VMEM((ATTN_HEADS, CHUNK, CHUNK), F32),
            pltpu.VMEM((ATTN_HEADS, CHUNK, CHUNK), BF16),
            pltpu.VMEM((ATTN_HEADS, CHUNK), F32),
            pltpu.VMEM((ATTN_HEADS, CHUNK), F32),
            pltpu.VMEM((ATTN_HEADS, CHUNK), F32),
            pltpu.VMEM((ATTN_WIDTH, CHUNK), F32),
        ],
        compiler_params=pltpu.CompilerParams(
            dimension_semantics=("arbitrary", "arbitrary"), vmem_limit_bytes=VMEM_LIMIT),
        name="dsa_attn",
    )(q2t, iqt, iwt, k4, vt, ik4, bias)


ROUTE_LANES = LANES


def _layer_norm(y, g, b):
    mu = jnp.mean(y, axis=-1, keepdims=True)
    yc = y - mu
    var = jnp.mean(yc * yc, axis=-1, keepdims=True)
    return yc * lax.rsqrt(var + LN_EPS) * g + b


def _mix_kernel(attn_ref, u_ref, halo_ref, x_ref, wpool_ref, pscale_ref, wout_ref, g_ref, b_ref,
                wr_ref, br_ref, h_ref, hb_ref, comb_ref, *, tm):
    i = pl.program_id(1)
    halo = jnp.where(i > 0, halo_ref[0], 0.0)
    ue = jnp.concatenate([halo, u_ref[0]], axis=0)
    pos = i * tm + lax.broadcasted_iota(I32, (tm, 1), 0)
    mixed = []
    for g, w in enumerate(POOL_WINDOWS):
        gs = slice(g * POOL_GROUP_DIM, (g + 1) * POOL_GROUP_DIM)
        ch = ue[:, gs]
        win = ch
        step = 1
        while step < w:
            win = win + pltpu.roll(win, step, axis=0)
            step *= 2
        cnt = jnp.minimum(pos + 1, w).astype(F32)
        pooled = win[POOL_HALO:] / cnt - ch[POOL_HALO:]
        mg = jnp.dot(pooled.astype(BF16), wpool_ref[g], preferred_element_type=F32)
        mixed.append((mg * pscale_ref[:, gs]).astype(BF16))
    cat = jnp.concatenate([attn_ref[0]] + mixed, axis=-1)
    mix = jnp.dot(cat, wout_ref[...], preferred_element_type=F32)
    h = _layer_norm(DEEPNORM_ALPHA * x_ref[0] + mix, g_ref[...], b_ref[...])
    h_ref[0] = h
    hb = h.astype(BF16)
    hb_ref[0] = hb

    lg = jnp.dot(hb, wr_ref[...], preferred_element_type=F32) + br_ref[...]
    lane = lax.broadcasted_iota(I32, (tm, ROUTE_LANES), 1)
    gl = jnp.where(lane >= N_EXPERTS, jnp.where(lane < N_EXPERTS + N_GROUPS, lg, -jnp.inf), -jnp.inf)
    ge = jnp.exp(gl - jnp.max(gl, axis=-1, keepdims=True))
    pg = ge / jnp.sum(ge, axis=-1, keepdims=True)
    pg_top = jnp.max(pg, axis=-1, keepdims=True)
    g_lane = jnp.min(jnp.where(pg == pg_top, lane, ROUTE_LANES), axis=-1, keepdims=True)
    e_lo = (g_lane - N_EXPERTS) * EXPERTS_PER_GROUP
    fl = jnp.where(lane >= e_lo, jnp.where(lane < e_lo + EXPERTS_PER_GROUP, lg, -jnp.inf), -jnp.inf)
    fe = jnp.exp(fl - jnp.max(fl, axis=-1, keepdims=True))
    pf = fe / jnp.sum(fe, axis=-1, keepdims=True)
    p1 = jnp.max(pf, axis=-1, keepdims=True)
    i1 = jnp.min(jnp.where(pf == p1, lane, ROUTE_LANES), axis=-1, keepdims=True)
    pr = jnp.where(lane == i1, -1.0, jnp.where(fl == -jnp.inf, -1.0, pf))
    p2 = jnp.max(pr, axis=-1, keepdims=True)
    i2 = jnp.min(jnp.where(pr == p2, lane, ROUTE_LANES), axis=-1, keepdims=True)
    psum = p1 + p2
    w1 = pg_top * p1 / psum
    w2 = pg_top * p2 / psum
    comb = jnp.where(lane == i1, w1, jnp.where(lane == i2, w2, 0.0))
    comb_ref[0] = comb[:, :N_EXPERTS]


def _mix_norm_route(attn, u, x, w_pool, pool_scale, w_out, ln_g, ln_b, w_r1, b_r1, w_r2, b_r2, *, tm=256):
    B, S, D = x.shape
    wr = jnp.concatenate([w_r2, w_r1, jnp.zeros((D, ROUTE_LANES - N_EXPERTS - N_GROUPS), w_r1.dtype)],
                         axis=1).astype(BF16)
    br = jnp.concatenate([b_r2, b_r1, jnp.zeros((ROUTE_LANES - N_EXPERTS - N_GROUPS,), b_r1.dtype)])[None, :]
    hpt = tm // POOL_HALO
    return pl.pallas_call(
        functools.partial(_mix_kernel, tm=tm),
        out_shape=(jax.ShapeDtypeStruct((B, S, D), F32),
                   jax.ShapeDtypeStruct((B, S, D), BF16),
                   jax.ShapeDtypeStruct((B, S, N_EXPERTS), F32)),
        grid=(B, S // tm),
        in_specs=[
            pl.BlockSpec((1, tm, ATTN_WIDTH), lambda b, i: (b, i, 0)),
            pl.BlockSpec((1, tm, POOL_WIDTH), lambda b, i: (b, i, 0)),
            pl.BlockSpec((1, POOL_HALO, POOL_WIDTH), lambda b, i: (b, jnp.maximum(i * hpt - 1, 0), 0)),
            pl.BlockSpec((1, tm, D), lambda b, i: (b, i, 0)),
            pl.BlockSpec((len(POOL_WINDOWS), POOL_GROUP_DIM, POOL_GROUP_DIM), lambda b, i: (0, 0, 0)),
            pl.BlockSpec((1, POOL_WIDTH), lambda b, i: (0, 0)),
            pl.BlockSpec((D, D), lambda b, i: (0, 0)),
            pl.BlockSpec((1, D), lambda b, i: (0, 0)),
            pl.BlockSpec((1, D), lambda b, i: (0, 0)),
            pl.BlockSpec((D, ROUTE_LANES), lambda b, i: (0, 0)),
            pl.BlockSpec((1, ROUTE_LANES), lambda b, i: (0, 0)),
        ],
        out_specs=(pl.BlockSpec((1, tm, D), lambda b, i: (b, i, 0)),
                   pl.BlockSpec((1, tm, D), lambda b, i: (b, i, 0)),
                   pl.BlockSpec((1, tm, N_EXPERTS), lambda b, i: (b, i, 0))),
        compiler_params=pltpu.CompilerParams(
            dimension_semantics=("arbitrary", "arbitrary"), vmem_limit_bytes=VMEM_LIMIT),
        name="mix_norm",
    )(attn, u, u, x, w_pool.astype(BF16), pool_scale[None, :], w_out.astype(BF16),
      ln_g[None, :], ln_b[None, :], wr, br)


def _moe_kernel(hb_ref, h_ref, comb_ref, wgu_ref, wd_ref, g_ref, b_ref, o_ref, acc_ref, *, d_expert):
    e = pl.program_id(1)

    @pl.when(e == 0)
    def _():
        acc_ref[...] = jnp.zeros(acc_ref.shape, F32)

    gu = jnp.dot(hb_ref[...], wgu_ref[0], preferred_element_type=F32)
    gate = gu[:, :d_expert]
    a = gate * jax.nn.sigmoid(gate) * gu[:, d_expert:]
    y = jnp.dot(a.astype(BF16), wd_ref[0], preferred_element_type=F32)
    comb = comb_ref[...]
    lane = lax.broadcasted_iota(I32, comb.shape, 1)
    ce = jnp.sum(jnp.where(lane == e, comb, 0.0), axis=-1, keepdims=True)
    acc_ref[...] += ce * y

    @pl.when(e == pl.num_programs(1) - 1)
    def _():
        o_ref[...] = _layer_norm(DEEPNORM_ALPHA * h_ref[...] + acc_ref[...], g_ref[...], b_ref[...])


def _moe_norm(hb, h, comb, w_gate, w_up, w_down, ln_g, ln_b, *, tm=1024):
    T, D = h.shape
    E, _, F = w_gate.shape
    wgu = jnp.concatenate([w_gate, w_up], axis=-1).astype(BF16)
    wd = w_down.astype(BF16)
    return pl.pallas_call(
        functools.partial(_moe_kernel, d_expert=F),
        out_shape=jax.ShapeDtypeStruct((T, D), F32),
        grid=(T // tm, E),
        in_specs=[
            pl.BlockSpec((tm, D), lambda t, e: (t, 0)),
            pl.BlockSpec((tm, D), lambda t, e: (t, 0)),
            pl.BlockSpec((tm, E), lambda t, e: (t, 0)),
            pl.BlockSpec((1, D, 2 * F), lambda t, e: (e, 0, 0)),
            pl.BlockSpec((1, F, D), lambda t, e: (e, 0, 0)),
            pl.BlockSpec((1, D), lambda t, e: (0, 0)),
            pl.BlockSpec((1, D), lambda t, e: (0, 0)),
        ],
        out_specs=pl.BlockSpec((tm, D), lambda t, e: (t, 0)),
        scratch_shapes=[pltpu.VMEM((tm, D), F32)],
        compiler_params=pltpu.CompilerParams(
            dimension_semantics=("arbitrary", "arbitrary"), vmem_limit_bytes=VMEM_LIMIT),
        name="moe_norm",
    )(hb, h, comb, wgu, wd, ln_g[None, :], ln_b[None, :])


def kernel(x, w_in, w_pool, pool_scale, w_out, rel_bias, ln1_g, ln1_b, w_r1, b_r1, w_r2, b_r2,
           w_gate, w_up, w_down, ln2_g, ln2_b):
    B, S, D = x.shape
    assert w_in.shape[0] == DEPTH == 1 and S % CHUNK == 0
    bias = _bias_tiles(rel_bias)
    q2t, vt, iqt, iwt, k, ik, u = _in_proj(x, w_in[0])
    attn = _dsa_attention(q2t, iqt, iwt, k, vt, ik, bias)
    h, hb, comb = _mix_norm_route(attn, u, x, w_pool[0], pool_scale[0], w_out[0], ln1_g[0], ln1_b[0],
                                  w_r1[0], b_r1[0], w_r2[0], b_r2[0])
    out = _moe_norm(hb.reshape(B * S, D), h.reshape(B * S, D), comb.reshape(B * S, N_EXPERTS),
                    w_gate[0], w_up[0], w_down[0], ln2_g[0], ln2_b[0])
    return out.reshape(B, S, D)
```

```python
import functools
import math

import numpy as np
import jax
import jax.numpy as jnp
from jax import lax
from jax.experimental import pallas as pl
from jax.experimental.pallas import tpu as pltpu

F32 = jnp.float32
BF16 = jnp.bfloat16
I32 = jnp.int32

ATTN_HEADS = 8
HEAD_DIM = 64
ATTN_WIDTH = ATTN_HEADS * HEAD_DIM
IDX_HEADS = 8
IDX_DIM = 64
TOPK_MAX = 256
POOL_WINDOWS = (2, 4, 8, 16)
POOL_GROUP_DIM = 128
POOL_WIDTH = len(POOL_WINDOWS) * POOL_GROUP_DIM
POOL_HALO = 16
REL_BUCKETS = 32
REL_MAX_DIST = 128
N_GROUPS = 4
EXPERTS_PER_GROUP = 8
N_EXPERTS = N_GROUPS * EXPERTS_PER_GROUP
LN_EPS = 1e-5
DEPTH = 1
DEEPNORM_ALPHA = (2 * DEPTH) ** 0.25
LOG2E = math.log2(math.e)

LANES = 128
CHUNK = 256
INT_MIN = -2 ** 31
NEG_BIG = -1e30
VMEM_LIMIT = 56 * 1024 * 1024


def _rel_bucket_table(n):
    max_exact = REL_BUCKETS // 2
    d = np.arange(n)
    nf = np.maximum(d, 1).astype(np.float32)
    ratio = np.log(nf / np.float32(max_exact)) / np.float32(math.log(REL_MAX_DIST / max_exact))
    large = max_exact + (ratio * np.float32(REL_BUCKETS - max_exact)).astype(np.int32)
    large = np.minimum(large, REL_BUCKETS - 1)
    return np.where(d < max_exact, d, large).astype(np.int32)


def _near_bucket_tiles():
    tbl = _rel_bucket_table(2 * CHUNK)
    a = np.arange(CHUNK)[:, None]
    b = np.arange(CHUNK)[None, :]
    tiles = [tbl[np.maximum(delta * CHUNK + b - a, 0)] for delta in (0, 1)]
    return np.stack(tiles).astype(np.int32)


FAR_BUCKET = REL_BUCKETS - 1
assert int(_rel_bucket_table(2 * CHUNK)[CHUNK + 1:].min()) == FAR_BUCKET


def _bias_kernel(rb_ref, bucket_ref, o_ref):
    h = pl.program_id(1)
    bk = bucket_ref[0]
    far = rb_ref[FAR_BUCKET, h]
    acc = jnp.zeros(bk.shape, F32)
    for n in range(REL_BUCKETS):
        acc = jnp.where(bk == n, rb_ref[n, h] - far, acc)
    o_ref[0, 0] = acc * LOG2E


def _bias_tiles(rel_bias):
    buckets = jnp.asarray(_near_bucket_tiles())
    return pl.pallas_call(
        _bias_kernel,
        out_shape=jax.ShapeDtypeStruct((2, ATTN_HEADS, CHUNK, CHUNK), F32),
        grid=(2, ATTN_HEADS),
        in_specs=[pl.BlockSpec(memory_space=pltpu.SMEM),
                  pl.BlockSpec((1, CHUNK, CHUNK), lambda d, h: (d, 0, 0))],
        out_specs=pl.BlockSpec((1, 1, CHUNK, CHUNK), lambda d, h: (d, h, 0, 0)),
        name="bias_tiles",
    )(rel_bias, buckets)


Q2_ROWS = 2 * ATTN_WIDTH
V_SLOT = HEAD_DIM + 16
VT_ROWS = ATTN_HEADS * V_SLOT
WT_Q2, WT_V, WT_IQ, WT_IW = 0, Q2_ROWS, Q2_ROWS + VT_ROWS, Q2_ROWS + VT_ROWS + ATTN_WIDTH
WT_ROWS = WT_IW + 16
WN_K, WN_U, WN_IK = 0, ATTN_WIDTH, ATTN_WIDTH + POOL_WIDTH
WN_COLS = WN_IK + LANES


def _proj_kernel(x_ref, wt_ref, wn_ref, q2t_ref, vt_ref, iqt_ref, iwt_ref, k_ref, ik_ref, u_ref, *, tm):
    xb = x_ref[0].astype(BF16)
    t = lax.dot_general(wt_ref[...], xb, (((1,), (1,)), ((), ())),
                        preferred_element_type=F32)
    q2t_ref[0] = (t[WT_Q2:WT_V] * (HEAD_DIM ** -0.5 * LOG2E)).astype(BF16)
    iqt_ref[0] = (t[WT_IQ:WT_IW] * (IDX_DIM ** -0.5)).astype(BF16)
    iwt_ref[0] = t[WT_IW:WT_IW + IDX_HEADS] * (IDX_HEADS ** -0.5)
    slot_row = lax.broadcasted_iota(I32, (VT_ROWS, tm), 0) % V_SLOT
    vt = jnp.where(slot_row >= HEAD_DIM, 1.0, t[WT_V:WT_IQ]).astype(BF16)
    for j in range(tm // CHUNK):
        vt_ref[0, j] = vt[:, j * CHUNK:(j + 1) * CHUNK]
    n = jnp.dot(xb, wn_ref[...], preferred_element_type=F32)
    k_ref[0] = n[:, WN_K:WN_U].astype(BF16)
    u_ref[0] = n[:, WN_U:WN_IK]
    ik_ref[0] = n[:, WN_IK:WN_IK + IDX_DIM].astype(BF16)


def _in_proj(x, w_in, *, tm=512):
    B, S, D = x.shape
    q_off, k_off, v_off = 0, ATTN_WIDTH, 2 * ATTN_WIDTH
    iq_off = 3 * ATTN_WIDTH
    ik_off = iq_off + IDX_HEADS * IDX_DIM
    iw_off = ik_off + IDX_DIM
    p_off = iw_off + IDX_HEADS
    wq = w_in[:, q_off:k_off].T.reshape(ATTN_HEADS // 2, 2, HEAD_DIM, D)
    zq = jnp.zeros((ATTN_HEADS // 2, HEAD_DIM, D), w_in.dtype)
    wq2 = jnp.stack([wq[:, 0], zq, zq, wq[:, 1]], axis=1).reshape(Q2_ROWS, D)
    wv = w_in[:, v_off:iq_off].T.reshape(ATTN_HEADS, HEAD_DIM, D)
    wv = jnp.pad(wv, ((0, 0), (0, V_SLOT - HEAD_DIM), (0, 0))).reshape(VT_ROWS, D)
    wt = jnp.concatenate([
        wq2, wv, w_in[:, iq_off:ik_off].T, w_in[:, iw_off:p_off].T,
        jnp.zeros((WT_ROWS - WT_IW - IDX_HEADS, D), w_in.dtype)], axis=0).astype(BF16)
    wn = jnp.concatenate([
        w_in[:, k_off:v_off], w_in[:, p_off:], w_in[:, ik_off:iw_off],
        jnp.zeros((D, WN_COLS - WN_IK - IDX_DIM), w_in.dtype)], axis=1).astype(BF16)
    nt = S // tm
    cpt = tm // CHUNK
    outs = pl.pallas_call(
        functools.partial(_proj_kernel, tm=tm),
        out_shape=(
            jax.ShapeDtypeStruct((B, Q2_ROWS, S), BF16),
            jax.ShapeDtypeStruct((B, S // CHUNK, VT_ROWS, CHUNK), BF16),
            jax.ShapeDtypeStruct((B, IDX_HEADS * IDX_DIM, S), BF16),
            jax.ShapeDtypeStruct((B, IDX_HEADS, S), F32),
            jax.ShapeDtypeStruct((B, S, ATTN_WIDTH), BF16),
            jax.ShapeDtypeStruct((B, S, IDX_DIM), BF16),
            jax.ShapeDtypeStruct((B, S, POOL_WIDTH), F32),
        ),
        grid=(B, nt),
        in_specs=[pl.BlockSpec((1, tm, D), lambda b, i: (b, i, 0)),
                  pl.BlockSpec((WT_ROWS, D), lambda b, i: (0, 0)),
                  pl.BlockSpec((D, WN_COLS), lambda b, i: (0, 0))],
        out_specs=(
            pl.BlockSpec((1, Q2_ROWS, tm), lambda b, i: (b, 0, i)),
            pl.BlockSpec((1, cpt, VT_ROWS, CHUNK), lambda b, i: (b, i, 0, 0)),
            pl.BlockSpec((1, IDX_HEADS * IDX_DIM, tm), lambda b, i: (b, 0, i)),
            pl.BlockSpec((1, IDX_HEADS, tm), lambda b, i: (b, 0, i)),
            pl.BlockSpec((1, tm, ATTN_WIDTH), lambda b, i: (b, i, 0)),
            pl.BlockSpec((1, tm, IDX_DIM), lambda b, i: (b, i, 0)),
            pl.BlockSpec((1, tm, POOL_WIDTH), lambda b, i: (b, i, 0)),
        ),
        compiler_params=pltpu.CompilerParams(
            dimension_semantics=("arbitrary", "arbitrary"), vmem_limit_bytes=VMEM_LIMIT),
        name="in_proj",
    )(x, wt, wn)
    return outs


SNAP_FIRST = 14
SNAP_EVERY = 5
MAX_STEPS = SNAP_FIRST + 5 * 17
ROW_LO, ROW_HI, ROW_CNT, ROW_PROBE = 0, 1, 2, 3


def _order_key(bits):
    return bits ^ ((bits >> 31) & 0x7FFFFFFF)


def _attn_kernel(q2t_ref, iqt_ref, iwt_ref, k_ref, vt_ref, ik_ref, bias_ref, o_ref,
                 sc_ref, st_ref, neg_ref, lg_ref, p_ref, m_ref, al_ref, acc_ref,
                 *, topk):
    i = pl.program_id(1)
    nch = i + 1
    C = CHUNK
    row = lax.broadcasted_iota(I32, (C, C), 0)
    col = lax.broadcasted_iota(I32, (C, C), 1)

    def fold_rows(op, x):
        return op(x.reshape(C // 8, 8, C), axis=0)

    def score_body(c, carry):
        smin, smax = carry
        ikc = ik_ref[0, c]
        s = jnp.zeros((C, C), F32)
        for j in range(IDX_HEADS):
            d = jnp.dot(ikc, iqt_ref[0, j * IDX_DIM:(j + 1) * IDX_DIM, :],
                        preferred_element_type=F32)
            s = s + iwt_ref[0, j:j + 1, :] * jnp.maximum(d, 0.0)
        causal = (c * C + row) <= (i * C + col)
        sc_ref[c] = jnp.where(causal, s, -jnp.inf)
        smin = jnp.minimum(smin, fold_rows(jnp.min, jnp.where(causal, s, jnp.inf)))
        smax = jnp.maximum(smax, fold_rows(jnp.max, jnp.where(causal, s, -jnp.inf)))
        return smin, smax

    smin, smax = lax.fori_loop(0, nch, score_body,
                               (jnp.full((8, C), jnp.inf, F32), jnp.full((8, C), -jnp.inf, F32)))
    smin = jnp.min(smin, axis=0, keepdims=True)
    smax = jnp.max(smax, axis=0, keepdims=True)

    pos = i * C + lax.broadcasted_iota(I32, (1, C), 1)
    n_keys = (pos + 1).astype(F32)
    k_eff = jnp.minimum(pos + 1, topk).astype(F32)

    @pl.when(nch % 2 == 1)
    def _():
        sc_ref[nch] = jnp.full((C, C), -jnp.inf, F32)

    def count_ge(mid):
        def body(cp, acc):
            for c in (2 * cp, 2 * cp + 1):
                acc = acc + fold_rows(jnp.sum, jnp.where(sc_ref[c] >= mid, 1.0, 0.0))
            return acc
        part = lax.fori_loop(0, lax.shift_right_logical(nch + 1, 1), body, jnp.zeros((8, C), F32))
        return jnp.sum(part, axis=0, keepdims=True)

    def load_state():
        return (st_ref[ROW_LO:ROW_LO + 1, :], st_ref[ROW_HI:ROW_HI + 1, :],
                st_ref[ROW_CNT:ROW_CNT + 1, :], st_ref[ROW_PROBE:ROW_PROBE + 1, :])

    def store_state(lo, hi, cnt_lo, probe):
        st_ref[ROW_LO:ROW_LO + 1, :] = lo
        st_ref[ROW_HI:ROW_HI + 1, :] = hi
        st_ref[ROW_CNT:ROW_CNT + 1, :] = cnt_lo
        st_ref[ROW_PROBE:ROW_PROBE + 1, :] = probe

    def open_cols(lo, hi, cnt_lo):
        return jnp.where(cnt_lo > k_eff, jnp.where(hi > lo, 1.0, 0.0), 0.0)

    def bisect(it, lo, hi, cnt_lo, probe):
        klo = _order_key(pltpu.bitcast(lo, I32))
        khi = _order_key(pltpu.bitcast(hi, I32))
        i_mid = pltpu.bitcast(_order_key((klo >> 1) + (khi >> 1) + (klo & khi & 1)), F32)
        mid = jnp.where(jnp.logical_and(it > SNAP_FIRST, it % 2 == 1), i_mid, 0.5 * lo + 0.5 * hi)
        probing = probe > lo
        mid = jnp.where(probing, probe, mid)
        ok = open_cols(lo, hi, cnt_lo) * jnp.where(mid > lo, jnp.where(mid < hi, 1.0, 0.0), 0.0)
        cnt = count_ge(mid)
        up = ok * jnp.where(cnt >= k_eff, 1.0, 0.0)
        dn = ok - up
        new_hi = jnp.where(dn + up * jnp.where(probing, 1.0, 0.0) > 0.0, mid, hi)
        return (jnp.where(up > 0.0, mid, lo), new_hi, jnp.where(up > 0.0, cnt, cnt_lo),
                jnp.full((1, C), -jnp.inf, F32))

    def snap(it, lo, hi, cnt_lo, probe):
        def body(c, carry):
            vmin, vmax = carry
            sc = sc_ref[c]
            vmin = jnp.minimum(vmin, fold_rows(jnp.min, jnp.where(sc >= lo, sc, jnp.inf)))
            vmax = jnp.maximum(vmax, fold_rows(jnp.max, jnp.where(sc < hi, sc, -jnp.inf)))
            return vmin, vmax
        vmin, vmax = lax.fori_loop(0, nch, body, (jnp.full((8, C), jnp.inf, F32),
                                                  jnp.full((8, C), -jnp.inf, F32)))
        vmin = jnp.min(vmin, axis=0, keepdims=True)
        vmax = jnp.max(vmax, axis=0, keepdims=True)
        is_open = open_cols(lo, hi, cnt_lo) > 0.0
        single = vmin >= vmax
        return (jnp.where(is_open, vmin, lo),
                jnp.where(is_open, jnp.where(single, vmin, hi), hi),
                cnt_lo,
                jnp.where(is_open, jnp.where(single, -jnp.inf, vmax), -jnp.inf))

    def search_cond(st):
        it, n_open = st
        return jnp.logical_and(it < MAX_STEPS, n_open > 0.0)

    def search_body(st):
        it, _ = st
        do_snap = jnp.logical_and(it >= SNAP_FIRST, (it - SNAP_FIRST) % SNAP_EVERY == 0)

        @pl.when(do_snap)
        def _():
            store_state(*snap(it, *load_state()))

        @pl.when(jnp.logical_not(do_snap))
        def _():
            store_state(*bisect(it, *load_state()))

        lo, hi, cnt_lo, _ = load_state()
        return it + 1, jnp.max(open_cols(lo, hi, cnt_lo))

    kmax = _order_key(pltpu.bitcast(smax, I32))
    hi0 = pltpu.bitcast(_order_key(jnp.where(kmax == 2 ** 31 - 1, kmax, kmax + 1)), F32)
    store_state(smin, hi0, n_keys, jnp.full((1, C), -jnp.inf, F32))
    lax.while_loop(search_cond, search_body, (jnp.int32(0), jnp.max(open_cols(smin, hi0, n_keys))))
    th, _, cnt_th, _ = load_state()

    surplus = cnt_th - k_eff

    @pl.when(jnp.max(surplus) > 0.0)
    def _():
        later = jnp.where(col > row, 1.0, 0.0).astype(BF16)

        def drop_body(r, after):
            c = nch - 1 - r
            sc = sc_ref[c]
            tied = jnp.where(sc == th, 1.0, 0.0)
            follow = jnp.dot(later, tied.astype(BF16), preferred_element_type=F32) + after
            sc_ref[c] = jnp.where(tied * jnp.where(follow < surplus, 1.0, 0.0) > 0.0, -jnp.inf, sc)
            return after + jnp.sum(fold_rows(jnp.sum, tied), axis=0, keepdims=True)

        lax.fori_loop(0, nch, drop_body, jnp.zeros((1, C), F32))

    m_ref[...] = jnp.full(m_ref.shape, NEG_BIG, F32)
    acc_ref[...] = jnp.zeros(acc_ref.shape, F32)

    def attend(c, near):
        neg = jnp.where(sc_ref[c] >= th, 0.0, -jnp.inf)
        if near == 0:
            neg = jnp.where(row <= col, neg, -jnp.inf)
        neg_ref[...] = neg
        for h in range(ATTN_HEADS):
            p2 = (h // 2) * 2 * HEAD_DIM
            lg = jnp.dot(k_ref[0, c, :, p2:p2 + 2 * HEAD_DIM],
                         q2t_ref[0, 2 * HEAD_DIM * h:2 * HEAD_DIM * (h + 1), :],
                         preferred_element_type=F32)
            if near is not None:
                lg = lg + bias_ref[near, h]
            lg = lg + neg_ref[...]
            lg_ref[h] = lg
            m_old = m_ref[h:h + 1, :]
            m_new = jnp.maximum(m_old, jnp.max(lg, axis=0, keepdims=True))
            al_ref[h:h + 1, :] = jnp.exp2(m_old - m_new)
            m_ref[h:h + 1, :] = m_new
        for h in range(ATTN_HEADS):
            p_ref[h] = jnp.exp2(lg_ref[h] - m_ref[h:h + 1, :]).astype(BF16)
        for h in range(ATTN_HEADS):
            hs = slice(h * V_SLOT, (h + 1) * V_SLOT)
            pv = jnp.dot(vt_ref[0, c, hs, :], p_ref[h], preferred_element_type=F32)
            acc_ref[hs, :] = al_ref[h:h + 1, :] * acc_ref[hs, :] + pv

    def far_body(c, carry):
        attend(c, None)
        return carry

    lax.fori_loop(0, jnp.maximum(i - 1, 0), far_body, 0)

    @pl.when(i >= 1)
    def _():
        attend(i - 1, 1)

    attend(i, 0)

    out_t = jnp.concatenate(
        [acc_ref[h * V_SLOT:h * V_SLOT + HEAD_DIM, :] / acc_ref[h * V_SLOT + HEAD_DIM:h * V_SLOT + HEAD_DIM + 1, :]
         for h in range(ATTN_HEADS)], axis=0)
    o_ref[0] = out_t.T.astype(o_ref.dtype)


def _dsa_attention(q2t, iqt, iwt, k, vt, ik, bias):
    B, S, _ = k.shape
    nq = S // CHUNK
    assert nq % 2 == 0
    topk = min(TOPK_MAX, S // 4)
    k4 = k.reshape(B, nq, CHUNK, ATTN_WIDTH)
    ik4 = ik.reshape(B, nq, CHUNK, IDX_DIM)
    return pl.pallas_call(
        functools.partial(_attn_kernel, topk=topk),
        out_shape=jax.ShapeDtypeStruct((B, S, ATTN_WIDTH), BF16),
        grid=(B, nq),
        in_specs=[
            pl.BlockSpec((1, Q2_ROWS, CHUNK), lambda b, i: (b, 0, i)),
            pl.BlockSpec((1, IDX_HEADS * IDX_DIM, CHUNK), lambda b, i: (b, 0, i)),
            pl.BlockSpec((1, IDX_HEADS, CHUNK), lambda b, i: (b, 0, i)),
            pl.BlockSpec((1, nq, CHUNK, ATTN_WIDTH), lambda b, i: (b, 0, 0, 0)),
            pl.BlockSpec((1, nq, VT_ROWS, CHUNK), lambda b, i: (b, 0, 0, 0)),
            pl.BlockSpec((1, nq, CHUNK, IDX_DIM), lambda b, i: (b, 0, 0, 0)),
            pl.BlockSpec((2, ATTN_HEADS, CHUNK, CHUNK), lambda b, i: (0, 0, 0, 0)),
        ],
        out_specs=pl.BlockSpec((1, CHUNK, ATTN_WIDTH), lambda b, i: (b, i, 0)),
        scratch_shapes=[
            pltpu.VMEM((nq, CHUNK, CHUNK), F32),
            pltpu.VMEM((8, CHUNK), F32),
            pltpu.VMEM((CHUNK, CHUNK), F32),
            pltpu.VMEM((ATTN_HEADS, CHUNK, CHUNK), F32),
            pltpu.VMEM((ATTN_HEADS, CHUNK, CHUNK), BF16),
            pltpu.VMEM((ATTN_HEADS, CHUNK), F32),
            pltpu.VMEM((ATTN_HEADS, CHUNK), F32),
            pltpu.VMEM((VT_ROWS, CHUNK), F32),
        ],
        compiler_params=pltpu.CompilerParams(
            dimension_semantics=("arbitrary", "arbitrary"), vmem_limit_bytes=VMEM_LIMIT),
        name="dsa_attn",
    )(q2t, iqt, iwt, k4, vt, ik4, bias)


ROUTE_LANES = LANES


def _layer_norm(y, g, b):
    mu = jnp.mean(y, axis=-1, keepdims=True)
    yc = y - mu
    var = jnp.mean(yc * yc, axis=-1, keepdims=True)
    return yc * lax.rsqrt(var + LN_EPS) * g + b


def _mix_kernel(attn_ref, u_ref, halo_ref, x_ref, wpool_ref, pscale_ref, wout_ref, g_ref, b_ref,
                wr_ref, br_ref, h_ref, hb_ref, comb_ref, *, tm):
    i = pl.program_id(1)
    halo = jnp.where(i > 0, halo_ref[0], 0.0)
    ue = jnp.concatenate([halo, u_ref[0]], axis=0)
    pos = i * tm + lax.broadcasted_iota(I32, (tm, 1), 0)
    mixed = []
    for g, w in enumerate(POOL_WINDOWS):
        gs = slice(g * POOL_GROUP_DIM, (g + 1) * POOL_GROUP_DIM)
        ch = ue[:, gs]
        win = ch
        step = 1
        while step < w:
            win = win + pltpu.roll(win, step, axis=0)
            step *= 2
        cnt = jnp.minimum(pos + 1, w).astype(F32)
        pooled = win[POOL_HALO:] / cnt - ch[POOL_HALO:]
        mg = jnp.dot(pooled.astype(BF16), wpool_ref[g], preferred_element_type=F32)
        mixed.append((mg * pscale_ref[:, gs]).astype(BF16))
    cat = jnp.concatenate([attn_ref[0]] + mixed, axis=-1)
    mix = jnp.dot(cat, wout_ref[...], preferred_element_type=F32)
    h = _layer_norm(DEEPNORM_ALPHA * x_ref[0] + mix, g_ref[...], b_ref[...])
    h_ref[0] = h
    hb = h.astype(BF16)
    hb_ref[0] = hb

    lg = jnp.dot(hb, wr_ref[...], preferred_element_type=F32) + br_ref[...]
    lane = lax.broadcasted_iota(I32, (tm, ROUTE_LANES), 1)
    gl = jnp.where(lane >= N_EXPERTS, jnp.where(lane < N_EXPERTS + N_GROUPS, lg, -jnp.inf), -jnp.inf)
    ge = jnp.exp(gl - jnp.max(gl, axis=-1, keepdims=True))
    pg = ge / jnp.sum(ge, axis=-1, keepdims=True)
    pg_top = jnp.max(pg, axis=-1, keepdims=True)
    g_lane = jnp.min(jnp.where(pg == pg_top, lane, ROUTE_LANES), axis=-1, keepdims=True)
    e_lo = (g_lane - N_EXPERTS) * EXPERTS_PER_GROUP
    fl = jnp.where(lane >= e_lo, jnp.where(lane < e_lo + EXPERTS_PER_GROUP, lg, -jnp.inf), -jnp.inf)
    fe = jnp.exp(fl - jnp.max(fl, axis=-1, keepdims=True))
    pf = fe / jnp.sum(fe, axis=-1, keepdims=True)
    p1 = jnp.max(pf, axis=-1, keepdims=True)
    i1 = jnp.min(jnp.where(pf == p1, lane, ROUTE_LANES), axis=-1, keepdims=True)
    pr = jnp.where(lane == i1, -1.0, jnp.where(fl == -jnp.inf, -1.0, pf))
    p2 = jnp.max(pr, axis=-1, keepdims=True)
    i2 = jnp.min(jnp.where(pr == p2, lane, ROUTE_LANES), axis=-1, keepdims=True)
    psum = p1 + p2
    w1 = pg_top * p1 / psum
    w2 = pg_top * p2 / psum
    comb = jnp.where(lane == i1, w1, jnp.where(lane == i2, w2, 0.0))
    comb_ref[0] = comb[:, :N_EXPERTS]


def _mix_norm_route(attn, u, x, w_pool, pool_scale, w_out, ln_g, ln_b, w_r1, b_r1, w_r2, b_r2, *, tm=256):
    B, S, D = x.shape
    wr = jnp.concatenate([w_r2, w_r1, jnp.zeros((D, ROUTE_LANES - N_EXPERTS - N_GROUPS), w_r1.dtype)],
                         axis=1).astype(BF16)
    br = jnp.concatenate([b_r2, b_r1, jnp.zeros((ROUTE_LANES - N_EXPERTS - N_GROUPS,), b_r1.dtype)])[None, :]
    hpt = tm // POOL_HALO
    return pl.pallas_call(
        functools.partial(_mix_kernel, tm=tm),
        out_shape=(jax.ShapeDtypeStruct((B, S, D), F32),
                   jax.ShapeDtypeStruct((B, S, D), BF16),
                   jax.ShapeDtypeStruct((B, S, N_EXPERTS), F32)),
        grid=(B, S // tm),
        in_specs=[
            pl.BlockSpec((1, tm, ATTN_WIDTH), lambda b, i: (b, i, 0)),
            pl.BlockSpec((1, tm, POOL_WIDTH), lambda b, i: (b, i, 0)),
            pl.BlockSpec((1, POOL_HALO, POOL_WIDTH), lambda b, i: (b, jnp.maximum(i * hpt - 1, 0), 0)),
            pl.BlockSpec((1, tm, D), lambda b, i: (b, i, 0)),
            pl.BlockSpec((len(POOL_WINDOWS), POOL_GROUP_DIM, POOL_GROUP_DIM), lambda b, i: (0, 0, 0)),
            pl.BlockSpec((1, POOL_WIDTH), lambda b, i: (0, 0)),
            pl.BlockSpec((D, D), lambda b, i: (0, 0)),
            pl.BlockSpec((1, D), lambda b, i: (0, 0)),
            pl.BlockSpec((1, D), lambda b, i: (0, 0)),
            pl.BlockSpec((D, ROUTE_LANES), lambda b, i: (0, 0)),
            pl.BlockSpec((1, ROUTE_LANES), lambda b, i: (0, 0)),
        ],
        out_specs=(pl.BlockSpec((1, tm, D), lambda b, i: (b, i, 0)),
                   pl.BlockSpec((1, tm, D), lambda b, i: (b, i, 0)),
                   pl.BlockSpec((1, tm, N_EXPERTS), lambda b, i: (b, i, 0))),
        compiler_params=pltpu.CompilerParams(
            dimension_semantics=("arbitrary", "arbitrary"), vmem_limit_bytes=VMEM_LIMIT),
        name="mix_norm",
    )(attn, u, u, x, w_pool.astype(BF16), pool_scale[None, :], w_out.astype(BF16),
      ln_g[None, :], ln_b[None, :], wr, br)


def _moe_kernel(hb_ref, h_ref, comb_ref, wgu_ref, wd_ref, g_ref, b_ref, o_ref, acc_ref, *, d_expert):
    e = pl.program_id(1)

    @pl.when(e == 0)
    def _():
        acc_ref[...] = jnp.zeros(acc_ref.shape, F32)

    gu = jnp.dot(hb_ref[...], wgu_ref[0], preferred_element_type=F32)
    gate = gu[:, :d_expert]
    a = gate * jax.nn.sigmoid(gate) * gu[:, d_expert:]
    y = jnp.dot(a.astype(BF16), wd_ref[0], preferred_element_type=F32)
    comb = comb_ref[...]
    lane = lax.broadcasted_iota(I32, comb.shape, 1)
    ce = jnp.sum(jnp.where(lane == e, comb, 0.0), axis=-1, keepdims=True)
    acc_ref[...] += ce * y

    @pl.when(e == pl.num_programs(1) - 1)
    def _():
        o_ref[...] = _layer_norm(DEEPNORM_ALPHA * h_ref[...] + acc_ref[...], g_ref[...], b_ref[...])


def _moe_norm(hb, h, comb, w_gate, w_up, w_down, ln_g, ln_b, *, tm=1024):
    T, D = h.shape
    E, _, F = w_gate.shape
    wgu = jnp.concatenate([w_gate, w_up], axis=-1).astype(BF16)
    wd = w_down.astype(BF16)
    return pl.pallas_call(
        functools.partial(_moe_kernel, d_expert=F),
        out_shape=jax.ShapeDtypeStruct((T, D), F32),
        grid=(T // tm, E),
        in_specs=[
            pl.BlockSpec((tm, D), lambda t, e: (t, 0)),
            pl.BlockSpec((tm, D), lambda t, e: (t, 0)),
            pl.BlockSpec((tm, E), lambda t, e: (t, 0)),
            pl.BlockSpec((1, D, 2 * F), lambda t, e: (e, 0, 0)),
            pl.BlockSpec((1, F, D), lambda t, e: (e, 0, 0)),
            pl.BlockSpec((1, D), lambda t, e: (0, 0)),
            pl.BlockSpec((1, D), lambda t, e: (0, 0)),
        ],
        out_specs=pl.BlockSpec((tm, D), lambda t, e: (t, 0)),
        scratch_shapes=[pltpu.VMEM((tm, D), F32)],
        compiler_params=pltpu.CompilerParams(
            dimension_semantics=("arbitrary", "arbitrary"), vmem_limit_bytes=VMEM_LIMIT),
        name="moe_norm",
    )(hb, h, comb, wgu, wd, ln_g[None, :], ln_b[None, :])


def kernel(x, w_in, w_pool, pool_scale, w_out, rel_bias, ln1_g, ln1_b, w_r1, b_r1, w_r2, b_r2,
           w_gate, w_up, w_down, ln2_g, ln2_b):
    B, S, D = x.shape
    assert w_in.shape[0] == DEPTH == 1 and S % CHUNK == 0
    bias = _bias_tiles(rel_bias)
    q2t, vt, iqt, iwt, k, ik, u = _in_proj(x, w_in[0])
    attn = _dsa_attention(q2t, iqt, iwt, k, vt, ik, bias)
    h, hb, comb = _mix_norm_route(attn, u, x, w_pool[0], pool_scale[0], w_out[0], ln1_g[0], ln1_b[0],
                                  w_r1[0], b_r1[0], w_r2[0], b_r2[0])
    out = _moe_norm(hb.reshape(B * S, D), h.reshape(B * S, D), comb.reshape(B * S, N_EXPERTS),
                    w_gate[0], w_up[0], w_down[0], ln2_g[0], ln2_b[0])
    return out.reshape(B, S, D)
```

```python
import functools
import math

import numpy as np
import jax
import jax.numpy as jnp
from jax import lax
from jax.experimental import pallas as pl
from jax.experimental.pallas import tpu as pltpu

F32 = jnp.float32
BF16 = jnp.bfloat16
I32 = jnp.int32

ATTN_HEADS = 8
HEAD_DIM = 64
ATTN_WIDTH = ATTN_HEADS * HEAD_DIM
IDX_HEADS = 8
IDX_DIM = 64
TOPK_MAX = 256
POOL_WINDOWS = (2, 4, 8, 16)
POOL_GROUP_DIM = 128
POOL_WIDTH = len(POOL_WINDOWS) * POOL_GROUP_DIM
POOL_HALO = 16
REL_BUCKETS = 32
REL_MAX_DIST = 128
N_GROUPS = 4
EXPERTS_PER_GROUP = 8
N_EXPERTS = N_GROUPS * EXPERTS_PER_GROUP
LN_EPS = 1e-5
DEPTH = 1
DEEPNORM_ALPHA = (2 * DEPTH) ** 0.25
LOG2E = math.log2(math.e)

LANES = 128
CHUNK = 256
INT_MIN = -2 ** 31
NEG_BIG = -1e30
VMEM_LIMIT = 56 * 1024 * 1024


def _rel_bucket_table(n):
    max_exact = REL_BUCKETS // 2
    d = np.arange(n)
    nf = np.maximum(d, 1).astype(np.float32)
    ratio = np.log(nf / np.float32(max_exact)) / np.float32(math.log(REL_MAX_DIST / max_exact))
    large = max_exact + (ratio * np.float32(REL_BUCKETS - max_exact)).astype(np.int32)
    large = np.minimum(large, REL_BUCKETS - 1)
    return np.where(d < max_exact, d, large).astype(np.int32)


def _near_bucket_tiles():
    tbl = _rel_bucket_table(2 * CHUNK)
    a = np.arange(CHUNK)[:, None]
    b = np.arange(CHUNK)[None, :]
    tiles = [tbl[np.maximum(delta * CHUNK + b - a, 0)] for delta in (0, 1)]
    return np.stack(tiles).astype(np.int32)


FAR_BUCKET = REL_BUCKETS - 1
assert int(_rel_bucket_table(2 * CHUNK)[CHUNK + 1:].min()) == FAR_BUCKET


def _bias_kernel(rb_ref, bucket_ref, o_ref):
    h = pl.program_id(1)
    bk = bucket_ref[0]
    far = rb_ref[FAR_BUCKET, h]
    acc = jnp.zeros(bk.shape, F32)
    for n in range(REL_BUCKETS):
        acc = jnp.where(bk == n, rb_ref[n, h] - far, acc)
    o_ref[0, 0] = acc * LOG2E


def _bias_tiles(rel_bias):
    buckets = jnp.asarray(_near_bucket_tiles())
    return pl.pallas_call(
        _bias_kernel,
        out_shape=jax.ShapeDtypeStruct((2, ATTN_HEADS, CHUNK, CHUNK), F32),
        grid=(2, ATTN_HEADS),
        in_specs=[pl.BlockSpec(memory_space=pltpu.SMEM),
                  pl.BlockSpec((1, CHUNK, CHUNK), lambda d, h: (d, 0, 0))],
        out_specs=pl.BlockSpec((1, 1, CHUNK, CHUNK), lambda d, h: (d, h, 0, 0)),
        name="bias_tiles",
    )(rel_bias, buckets)


Q2_ROWS = 2 * ATTN_WIDTH
V_SLOT = HEAD_DIM + 16
VT_ROWS = ATTN_HEADS * V_SLOT
WT_Q2, WT_V, WT_IQ, WT_IW = 0, Q2_ROWS, Q2_ROWS + VT_ROWS, Q2_ROWS + VT_ROWS + ATTN_WIDTH
WT_ROWS = WT_IW + 16
WN_K, WN_U, WN_IK = 0, ATTN_WIDTH, ATTN_WIDTH + POOL_WIDTH
WN_COLS = WN_IK + LANES


def _proj_kernel(x_ref, wt_ref, wn_ref, q2t_ref, vt_ref, iqt_ref, iwt_ref, k_ref, ik_ref, u_ref, *, tm):
    xb = x_ref[0].astype(BF16)
    t = lax.dot_general(wt_ref[...], xb, (((1,), (1,)), ((), ())),
                        preferred_element_type=F32)
    q2t_ref[0] = (t[WT_Q2:WT_V] * (HEAD_DIM ** -0.5 * LOG2E)).astype(BF16)
    iqt_ref[0] = (t[WT_IQ:WT_IW] * (IDX_DIM ** -0.5)).astype(BF16)
    iwt_ref[0] = t[WT_IW:WT_IW + IDX_HEADS] * (IDX_HEADS ** -0.5)
    slot_row = lax.broadcasted_iota(I32, (VT_ROWS, tm), 0) % V_SLOT
    vt = jnp.where(slot_row >= HEAD_DIM, 1.0, t[WT_V:WT_IQ]).astype(BF16)
    for j in range(tm // CHUNK):
        vt_ref[0, j] = vt[:, j * CHUNK:(j + 1) * CHUNK]
    n = jnp.dot(xb, wn_ref[...], preferred_element_type=F32)
    k_ref[0] = n[:, WN_K:WN_U].astype(BF16)
    u_ref[0] = n[:, WN_U:WN_IK]
    ik_ref[0] = n[:, WN_IK:WN_IK + IDX_DIM].astype(BF16)


def _in_proj(x, w_in, *, tm=512):
    B, S, D = x.shape
    q_off, k_off, v_off = 0, ATTN_WIDTH, 2 * ATTN_WIDTH
    iq_off = 3 * ATTN_WIDTH
    ik_off = iq_off + IDX_HEADS * IDX_DIM
    iw_off = ik_off + IDX_DIM
    p_off = iw_off + IDX_HEADS
    wq = w_in[:, q_off:k_off].T.reshape(ATTN_HEADS // 2, 2, HEAD_DIM, D)
    zq = jnp.zeros((ATTN_HEADS // 2, HEAD_DIM, D), w_in.dtype)
    wq2 = jnp.stack([wq[:, 0], zq, zq, wq[:, 1]], axis=1).reshape(Q2_ROWS, D)
    wv = w_in[:, v_off:iq_off].T.reshape(ATTN_HEADS, HEAD_DIM, D)
    wv = jnp.pad(wv, ((0, 0), (0, V_SLOT - HEAD_DIM), (0, 0))).reshape(VT_ROWS, D)
    wt = jnp.concatenate([
        wq2, wv, w_in[:, iq_off:ik_off].T, w_in[:, iw_off:p_off].T,
        jnp.zeros((WT_ROWS - WT_IW - IDX_HEADS, D), w_in.dtype)], axis=0).astype(BF16)
    wn = jnp.concatenate([
        w_in[:, k_off:v_off], w_in[:, p_off:], w_in[:, ik_off:iw_off],
        jnp.zeros((D, WN_COLS - WN_IK - IDX_DIM), w_in.dtype)], axis=1).astype(BF16)
    nt = S // tm
    cpt = tm // CHUNK
    outs = pl.pallas_call(
        functools.partial(_proj_kernel, tm=tm),
        out_shape=(
            jax.ShapeDtypeStruct((B, Q2_ROWS, S), BF16),
            jax.ShapeDtypeStruct((B, S // CHUNK, VT_ROWS, CHUNK), BF16),
            jax.ShapeDtypeStruct((B, IDX_HEADS * IDX_DIM, S), BF16),
            jax.ShapeDtypeStruct((B, IDX_HEADS, S), F32),
            jax.ShapeDtypeStruct((B, S, ATTN_WIDTH), BF16),
            jax.ShapeDtypeStruct((B, S, IDX_DIM), BF16),
            jax.ShapeDtypeStruct((B, S, POOL_WIDTH), F32),
        ),
        grid=(B, nt),
        in_specs=[pl.BlockSpec((1, tm, D), lambda b, i: (b, i, 0)),
                  pl.BlockSpec((WT_ROWS, D), lambda b, i: (0, 0)),
                  pl.BlockSpec((D, WN_COLS), lambda b, i: (0, 0))],
        out_specs=(
            pl.BlockSpec((1, Q2_ROWS, tm), lambda b, i: (b, 0, i)),
            pl.BlockSpec((1, cpt, VT_ROWS, CHUNK), lambda b, i: (b, i, 0, 0)),
            pl.BlockSpec((1, IDX_HEADS * IDX_DIM, tm), lambda b, i: (b, 0, i)),
            pl.BlockSpec((1, IDX_HEADS, tm), lambda b, i: (b, 0, i)),
            pl.BlockSpec((1, tm, ATTN_WIDTH), lambda b, i: (b, i, 0)),
            pl.BlockSpec((1, tm, IDX_DIM), lambda b, i: (b, i, 0)),
            pl.BlockSpec((1, tm, POOL_WIDTH), lambda b, i: (b, i, 0)),
        ),
        compiler_params=pltpu.CompilerParams(
            dimension_semantics=("arbitrary", "arbitrary"), vmem_limit_bytes=VMEM_LIMIT),
        name="in_proj",
    )(x, wt, wn)
    return outs


SNAP_FIRST = 14
SNAP_EVERY = 5
MAX_STEPS = SNAP_FIRST + 5 * 17
ROW_LO, ROW_HI, ROW_CNT, ROW_PROBE = 0, 1, 2, 3


def _order_key(bits):
    return bits ^ ((bits >> 31) & 0x7FFFFFFF)


def _attn_kernel(q2t_ref, iqt_ref, iwt_ref, k_ref, vt_ref, ik_ref, bias_ref, o_ref,
                 sc_ref, st_ref, neg_ref, lg_ref, p_ref, m_ref, al_ref, acc_ref,
                 *, topk):
    i = pl.program_id(1)
    nch = i + 1
    C = CHUNK
    row = lax.broadcasted_iota(I32, (C, C), 0)
    col = lax.broadcasted_iota(I32, (C, C), 1)

    def fold_rows(op, x):
        return op(x.reshape(C // 8, 8, C), axis=0)

    def score_body(c, carry):
        smin, smax = carry
        ikc = ik_ref[0, c]
        s = jnp.zeros((C, C), F32)
        for j in range(IDX_HEADS):
            d = jnp.dot(ikc, iqt_ref[0, j * IDX_DIM:(j + 1) * IDX_DIM, :],
                        preferred_element_type=F32)
            s = s + iwt_ref[0, j:j + 1, :] * jnp.maximum(d, 0.0)
        causal = (c * C + row) <= (i * C + col)
        sc_ref[c] = jnp.where(causal, s, -jnp.inf)
        smin = jnp.minimum(smin, fold_rows(jnp.min, jnp.where(causal, s, jnp.inf)))
        smax = jnp.maximum(smax, fold_rows(jnp.max, jnp.where(causal, s, -jnp.inf)))
        return smin, smax

    smin, smax = lax.fori_loop(0, nch, score_body,
                               (jnp.full((8, C), jnp.inf, F32), jnp.full((8, C), -jnp.inf, F32)))
    smin = jnp.min(smin, axis=0, keepdims=True)
    smax = jnp.max(smax, axis=0, keepdims=True)

    pos = i * C + lax.broadcasted_iota(I32, (1, C), 1)
    n_keys = (pos + 1).astype(F32)
    k_eff = jnp.minimum(pos + 1, topk).astype(F32)

    @pl.when(nch % 2 == 1)
    def _():
        sc_ref[nch] = jnp.full((C, C), -jnp.inf, F32)

    def count_ge(mid):
        def body(cp, acc):
            for c in (2 * cp, 2 * cp + 1):
                acc = acc + fold_rows(jnp.sum, jnp.where(sc_ref[c] >= mid, 1.0, 0.0))
            return acc
        part = lax.fori_loop(0, lax.shift_right_logical(nch + 1, 1), body, jnp.zeros((8, C), F32))
        return jnp.sum(part, axis=0, keepdims=True)

    def load_state():
        return (st_ref[ROW_LO:ROW_LO + 1, :], st_ref[ROW_HI:ROW_HI + 1, :],
                st_ref[ROW_CNT:ROW_CNT + 1, :], st_ref[ROW_PROBE:ROW_PROBE + 1, :])

    def store_state(lo, hi, cnt_lo, probe):
        st_ref[ROW_LO:ROW_LO + 1, :] = lo
        st_ref[ROW_HI:ROW_HI + 1, :] = hi
        st_ref[ROW_CNT:ROW_CNT + 1, :] = cnt_lo
        st_ref[ROW_PROBE:ROW_PROBE + 1, :] = probe

    def open_cols(lo, hi, cnt_lo):
        return jnp.where(cnt_lo > k_eff, jnp.where(hi > lo, 1.0, 0.0), 0.0)

    def bisect(it, lo, hi, cnt_lo, probe):
        klo = _order_key(pltpu.bitcast(lo, I32))
        khi = _order_key(pltpu.bitcast(hi, I32))
        i_mid = pltpu.bitcast(_order_key((klo >> 1) + (khi >> 1) + (klo & khi & 1)), F32)
        mid = jnp.where(jnp.logical_and(it > SNAP_FIRST, it % 2 == 1), i_mid, 0.5 * lo + 0.5 * hi)
        probing = probe > lo
        mid = jnp.where(probing, probe, mid)
        ok = open_cols(lo, hi, cnt_lo) * jnp.where(mid > lo, jnp.where(mid < hi, 1.0, 0.0), 0.0)
        cnt = count_ge(mid)
        up = ok * jnp.where(cnt >= k_eff, 1.0, 0.0)
        dn = ok - up
        new_hi = jnp.where(dn + up * jnp.where(probing, 1.0, 0.0) > 0.0, mid, hi)
        return (jnp.where(up > 0.0, mid, lo), new_hi, jnp.where(up > 0.0, cnt, cnt_lo),
                jnp.full((1, C), -jnp.inf, F32))

    def snap(it, lo, hi, cnt_lo, probe):
        def body(c, carry):
            vmin, vmax = carry
            sc = sc_ref[c]
            vmin = jnp.minimum(vmin, fold_rows(jnp.min, jnp.where(sc >= lo, sc, jnp.inf)))
            vmax = jnp.maximum(vmax, fold_rows(jnp.max, jnp.where(sc < hi, sc, -jnp.inf)))
            return vmin, vmax
        vmin, vmax = lax.fori_loop(0, nch, body, (jnp.full((8, C), jnp.inf, F32),
                                                  jnp.full((8, C), -jnp.inf, F32)))
        vmin = jnp.min(vmin, axis=0, keepdims=True)
        vmax = jnp.max(vmax, axis=0, keepdims=True)
        is_open = open_cols(lo, hi, cnt_lo) > 0.0
        single = vmin >= vmax
        return (jnp.where(is_open, vmin, lo),
                jnp.where(is_open, jnp.where(single, vmin, hi), hi),
                cnt_lo,
                jnp.where(is_open, jnp.where(single, -jnp.inf, vmax), -jnp.inf))

    def search_cond(st):
        it, n_open = st
        return jnp.logical_and(it < MAX_STEPS, n_open > 0.0)

    def search_body(st):
        it, _ = st
        do_snap = jnp.logical_and(it >= SNAP_FIRST, (it - SNAP_FIRST) % SNAP_EVERY == 0)

        @pl.when(do_snap)
        def _():
            store_state(*snap(it, *load_state()))

        @pl.when(jnp.logical_not(do_snap))
        def _():
            store_state(*bisect(it, *load_state()))

        lo, hi, cnt_lo, _ = load_state()
        return it + 1, jnp.max(open_cols(lo, hi, cnt_lo))

    kmax = _order_key(pltpu.bitcast(smax, I32))
    hi0 = pltpu.bitcast(_order_key(jnp.where(kmax == 2 ** 31 - 1, kmax, kmax + 1)), F32)
    store_state(smin, hi0, n_keys, jnp.full((1, C), -jnp.inf, F32))
    lax.while_loop(search_cond, search_body, (jnp.int32(0), jnp.max(open_cols(smin, hi0, n_keys))))
    th, _, cnt_th, _ = load_state()

    surplus = cnt_th - k_eff

    @pl.when(jnp.max(surplus) > 0.0)
    def _():
        later = jnp.where(col > row, 1.0, 0.0).astype(BF16)

        def drop_body(r, after):
            c = nch - 1 - r
            sc = sc_ref[c]
            tied = jnp.where(sc == th, 1.0, 0.0)
            follow = jnp.dot(later, tied.astype(BF16), preferred_element_type=F32) + after
            sc_ref[c] = jnp.where(tied * jnp.where(follow < surplus, 1.0, 0.0) > 0.0, -jnp.inf, sc)
            return after + jnp.sum(fold_rows(jnp.sum, tied), axis=0, keepdims=True)

        lax.fori_loop(0, nch, drop_body, jnp.zeros((1, C), F32))

    m_ref[...] = jnp.full(m_ref.shape, NEG_BIG, F32)
    acc_ref[...] = jnp.zeros(acc_ref.shape, F32)

    def attend(c, near):
        neg = jnp.where(sc_ref[c] >= th, 0.0, -jnp.inf)
        if near == 0:
            neg = jnp.where(row <= col, neg, -jnp.inf)
        neg_ref[...] = neg
        for h in range(ATTN_HEADS):
            p2 = (h // 2) * 2 * HEAD_DIM
            lg = jnp.dot(k_ref[0, c, :, p2:p2 + 2 * HEAD_DIM],
                         q2t_ref[0, 2 * HEAD_DIM * h:2 * HEAD_DIM * (h + 1), :],
                         preferred_element_type=F32)
            if near is not None:
                lg = lg + bias_ref[near, h]
            lg = lg + neg_ref[...]
            lg_ref[h] = lg
            m_old = m_ref[h:h + 1, :]
            m_new = jnp.maximum(m_old, jnp.max(lg, axis=0, keepdims=True))
            al_ref[h:h + 1, :] = jnp.exp2(m_old - m_new)
            m_ref[h:h + 1, :] = m_new
        for h in range(ATTN_HEADS):
            p_ref[h] = jnp.exp2(lg_ref[h] - m_ref[h:h + 1, :]).astype(BF16)
        for h in range(ATTN_HEADS):
            hs = slice(h * V_SLOT, (h + 1) * V_SLOT)
            pv = jnp.dot(vt_ref[0, c, hs, :], p_ref[h], preferred_element_type=F32)
            acc_ref[hs, :] = al_ref[h:h + 1, :] * acc_ref[hs, :] + pv

    def far_body(c, carry):
        attend(c, None)
        return carry

    lax.fori_loop(0, jnp.maximum(i - 1, 0), far_body, 0)

    @pl.when(i >= 1)
    def _():
        attend(i - 1, 1)

    attend(i, 0)

    out_t = jnp.concatenate(
        [acc_ref[h * V_SLOT:h * V_SLOT + HEAD_DIM, :] / acc_ref[h * V_SLOT + HEAD_DIM:h * V_SLOT + HEAD_DIM + 1, :]
         for h in range(ATTN_HEADS)], axis=0)
    o_ref[0] = out_t.T.astype(o_ref.dtype)


def _dsa_attention(q2t, iqt, iwt, k, vt, ik, bias):
    B, S, _ = k.shape
    nq = S // CHUNK
    assert nq % 2 == 0
    topk = min(TOPK_MAX, S // 4)
    k4 = k.reshape(B, nq, CHUNK, ATTN_WIDTH)
    ik4 = ik.reshape(B, nq, CHUNK, IDX_DIM)
    return pl.pallas_call(
        functools.partial(_attn_kernel, topk=topk),
        out_shape=jax.ShapeDtypeStruct((B, S, ATTN_WIDTH), BF16),
        grid=(B, nq),
        in_specs=[
            pl.BlockSpec((1, Q2_ROWS, CHUNK), lambda b, i: (b, 0, i)),
            pl.BlockSpec((1, IDX_HEADS * IDX_DIM, CHUNK), lambda b, i: (b, 0, i)),
            pl.BlockSpec((1, IDX_HEADS, CHUNK), lambda b, i: (b, 0, i)),
            pl.BlockSpec((1, nq, CHUNK, ATTN_WIDTH), lambda b, i: (b, 0, 0, 0)),
            pl.BlockSpec((1, nq, VT_ROWS, CHUNK), lambda b, i: (b, 0, 0, 0)),
            pl.BlockSpec((1, nq, CHUNK, IDX_DIM), lambda b, i: (b, 0, 0, 0)),
            pl.BlockSpec((2, ATTN_HEADS, CHUNK, CHUNK), lambda b, i: (0, 0, 0, 0)),
        ],
        out_specs=pl.BlockSpec((1, CHUNK, ATTN_WIDTH), lambda b, i: (b, i, 0)),
        scratch_shapes=[
            pltpu.VMEM((nq, CHUNK, CHUNK), F32),
            pltpu.VMEM((8, CHUNK), F32),
            pltpu.VMEM((CHUNK, CHUNK), F32),
            pltpu.VMEM((ATTN_HEADS, CHUNK, CHUNK), F32),
            pltpu.VMEM((ATTN_HEADS, CHUNK, CHUNK), BF16),
            pltpu.VMEM((ATTN_HEADS, CHUNK), F32),
            pltpu.VMEM((ATTN_HEADS, CHUNK), F32),
            pltpu.VMEM((VT_ROWS, CHUNK), F32),
        ],
        compiler_params=pltpu.CompilerParams(
            dimension_semantics=("arbitrary", "arbitrary"), vmem_limit_bytes=VMEM_LIMIT),
        name="dsa_attn",
    )(q2t, iqt, iwt, k4, vt, ik4, bias)


ROUTE_LANES = LANES


def _layer_norm(y, g, b):
    mu = jnp.mean(y, axis=-1, keepdims=True)
    yc = y - mu
    var = jnp.mean(yc * yc, axis=-1, keepdims=True)
    return yc * lax.rsqrt(var + LN_EPS) * g + b


def _mix_kernel(attn_ref, u_ref, halo_ref, x_ref, wpool_ref, pscale_ref, wout_ref, g_ref, b_ref,
                wr_ref, br_ref, h_ref, route_ref, rw_ref, *, tm):
    i = pl.program_id(1)
    halo = jnp.where(i > 0, halo_ref[0], 0.0)
    ue = jnp.concatenate([halo, u_ref[0]], axis=0)
    pos = i * tm + lax.broadcasted_iota(I32, (tm, 1), 0)
    mixed = []
    for g, w in enumerate(POOL_WINDOWS):
        gs = slice(g * POOL_GROUP_DIM, (g + 1) * POOL_GROUP_DIM)
        ch = ue[:, gs]
        win = ch
        step = 1
        while step < w:
            win = win + pltpu.roll(win, step, axis=0)
            step *= 2
        cnt = jnp.minimum(pos + 1, w).astype(F32)
        pooled = win[POOL_HALO:] / cnt - ch[POOL_HALO:]
        mg = jnp.dot(pooled.astype(BF16), wpool_ref[g], preferred_element_type=F32)
        mixed.append((mg * pscale_ref[:, gs]).astype(BF16))
    cat = jnp.concatenate([attn_ref[0]] + mixed, axis=-1)
    mix = jnp.dot(cat, wout_ref[...], preferred_element_type=F32)
    h = _layer_norm(DEEPNORM_ALPHA * x_ref[0] + mix, g_ref[...], b_ref[...])
    h_ref[0] = h
    hb = h.astype(BF16)

    lg = jnp.dot(hb, wr_ref[...], preferred_element_type=F32) + br_ref[...]
    lane = lax.broadcasted_iota(I32, (tm, ROUTE_LANES), 1)
    gl = jnp.where(lane >= N_EXPERTS, jnp.where(lane < N_EXPERTS + N_GROUPS, lg, -jnp.inf), -jnp.inf)
    ge = jnp.exp(gl - jnp.max(gl, axis=-1, keepdims=True))
    pg = ge / jnp.sum(ge, axis=-1, keepdims=True)
    pg_top = jnp.max(pg, axis=-1, keepdims=True)
    g_lane = jnp.min(jnp.where(pg == pg_top, lane, ROUTE_LANES), axis=-1, keepdims=True)
    e_lo = (g_lane - N_EXPERTS) * EXPERTS_PER_GROUP
    fl = jnp.where(lane >= e_lo, jnp.where(lane < e_lo + EXPERTS_PER_GROUP, lg, -jnp.inf), -jnp.inf)
    fe = jnp.exp(fl - jnp.max(fl, axis=-1, keepdims=True))
    pf = fe / jnp.sum(fe, axis=-1, keepdims=True)
    p1 = jnp.max(pf, axis=-1, keepdims=True)
    i1 = jnp.min(jnp.where(pf == p1, lane, ROUTE_LANES), axis=-1, keepdims=True)
    pr = jnp.where(lane == i1, -1.0, jnp.where(fl == -jnp.inf, -1.0, pf))
    p2 = jnp.max(pr, axis=-1, keepdims=True)
    i2 = jnp.min(jnp.where(pr == p2, lane, ROUTE_LANES), axis=-1, keepdims=True)
    psum = p1 + p2
    w1 = pg_top * p1 / psum
    w2 = pg_top * p2 / psum
    ids = jnp.where(lane == 0, i1, jnp.where(lane == 1, i2, 0)).astype(F32)
    route_ref[...] = ids.T[0:8, :].astype(I32)
    lane8 = lax.broadcasted_iota(I32, (tm, 8), 1)
    rw_ref[0] = jnp.where(lane8 == 0, w1, jnp.where(lane8 == 1, w2, 0.0))


def _mix_norm_route(attn, u, x, w_pool, pool_scale, w_out, ln_g, ln_b, w_r1, b_r1, w_r2, b_r2, *, tm=256):
    B, S, D = x.shape
    wr = jnp.concatenate([w_r2, w_r1, jnp.zeros((D, ROUTE_LANES - N_EXPERTS - N_GROUPS), w_r1.dtype)],
                         axis=1).astype(BF16)
    br = jnp.concatenate([b_r2, b_r1, jnp.zeros((ROUTE_LANES - N_EXPERTS - N_GROUPS,), b_r1.dtype)])[None, :]
    hpt = tm // POOL_HALO
    return pl.pallas_call(
        functools.partial(_mix_kernel, tm=tm),
        out_shape=(jax.ShapeDtypeStruct((B, S, D), F32),
                   jax.ShapeDtypeStruct((8, B * S), I32),
                   jax.ShapeDtypeStruct((B, S, 8), F32)),
        grid=(B, S // tm),
        in_specs=[
            pl.BlockSpec((1, tm, ATTN_WIDTH), lambda b, i: (b, i, 0)),
            pl.BlockSpec((1, tm, POOL_WIDTH), lambda b, i: (b, i, 0)),
            pl.BlockSpec((1, POOL_HALO, POOL_WIDTH), lambda b, i: (b, jnp.maximum(i * hpt - 1, 0), 0)),
            pl.BlockSpec((1, tm, D), lambda b, i: (b, i, 0)),
            pl.BlockSpec((len(POOL_WINDOWS), POOL_GROUP_DIM, POOL_GROUP_DIM), lambda b, i: (0, 0, 0)),
            pl.BlockSpec((1, POOL_WIDTH), lambda b, i: (0, 0)),
            pl.BlockSpec((D, D), lambda b, i: (0, 0)),
            pl.BlockSpec((1, D), lambda b, i: (0, 0)),
            pl.BlockSpec((1, D), lambda b, i: (0, 0)),
            pl.BlockSpec((D, ROUTE_LANES), lambda b, i: (0, 0)),
            pl.BlockSpec((1, ROUTE_LANES), lambda b, i: (0, 0)),
        ],
        out_specs=(pl.BlockSpec((1, tm, D), lambda b, i: (b, i, 0)),
                   pl.BlockSpec((8, tm), lambda b, i: (0, b * (S // tm) + i)),
                   pl.BlockSpec((1, tm, 8), lambda b, i: (b, i, 0))),
        compiler_params=pltpu.CompilerParams(
            dimension_semantics=("arbitrary", "arbitrary"), vmem_limit_bytes=VMEM_LIMIT),
        name="mix_norm",
    )(attn, u, u, x, w_pool.astype(BF16), pool_scale[None, :], w_out.astype(BF16),
      ln_g[None, :], ln_b[None, :], wr, br)


MOE_TM = 256
PLAN_TB = 1024
GATHER_UNROLL = 8


def _moe_rows(n_tokens):
    return 2 * n_tokens + N_EXPERTS * MOE_TM


def _plan_kernel(route_ref, pos_ref, tile_ref, cnt_ref, off_ref, carry_ref, *, n_tile_lanes):
    ph = pl.program_id(0)
    b = pl.program_id(1)
    tb = route_ref.shape[1]
    esub = lax.broadcasted_iota(I32, (N_EXPERTS, tb), 0)
    a1 = jnp.where(esub == route_ref[0:1, :], 1.0, 0.0)
    a2 = jnp.where(esub == route_ref[1:2, :], 1.0, 0.0)
    a = a1 + a2
    n_here = jnp.sum(a, axis=1, keepdims=True)

    @pl.when(jnp.logical_and(ph == 0, b == 0))
    def _():
        cnt_ref[...] = jnp.zeros(cnt_ref.shape, F32)

    @pl.when(ph == 0)
    def _():
        cnt_ref[...] += n_here

    @pl.when(jnp.logical_and(ph == 1, b == 0))
    def _():
        n_tile = jnp.floor((cnt_ref[...] + (MOE_TM - 1)) * (1.0 / MOE_TM))
        er = lax.broadcasted_iota(I32, (N_EXPERTS, N_EXPERTS), 0)
        ec = lax.broadcasted_iota(I32, (N_EXPERTS, N_EXPERTS), 1)
        before = jnp.where(ec < er, 1.0, 0.0).astype(BF16)
        t_off = jnp.dot(before, n_tile.astype(BF16), preferred_element_type=F32)
        off_ref[...] = t_off * MOE_TM
        carry_ref[...] = jnp.zeros(carry_ref.shape, F32)
        j = lax.broadcasted_iota(I32, (N_EXPERTS, n_tile_lanes), 1).astype(F32)
        owner = jnp.sum(jnp.where(t_off[:, 0:1] <= j, 1.0, 0.0), axis=0, keepdims=True) - 1.0
        total = jnp.sum(n_tile[:, 0:1], axis=0, keepdims=True)
        used = jnp.where(j[0:1, :] < total, 1.0, 0.0)
        row8 = lax.broadcasted_iota(I32, (8, n_tile_lanes), 0)
        tile_ref[...] = jnp.where(row8 == 0, owner, jnp.where(row8 == 1, used, 0.0)).astype(I32)

    @pl.when(ph == 1)
    def _():
        tr = lax.broadcasted_iota(I32, (tb, tb), 0)
        tc = lax.broadcasted_iota(I32, (tb, tb), 1)
        earlier = jnp.where(tr < tc, 1.0, 0.0).astype(BF16)
        seen = jnp.dot(a.astype(BF16), earlier, preferred_element_type=F32)
        dest = seen + carry_ref[:, 0:1] + off_ref[:, 0:1]
        p1 = jnp.sum(a1 * dest, axis=0, keepdims=True)
        p2 = jnp.sum(a2 * dest, axis=0, keepdims=True)
        row8 = lax.broadcasted_iota(I32, (8, tb), 0)
        pos_ref[...] = jnp.where(row8 == 0, p1, jnp.where(row8 == 1, p2, 0.0)).astype(I32)
        carry_ref[...] += n_here


def _moe_plan(route_t):
    _, T = route_t.shape
    n_tiles = _moe_rows(T) // MOE_TM
    n_tile_lanes = -(-n_tiles // LANES) * LANES
    nb = T // PLAN_TB
    return pl.pallas_call(
        functools.partial(_plan_kernel, n_tile_lanes=n_tile_lanes),
        out_shape=(jax.ShapeDtypeStruct((8, T), I32), jax.ShapeDtypeStruct((8, n_tile_lanes), I32)),
        grid=(2, nb),
        in_specs=[pl.BlockSpec((8, PLAN_TB), lambda ph, b: (0, b))],
        out_specs=(pl.BlockSpec((8, PLAN_TB), lambda ph, b: (0, b * ph)),
                   pl.BlockSpec((8, n_tile_lanes), lambda ph, b: (0, 0))),
        scratch_shapes=[pltpu.VMEM((N_EXPERTS, LANES), F32),
                        pltpu.VMEM((N_EXPERTS, LANES), F32),
                        pltpu.VMEM((N_EXPERTS, LANES), F32)],
        compiler_params=pltpu.CompilerParams(dimension_semantics=("arbitrary", "arbitrary")),
        name="moe_plan",
    )(route_t)


def _row_copy(src_hbm, src_row, dst_ref, dst_row, sem):
    return pltpu.make_async_copy(src_hbm.at[pl.ds(src_row, 1)], dst_ref.at[pl.ds(dst_row, 1)], sem)


def _dispatch_kernel(p1_ref, p2_ref, h_hbm, init_hbm, out_hbm, sem, *, tb):
    del init_hbm
    base = pl.program_id(0) * tb

    def start(t, carry):
        _row_copy(h_hbm, base + t, out_hbm, p1_ref[t], sem).start()
        _row_copy(h_hbm, base + t, out_hbm, p2_ref[t], sem).start()
        return carry

    lax.fori_loop(0, tb, start, 0, unroll=GATHER_UNROLL)

    def wait(t, carry):
        _row_copy(h_hbm, 0, out_hbm, 0, sem).wait()
        _row_copy(h_hbm, 0, out_hbm, 0, sem).wait()
        return carry

    lax.fori_loop(0, tb, wait, 0, unroll=GATHER_UNROLL)


def _moe_dispatch(h, pos1, pos2, *, tb=2048):
    T, D = h.shape
    rows = _moe_rows(T)
    return pl.pallas_call(
        functools.partial(_dispatch_kernel, tb=tb),
        out_shape=jax.ShapeDtypeStruct((rows, D), h.dtype),
        grid=(T // tb,),
        in_specs=[pl.BlockSpec((tb,), lambda i: (i,), memory_space=pltpu.SMEM),
                  pl.BlockSpec((tb,), lambda i: (i,), memory_space=pltpu.SMEM),
                  pl.BlockSpec(memory_space=pl.ANY),
                  pl.BlockSpec(memory_space=pl.ANY)],
        out_specs=pl.BlockSpec(memory_space=pl.ANY),
        scratch_shapes=[pltpu.SemaphoreType.DMA(())],
        input_output_aliases={3: 0},
        compiler_params=pltpu.CompilerParams(dimension_semantics=("arbitrary",)),
        name="moe_dispatch",
    )(pos1, pos2, h, jnp.zeros((rows, D), h.dtype))


def _ffn_kernel(owner_ref, used_ref, x_ref, wg_ref, wu_ref, wd_ref, o_ref):
    j = pl.program_id(0)

    @pl.when(used_ref[j] == 1)
    def _():
        xb = x_ref[...].astype(BF16)
        gate = jnp.dot(xb, wg_ref[0].astype(BF16), preferred_element_type=F32)
        up = jnp.dot(xb, wu_ref[0].astype(BF16), preferred_element_type=F32)
        a = gate * jax.nn.sigmoid(gate) * up
        o_ref[...] = jnp.dot(a.astype(BF16), wd_ref[0].astype(BF16), preferred_element_type=F32)

    @pl.when(used_ref[j] == 0)
    def _():
        o_ref[...] = jnp.zeros(o_ref.shape, F32)


def _moe_ffn(xs, owner, used, w_gate, w_up, w_down):
    rows, D = xs.shape
    E, _, F = w_gate.shape
    return pl.pallas_call(
        _ffn_kernel,
        out_shape=jax.ShapeDtypeStruct((rows, D), F32),
        grid_spec=pltpu.PrefetchScalarGridSpec(
            num_scalar_prefetch=2,
            grid=(rows // MOE_TM,),
            in_specs=[pl.BlockSpec((MOE_TM, D), lambda j, ow, us: (j, 0)),
                      pl.BlockSpec((1, D, F), lambda j, ow, us: (ow[j], 0, 0)),
                      pl.BlockSpec((1, D, F), lambda j, ow, us: (ow[j], 0, 0)),
                      pl.BlockSpec((1, F, D), lambda j, ow, us: (ow[j], 0, 0))],
            out_specs=pl.BlockSpec((MOE_TM, D), lambda j, ow, us: (j, 0)),
        ),
        compiler_params=pltpu.CompilerParams(dimension_semantics=("arbitrary",), vmem_limit_bytes=VMEM_LIMIT),
        name="moe_ffn",
    )(owner, used, xs, w_gate, w_up, w_down)


def _combine_kernel(p1_ref, p2_ref, p1n_ref, p2n_ref, rw_ref, h_ref, g_ref, b_ref, y_hbm, o_ref,
                    buf_ref, sem, *, tm):
    s = pl.program_id(0)
    slot = s % 2

    def fetch(pa_ref, pb_ref, into):
        def start(t, carry):
            pltpu.make_async_copy(y_hbm.at[pl.ds(pa_ref[t], 1)], buf_ref.at[into, 0, pl.ds(t, 1)],
                                  sem.at[into]).start()
            pltpu.make_async_copy(y_hbm.at[pl.ds(pb_ref[t], 1)], buf_ref.at[into, 1, pl.ds(t, 1)],
                                  sem.at[into]).start()
            return carry
        lax.fori_loop(0, tm, start, 0, unroll=GATHER_UNROLL)

    @pl.when(s == 0)
    def _():
        fetch(p1_ref, p2_ref, 0)

    @pl.when(s + 1 < pl.num_programs(0))
    def _():
        fetch(p1n_ref, p2n_ref, 1 - slot)

    def wait(t, carry):
        for half in range(2):
            pltpu.make_async_copy(y_hbm.at[pl.ds(0, 1)], buf_ref.at[slot, half, pl.ds(0, 1)], sem.at[slot]).wait()
        return carry

    lax.fori_loop(0, tm, wait, 0, unroll=GATHER_UNROLL)
    y = rw_ref[:, 0:1] * buf_ref[slot, 0] + rw_ref[:, 1:2] * buf_ref[slot, 1]
    o_ref[...] = _layer_norm(DEEPNORM_ALPHA * h_ref[...] + y, g_ref[...], b_ref[...])


def _moe_combine(h, y_sorted, pos1, pos2, rw, ln_g, ln_b, *, tm=256):
    T, D = h.shape
    last = T // tm - 1
    smem = lambda imap: pl.BlockSpec((tm,), imap, memory_space=pltpu.SMEM)
    return pl.pallas_call(
        functools.partial(_combine_kernel, tm=tm),
        out_shape=jax.ShapeDtypeStruct((T, D), F32),
        grid=(T // tm,),
        in_specs=[smem(lambda i: (i,)), smem(lambda i: (i,)),
                  smem(lambda i: (jnp.minimum(i + 1, last),)), smem(lambda i: (jnp.minimum(i + 1, last),)),
                  pl.BlockSpec((tm, 8), lambda i: (i, 0)),
                  pl.BlockSpec((tm, D), lambda i: (i, 0)),
                  pl.BlockSpec((1, D), lambda i: (0, 0)),
                  pl.BlockSpec((1, D), lambda i: (0, 0)),
                  pl.BlockSpec(memory_space=pl.ANY)],
        out_specs=pl.BlockSpec((tm, D), lambda i: (i, 0)),
        scratch_shapes=[pltpu.VMEM((2, 2, tm, D), F32), pltpu.SemaphoreType.DMA((2,))],
        compiler_params=pltpu.CompilerParams(dimension_semantics=("arbitrary",), vmem_limit_bytes=VMEM_LIMIT),
        name="moe_combine",
    )(pos1, pos2, pos1, pos2, rw, h, ln_g[None, :], ln_b[None, :], y_sorted)


def _moe_norm(h, route_t, rw, w_gate, w_up, w_down, ln_g, ln_b):
    pos, tiles = _moe_plan(route_t)
    xs = _moe_dispatch(h, pos[0], pos[1])
    n_tiles = xs.shape[0] // MOE_TM
    ys = _moe_ffn(xs, tiles[0, :n_tiles], tiles[1, :n_tiles], w_gate, w_up, w_down)
    return _moe_combine(h, ys, pos[0], pos[1], rw, ln_g, ln_b)


def kernel(x, w_in, w_pool, pool_scale, w_out, rel_bias, ln1_g, ln1_b, w_r1, b_r1, w_r2, b_r2,
           w_gate, w_up, w_down, ln2_g, ln2_b):
    B, S, D = x.shape
    assert w_in.shape[0] == DEPTH == 1 and S % CHUNK == 0
    bias = _bias_tiles(rel_bias)
    q2t, vt, iqt, iwt, k, ik, u = _in_proj(x, w_in[0])
    attn = _dsa_attention(q2t, iqt, iwt, k, vt, ik, bias)
    h, route_t, rw = _mix_norm_route(attn, u, x, w_pool[0], pool_scale[0], w_out[0], ln1_g[0], ln1_b[0],
                                     w_r1[0], b_r1[0], w_r2[0], b_r2[0])
    out = _moe_norm(h.reshape(B * S, D), route_t, rw.reshape(B * S, 8),
                    w_gate[0], w_up[0], w_down[0], ln2_g[0], ln2_b[0])
    return out.reshape(B, S, D)
```

```python
import functools
import math

import numpy as np
import jax
import jax.numpy as jnp
from jax import lax
from jax.experimental import pallas as pl
from jax.experimental.pallas import tpu as pltpu

F32 = jnp.float32
BF16 = jnp.bfloat16
I32 = jnp.int32

ATTN_HEADS = 8
HEAD_DIM = 64
ATTN_WIDTH = ATTN_HEADS * HEAD_DIM
IDX_HEADS = 8
IDX_DIM = 64
TOPK_MAX = 256
POOL_WINDOWS = (2, 4, 8, 16)
POOL_GROUP_DIM = 128
POOL_WIDTH = len(POOL_WINDOWS) * POOL_GROUP_DIM
POOL_HALO = 16
REL_BUCKETS = 32
REL_MAX_DIST = 128
N_GROUPS = 4
EXPERTS_PER_GROUP = 8
N_EXPERTS = N_GROUPS * EXPERTS_PER_GROUP
LN_EPS = 1e-5
DEPTH = 1
DEEPNORM_ALPHA = (2 * DEPTH) ** 0.25
LOG2E = math.log2(math.e)

LANES = 128
CHUNK = 256
INT_MIN = -2 ** 31
NEG_BIG = -1e30
VMEM_LIMIT = 56 * 1024 * 1024


def _rel_bucket_table(n):
    max_exact = REL_BUCKETS // 2
    d = np.arange(n)
    nf = np.maximum(d, 1).astype(np.float32)
    ratio = np.log(nf / np.float32(max_exact)) / np.float32(math.log(REL_MAX_DIST / max_exact))
    large = max_exact + (ratio * np.float32(REL_BUCKETS - max_exact)).astype(np.int32)
    large = np.minimum(large, REL_BUCKETS - 1)
    return np.where(d < max_exact, d, large).astype(np.int32)


def _near_bucket_tiles():
    tbl = _rel_bucket_table(2 * CHUNK)
    a = np.arange(CHUNK)[:, None]
    b = np.arange(CHUNK)[None, :]
    tiles = [tbl[np.maximum(delta * CHUNK + b - a, 0)] for delta in (0, 1)]
    return np.stack(tiles).astype(np.int32)


FAR_BUCKET = REL_BUCKETS - 1
assert int(_rel_bucket_table(2 * CHUNK)[CHUNK + 1:].min()) == FAR_BUCKET


def _bias_kernel(rb_ref, bucket_ref, o_ref):
    h = pl.program_id(1)
    bk = bucket_ref[0]
    far = rb_ref[FAR_BUCKET, h]
    acc = jnp.zeros(bk.shape, F32)
    for n in range(REL_BUCKETS):
        acc = jnp.where(bk == n, rb_ref[n, h] - far, acc)
    o_ref[0, 0] = acc * LOG2E


def _bias_tiles(rel_bias):
    buckets = jnp.asarray(_near_bucket_tiles())
    return pl.pallas_call(
        _bias_kernel,
        out_shape=jax.ShapeDtypeStruct((2, ATTN_HEADS, CHUNK, CHUNK), F32),
        grid=(2, ATTN_HEADS),
        in_specs=[pl.BlockSpec(memory_space=pltpu.SMEM),
                  pl.BlockSpec((1, CHUNK, CHUNK), lambda d, h: (d, 0, 0))],
        out_specs=pl.BlockSpec((1, 1, CHUNK, CHUNK), lambda d, h: (d, h, 0, 0)),
        name="bias_tiles",
    )(rel_bias, buckets)


Q2_ROWS = 2 * ATTN_WIDTH
V_SLOT = HEAD_DIM + 16
VT_ROWS = ATTN_HEADS * V_SLOT
WT_Q2, WT_V, WT_IQ, WT_IW = 0, Q2_ROWS, Q2_ROWS + VT_ROWS, Q2_ROWS + VT_ROWS + ATTN_WIDTH
WT_ROWS = WT_IW + 16
WN_K, WN_U, WN_IK = 0, ATTN_WIDTH, ATTN_WIDTH + POOL_WIDTH
WN_COLS = WN_IK + LANES


def _proj_kernel(x_ref, wt_ref, wn_ref, q2t_ref, vt_ref, iqt_ref, iwt_ref, k_ref, ik_ref, u_ref, *, tm):
    xb = x_ref[0].astype(BF16)
    t = lax.dot_general(wt_ref[...], xb, (((1,), (1,)), ((), ())),
                        preferred_element_type=F32)
    q2t_ref[0] = (t[WT_Q2:WT_V] * (HEAD_DIM ** -0.5 * LOG2E)).astype(BF16)
    iqt_ref[0] = (t[WT_IQ:WT_IW] * (IDX_DIM ** -0.5)).astype(BF16)
    iwt_ref[0] = t[WT_IW:WT_IW + IDX_HEADS] * (IDX_HEADS ** -0.5)
    slot_row = lax.broadcasted_iota(I32, (VT_ROWS, tm), 0) % V_SLOT
    vt = jnp.where(slot_row >= HEAD_DIM, 1.0, t[WT_V:WT_IQ]).astype(BF16)
    for j in range(tm // CHUNK):
        vt_ref[0, j] = vt[:, j * CHUNK:(j + 1) * CHUNK]
    n = jnp.dot(xb, wn_ref[...], preferred_element_type=F32)
    k_ref[0] = n[:, WN_K:WN_U].astype(BF16)
    u_ref[0] = n[:, WN_U:WN_IK]
    ik_ref[0] = n[:, WN_IK:WN_IK + IDX_DIM].astype(BF16)


def _in_proj(x, w_in, *, tm=512):
    B, S, D = x.shape
    q_off, k_off, v_off = 0, ATTN_WIDTH, 2 * ATTN_WIDTH
    iq_off = 3 * ATTN_WIDTH
    ik_off = iq_off + IDX_HEADS * IDX_DIM
    iw_off = ik_off + IDX_DIM
    p_off = iw_off + IDX_HEADS
    wq = w_in[:, q_off:k_off].T.reshape(ATTN_HEADS // 2, 2, HEAD_DIM, D)
    zq = jnp.zeros((ATTN_HEADS // 2, HEAD_DIM, D), w_in.dtype)
    wq2 = jnp.stack([wq[:, 0], zq, zq, wq[:, 1]], axis=1).reshape(Q2_ROWS, D)
    wv = w_in[:, v_off:iq_off].T.reshape(ATTN_HEADS, HEAD_DIM, D)
    wv = jnp.pad(wv, ((0, 0), (0, V_SLOT - HEAD_DIM), (0, 0))).reshape(VT_ROWS, D)
    wt = jnp.concatenate([
        wq2, wv, w_in[:, iq_off:ik_off].T, w_in[:, iw_off:p_off].T,
        jnp.zeros((WT_ROWS - WT_IW - IDX_HEADS, D), w_in.dtype)], axis=0).astype(BF16)
    wn = jnp.concatenate([
        w_in[:, k_off:v_off], w_in[:, p_off:], w_in[:, ik_off:iw_off],
        jnp.zeros((D, WN_COLS - WN_IK - IDX_DIM), w_in.dtype)], axis=1).astype(BF16)
    nt = S // tm
    cpt = tm // CHUNK
    outs = pl.pallas_call(
        functools.partial(_proj_kernel, tm=tm),
        out_shape=(
            jax.ShapeDtypeStruct((B, Q2_ROWS, S), BF16),
            jax.ShapeDtypeStruct((B, S // CHUNK, VT_ROWS, CHUNK), BF16),
            jax.ShapeDtypeStruct((B, IDX_HEADS * IDX_DIM, S), BF16),
            jax.ShapeDtypeStruct((B, IDX_HEADS, S), F32),
            jax.ShapeDtypeStruct((B, S, ATTN_WIDTH), BF16),
            jax.ShapeDtypeStruct((B, S, IDX_DIM), BF16),
            jax.ShapeDtypeStruct((B, S, POOL_WIDTH), F32),
        ),
        grid=(B, nt),
        in_specs=[pl.BlockSpec((1, tm, D), lambda b, i: (b, i, 0)),
                  pl.BlockSpec((WT_ROWS, D), lambda b, i: (0, 0)),
                  pl.BlockSpec((D, WN_COLS), lambda b, i: (0, 0))],
        out_specs=(
            pl.BlockSpec((1, Q2_ROWS, tm), lambda b, i: (b, 0, i)),
            pl.BlockSpec((1, cpt, VT_ROWS, CHUNK), lambda b, i: (b, i, 0, 0)),
            pl.BlockSpec((1, IDX_HEADS * IDX_DIM, tm), lambda b, i: (b, 0, i)),
            pl.BlockSpec((1, IDX_HEADS, tm), lambda b, i: (b, 0, i)),
            pl.BlockSpec((1, tm, ATTN_WIDTH), lambda b, i: (b, i, 0)),
            pl.BlockSpec((1, tm, IDX_DIM), lambda b, i: (b, i, 0)),
            pl.BlockSpec((1, tm, POOL_WIDTH), lambda b, i: (b, i, 0)),
        ),
        compiler_params=pltpu.CompilerParams(
            dimension_semantics=("arbitrary", "arbitrary"), vmem_limit_bytes=VMEM_LIMIT),
        name="in_proj",
    )(x, wt, wn)
    return outs


SNAP_FIRST = 14
SNAP_EVERY = 5
MAX_STEPS = SNAP_FIRST + 5 * 17
ROW_LO, ROW_HI, ROW_CNT, ROW_PROBE = 0, 1, 2, 3


def _order_key(bits):
    return bits ^ ((bits >> 31) & 0x7FFFFFFF)


def _attn_kernel(q2t_ref, iqt_ref, iwt_ref, k_ref, vt_ref, ik_ref, bias_ref, o_ref,
                 sc_ref, st_ref, neg_ref, lg_ref, p_ref, m_ref, al_ref, acc_ref,
                 *, topk):
    i = pl.program_id(1)
    nch = i + 1
    C = CHUNK
    row = lax.broadcasted_iota(I32, (C, C), 0)
    col = lax.broadcasted_iota(I32, (C, C), 1)

    def fold_rows(op, x):
        return op(x.reshape(C // 8, 8, C), axis=0)

    def score_body(c, carry):
        smin, smax = carry
        ikc = ik_ref[0, c]
        s = jnp.zeros((C, C), F32)
        for j in range(IDX_HEADS):
            d = jnp.dot(ikc, iqt_ref[0, j * IDX_DIM:(j + 1) * IDX_DIM, :],
                        preferred_element_type=F32)
            s = s + iwt_ref[0, j:j + 1, :] * jnp.maximum(d, 0.0)
        causal = (c * C + row) <= (i * C + col)
        sc_ref[c] = jnp.where(causal, s, -jnp.inf)
        smin = jnp.minimum(smin, fold_rows(jnp.min, jnp.where(causal, s, jnp.inf)))
        smax = jnp.maximum(smax, fold_rows(jnp.max, jnp.where(causal, s, -jnp.inf)))
        return smin, smax

    smin, smax = lax.fori_loop(0, nch, score_body,
                               (jnp.full((8, C), jnp.inf, F32), jnp.full((8, C), -jnp.inf, F32)))
    smin = jnp.min(smin, axis=0, keepdims=True)
    smax = jnp.max(smax, axis=0, keepdims=True)

    pos = i * C + lax.broadcasted_iota(I32, (1, C), 1)
    n_keys = (pos + 1).astype(F32)
    k_eff = jnp.minimum(pos + 1, topk).astype(F32)

    @pl.when(nch % 2 == 1)
    def _():
        sc_ref[nch] = jnp.full((C, C), -jnp.inf, F32)

    def count_ge(mid):
        def body(cp, acc):
            for c in (2 * cp, 2 * cp + 1):
                acc = acc + fold_rows(jnp.sum, jnp.where(sc_ref[c] >= mid, 1.0, 0.0))
            return acc
        part = lax.fori_loop(0, lax.shift_right_logical(nch + 1, 1), body, jnp.zeros((8, C), F32))
        return jnp.sum(part, axis=0, keepdims=True)

    def load_state():
        return (st_ref[ROW_LO:ROW_LO + 1, :], st_ref[ROW_HI:ROW_HI + 1, :],
                st_ref[ROW_CNT:ROW_CNT + 1, :], st_ref[ROW_PROBE:ROW_PROBE + 1, :])

    def store_state(lo, hi, cnt_lo, probe):
        st_ref[ROW_LO:ROW_LO + 1, :] = lo
        st_ref[ROW_HI:ROW_HI + 1, :] = hi
        st_ref[ROW_CNT:ROW_CNT + 1, :] = cnt_lo
        st_ref[ROW_PROBE:ROW_PROBE + 1, :] = probe

    def open_cols(lo, hi, cnt_lo):
        return jnp.where(cnt_lo > k_eff, jnp.where(hi > lo, 1.0, 0.0), 0.0)

    def bisect(it, lo, hi, cnt_lo, probe):
        klo = _order_key(pltpu.bitcast(lo, I32))
        khi = _order_key(pltpu.bitcast(hi, I32))
        i_mid = pltpu.bitcast(_order_key((klo >> 1) + (khi >> 1) + (klo & khi & 1)), F32)
        mid = jnp.where(jnp.logical_and(it > SNAP_FIRST, it % 2 == 1), i_mid, 0.5 * lo + 0.5 * hi)
        probing = probe > lo
        mid = jnp.where(probing, probe, mid)
        ok = open_cols(lo, hi, cnt_lo) * jnp.where(mid > lo, jnp.where(mid < hi, 1.0, 0.0), 0.0)
        cnt = count_ge(mid)
        up = ok * jnp.where(cnt >= k_eff, 1.0, 0.0)
        dn = ok - up
        new_hi = jnp.where(dn + up * jnp.where(probing, 1.0, 0.0) > 0.0, mid, hi)
        return (jnp.where(up > 0.0, mid, lo), new_hi, jnp.where(up > 0.0, cnt, cnt_lo),
                jnp.full((1, C), -jnp.inf, F32))

    def snap(it, lo, hi, cnt_lo, probe):
        def body(c, carry):
            vmin, vmax = carry
            sc = sc_ref[c]
            vmin = jnp.minimum(vmin, fold_rows(jnp.min, jnp.where(sc >= lo, sc, jnp.inf)))
            vmax = jnp.maximum(vmax, fold_rows(jnp.max, jnp.where(sc < hi, sc, -jnp.inf)))
            return vmin, vmax
        vmin, vmax = lax.fori_loop(0, nch, body, (jnp.full((8, C), jnp.inf, F32),
                                                  jnp.full((8, C), -jnp.inf, F32)))
        vmin = jnp.min(vmin, axis=0, keepdims=True)
        vmax = jnp.max(vmax, axis=0, keepdims=True)
        is_open = open_cols(lo, hi, cnt_lo) > 0.0
        single = vmin >= vmax
        return (jnp.where(is_open, vmin, lo),
                jnp.where(is_open, jnp.where(single, vmin, hi), hi),
                cnt_lo,
                jnp.where(is_open, jnp.where(single, -jnp.inf, vmax), -jnp.inf))

    def search_cond(st):
        it, n_open = st
        return jnp.logical_and(it < MAX_STEPS, n_open > 0.0)

    def search_body(st):
        it, _ = st
        do_snap = jnp.logical_and(it >= SNAP_FIRST, (it - SNAP_FIRST) % SNAP_EVERY == 0)

        @pl.when(do_snap)
        def _():
            store_state(*snap(it, *load_state()))

        @pl.when(jnp.logical_not(do_snap))
        def _():
            store_state(*bisect(it, *load_state()))

        lo, hi, cnt_lo, _ = load_state()
        return it + 1, jnp.max(open_cols(lo, hi, cnt_lo))

    kmax = _order_key(pltpu.bitcast(smax, I32))
    hi0 = pltpu.bitcast(_order_key(jnp.where(kmax == 2 ** 31 - 1, kmax, kmax + 1)), F32)
    store_state(smin, hi0, n_keys, jnp.full((1, C), -jnp.inf, F32))
    lax.while_loop(search_cond, search_body, (jnp.int32(0), jnp.max(open_cols(smin, hi0, n_keys))))
    th, _, cnt_th, _ = load_state()

    surplus = cnt_th - k_eff

    @pl.when(jnp.max(surplus) > 0.0)
    def _():
        later = jnp.where(col > row, 1.0, 0.0).astype(BF16)

        def drop_body(r, after):
            c = nch - 1 - r
            sc = sc_ref[c]
            tied = jnp.where(sc == th, 1.0, 0.0)
            follow = jnp.dot(later, tied.astype(BF16), preferred_element_type=F32) + after
            sc_ref[c] = jnp.where(tied * jnp.where(follow < surplus, 1.0, 0.0) > 0.0, -jnp.inf, sc)
            return after + jnp.sum(fold_rows(jnp.sum, tied), axis=0, keepdims=True)

        lax.fori_loop(0, nch, drop_body, jnp.zeros((1, C), F32))

    m_ref[...] = jnp.full(m_ref.shape, NEG_BIG, F32)
    acc_ref[...] = jnp.zeros(acc_ref.shape, F32)

    def attend(c, near):
        neg = jnp.where(sc_ref[c] >= th, 0.0, -jnp.inf)
        if near == 0:
            neg = jnp.where(row <= col, neg, -jnp.inf)
        neg_ref[...] = neg
        for h in range(ATTN_HEADS):
            p2 = (h // 2) * 2 * HEAD_DIM
            lg = jnp.dot(k_ref[0, c, :, p2:p2 + 2 * HEAD_DIM],
                         q2t_ref[0, 2 * HEAD_DIM * h:2 * HEAD_DIM * (h + 1), :],
                         preferred_element_type=F32)
            if near is not None:
                lg = lg + bias_ref[near, h]
            lg = lg + neg_ref[...]
            lg_ref[h] = lg
            m_old = m_ref[h:h + 1, :]
            m_new = jnp.maximum(m_old, jnp.max(lg, axis=0, keepdims=True))
            al_ref[h:h + 1, :] = jnp.exp2(m_old - m_new)
            m_ref[h:h + 1, :] = m_new
        for h in range(ATTN_HEADS):
            p_ref[h] = jnp.exp2(lg_ref[h] - m_ref[h:h + 1, :]).astype(BF16)
        for h in range(ATTN_HEADS):
            hs = slice(h * V_SLOT, (h + 1) * V_SLOT)
            pv = jnp.dot(vt_ref[0, c, hs, :], p_ref[h], preferred_element_type=F32)
            acc_ref[hs, :] = al_ref[h:h + 1, :] * acc_ref[hs, :] + pv

    def far_body(c, carry):
        attend(c, None)
        return carry

    lax.fori_loop(0, jnp.maximum(i - 1, 0), far_body, 0)

    @pl.when(i >= 1)
    def _():
        attend(i - 1, 1)

    attend(i, 0)

    out_t = jnp.concatenate(
        [acc_ref[h * V_SLOT:h * V_SLOT + HEAD_DIM, :] / acc_ref[h * V_SLOT + HEAD_DIM:h * V_SLOT + HEAD_DIM + 1, :]
         for h in range(ATTN_HEADS)], axis=0)
    o_ref[0] = out_t.T.astype(o_ref.dtype)


def _dsa_attention(q2t, iqt, iwt, k, vt, ik, bias):
    B, S, _ = k.shape
    nq = S // CHUNK
    assert nq % 2 == 0
    topk = min(TOPK_MAX, S // 4)
    k4 = k.reshape(B, nq, CHUNK, ATTN_WIDTH)
    ik4 = ik.reshape(B, nq, CHUNK, IDX_DIM)
    return pl.pallas_call(
        functools.partial(_attn_kernel, topk=topk),
        out_shape=jax.ShapeDtypeStruct((B, S, ATTN_WIDTH), BF16),
        grid=(B, nq),
        in_specs=[
            pl.BlockSpec((1, Q2_ROWS, CHUNK), lambda b, i: (b, 0, i)),
            pl.BlockSpec((1, IDX_HEADS * IDX_DIM, CHUNK), lambda b, i: (b, 0, i)),
            pl.BlockSpec((1, IDX_HEADS, CHUNK), lambda b, i: (b, 0, i)),
            pl.BlockSpec((1, nq, CHUNK, ATTN_WIDTH), lambda b, i: (b, 0, 0, 0)),
            pl.BlockSpec((1, nq, VT_ROWS, CHUNK), lambda b, i: (b, 0, 0, 0)),
            pl.BlockSpec((1, nq, CHUNK, IDX_DIM), lambda b, i: (b, 0, 0, 0)),
            pl.BlockSpec((2, ATTN_HEADS, CHUNK, CHUNK), lambda b, i: (0, 0, 0, 0)),
        ],
        out_specs=pl.BlockSpec((1, CHUNK, ATTN_WIDTH), lambda b, i: (b, i, 0)),
        scratch_shapes=[
            pltpu.VMEM((nq, CHUNK, CHUNK), F32),
            pltpu.VMEM((8, CHUNK), F32),
            pltpu.VMEM((CHUNK, CHUNK), F32),
            pltpu.VMEM((ATTN_HEADS, CHUNK, CHUNK), F32),
            pltpu.VMEM((ATTN_HEADS, CHUNK, CHUNK), BF16),
            pltpu.VMEM((ATTN_HEADS, CHUNK), F32),
            pltpu.VMEM((ATTN_HEADS, CHUNK), F32),
            pltpu.VMEM((VT_ROWS, CHUNK), F32),
        ],
        compiler_params=pltpu.CompilerParams(
            dimension_semantics=("arbitrary", "arbitrary"), vmem_limit_bytes=VMEM_LIMIT),
        name="dsa_attn",
    )(q2t, iqt, iwt, k4, vt, ik4, bias)


ROUTE_LANES = LANES


def _layer_norm(y, g, b):
    mu = jnp.mean(y, axis=-1, keepdims=True)
    yc = y - mu
    var = jnp.mean(yc * yc, axis=-1, keepdims=True)
    return yc * lax.rsqrt(var + LN_EPS) * g + b


def _mix_kernel(attn_ref, u_ref, halo_ref, x_ref, wpool_ref, pscale_ref, wout_ref, g_ref, b_ref,
                wr_ref, br_ref, h_ref, route_ref, rw_ref, *, tm):
    i = pl.program_id(1)
    halo = jnp.where(i > 0, halo_ref[0], 0.0)
    ue = jnp.concatenate([halo, u_ref[0]], axis=0)
    pos = i * tm + lax.broadcasted_iota(I32, (tm, 1), 0)
    mixed = []
    for g, w in enumerate(POOL_WINDOWS):
        gs = slice(g * POOL_GROUP_DIM, (g + 1) * POOL_GROUP_DIM)
        ch = ue[:, gs]
        win = ch
        step = 1
        while step < w:
            win = win + pltpu.roll(win, step, axis=0)
            step *= 2
        cnt = jnp.minimum(pos + 1, w).astype(F32)
        pooled = win[POOL_HALO:] / cnt - ch[POOL_HALO:]
        mg = jnp.dot(pooled.astype(BF16), wpool_ref[g], preferred_element_type=F32)
        mixed.append((mg * pscale_ref[:, gs]).astype(BF16))
    cat = jnp.concatenate([attn_ref[0]] + mixed, axis=-1)
    mix = jnp.dot(cat, wout_ref[...], preferred_element_type=F32)
    h = _layer_norm(DEEPNORM_ALPHA * x_ref[0] + mix, g_ref[...], b_ref[...])
    h_ref[0] = h
    hb = h.astype(BF16)

    lg = jnp.dot(hb, wr_ref[...], preferred_element_type=F32) + br_ref[...]
    lane = lax.broadcasted_iota(I32, (tm, ROUTE_LANES), 1)
    gl = jnp.where(lane >= N_EXPERTS, jnp.where(lane < N_EXPERTS + N_GROUPS, lg, -jnp.inf), -jnp.inf)
    ge = jnp.exp(gl - jnp.max(gl, axis=-1, keepdims=True))
    pg = ge / jnp.sum(ge, axis=-1, keepdims=True)
    pg_top = jnp.max(pg, axis=-1, keepdims=True)
    g_lane = jnp.min(jnp.where(pg == pg_top, lane, ROUTE_LANES), axis=-1, keepdims=True)
    e_lo = (g_lane - N_EXPERTS) * EXPERTS_PER_GROUP
    fl = jnp.where(lane >= e_lo, jnp.where(lane < e_lo + EXPERTS_PER_GROUP, lg, -jnp.inf), -jnp.inf)
    fe = jnp.exp(fl - jnp.max(fl, axis=-1, keepdims=True))
    pf = fe / jnp.sum(fe, axis=-1, keepdims=True)
    p1 = jnp.max(pf, axis=-1, keepdims=True)
    i1 = jnp.min(jnp.where(pf == p1, lane, ROUTE_LANES), axis=-1, keepdims=True)
    pr = jnp.where(lane == i1, -1.0, jnp.where(fl == -jnp.inf, -1.0, pf))
    p2 = jnp.max(pr, axis=-1, keepdims=True)
    i2 = jnp.min(jnp.where(pr == p2, lane, ROUTE_LANES), axis=-1, keepdims=True)
    psum = p1 + p2
    w1 = pg_top * p1 / psum
    w2 = pg_top * p2 / psum
    ids = jnp.where(lane == 0, i1, jnp.where(lane == 1, i2, 0)).astype(F32)
    route_ref[...] = ids.T[0:8, :].astype(I32)
    lane8 = lax.broadcasted_iota(I32, (tm, 8), 1)
    rw_ref[0] = jnp.where(lane8 == 0, w1, jnp.where(lane8 == 1, w2, 0.0))


def _mix_norm_route(attn, u, x, w_pool, pool_scale, w_out, ln_g, ln_b, w_r1, b_r1, w_r2, b_r2, *, tm=256):
    B, S, D = x.shape
    wr = jnp.concatenate([w_r2, w_r1, jnp.zeros((D, ROUTE_LANES - N_EXPERTS - N_GROUPS), w_r1.dtype)],
                         axis=1).astype(BF16)
    br = jnp.concatenate([b_r2, b_r1, jnp.zeros((ROUTE_LANES - N_EXPERTS - N_GROUPS,), b_r1.dtype)])[None, :]
    hpt = tm // POOL_HALO
    return pl.pallas_call(
        functools.partial(_mix_kernel, tm=tm),
        out_shape=(jax.ShapeDtypeStruct((B, S, D), F32),
                   jax.ShapeDtypeStruct((8, B * S), I32),
                   jax.ShapeDtypeStruct((B, S, 8), F32)),
        grid=(B, S // tm),
        in_specs=[
            pl.BlockSpec((1, tm, ATTN_WIDTH), lambda b, i: (b, i, 0)),
            pl.BlockSpec((1, tm, POOL_WIDTH), lambda b, i: (b, i, 0)),
            pl.BlockSpec((1, POOL_HALO, POOL_WIDTH), lambda b, i: (b, jnp.maximum(i * hpt - 1, 0), 0)),
            pl.BlockSpec((1, tm, D), lambda b, i: (b, i, 0)),
            pl.BlockSpec((len(POOL_WINDOWS), POOL_GROUP_DIM, POOL_GROUP_DIM), lambda b, i: (0, 0, 0)),
            pl.BlockSpec((1, POOL_WIDTH), lambda b, i: (0, 0)),
            pl.BlockSpec((D, D), lambda b, i: (0, 0)),
            pl.BlockSpec((1, D), lambda b, i: (0, 0)),
            pl.BlockSpec((1, D), lambda b, i: (0, 0)),
            pl.BlockSpec((D, ROUTE_LANES), lambda b, i: (0, 0)),
            pl.BlockSpec((1, ROUTE_LANES), lambda b, i: (0, 0)),
        ],
        out_specs=(pl.BlockSpec((1, tm, D), lambda b, i: (b, i, 0)),
                   pl.BlockSpec((8, tm), lambda b, i: (0, b * (S // tm) + i)),
                   pl.BlockSpec((1, tm, 8), lambda b, i: (b, i, 0))),
        compiler_params=pltpu.CompilerParams(
            dimension_semantics=("arbitrary", "arbitrary"), vmem_limit_bytes=VMEM_LIMIT),
        name="mix_norm",
    )(attn, u, u, x, w_pool.astype(BF16), pool_scale[None, :], w_out.astype(BF16),
      ln_g[None, :], ln_b[None, :], wr, br)


MOE_TM = 256
PLAN_TB = 1024
GATHER_UNROLL = 8


def _moe_rows(n_tokens):
    return 2 * n_tokens + N_EXPERTS * MOE_TM


def _plan_kernel(route_ref, pos_ref, tile_ref, cnt_ref, off_ref, carry_ref, *, n_tile_lanes):
    ph = pl.program_id(0)
    b = pl.program_id(1)
    tb = route_ref.shape[1]
    esub = lax.broadcasted_iota(I32, (N_EXPERTS, tb), 0)
    a1 = jnp.where(esub == route_ref[0:1, :], 1.0, 0.0)
    a2 = jnp.where(esub == route_ref[1:2, :], 1.0, 0.0)
    a = a1 + a2
    n_here = jnp.sum(a, axis=1, keepdims=True)

    @pl.when(jnp.logical_and(ph == 0, b == 0))
    def _():
        cnt_ref[...] = jnp.zeros(cnt_ref.shape, F32)

    @pl.when(ph == 0)
    def _():
        cnt_ref[...] += n_here

    @pl.when(jnp.logical_and(ph == 1, b == 0))
    def _():
        n_tile = jnp.floor((cnt_ref[...] + (MOE_TM - 1)) * (1.0 / MOE_TM))
        er = lax.broadcasted_iota(I32, (N_EXPERTS, N_EXPERTS), 0)
        ec = lax.broadcasted_iota(I32, (N_EXPERTS, N_EXPERTS), 1)
        before = jnp.where(ec < er, 1.0, 0.0).astype(BF16)
        t_off = jnp.dot(before, n_tile.astype(BF16), preferred_element_type=F32)
        off_ref[...] = t_off * MOE_TM
        carry_ref[...] = jnp.zeros(carry_ref.shape, F32)
        j = lax.broadcasted_iota(I32, (N_EXPERTS, n_tile_lanes), 1).astype(F32)
        owner = jnp.sum(jnp.where(t_off[:, 0:1] <= j, 1.0, 0.0), axis=0, keepdims=True) - 1.0
        total = jnp.sum(n_tile[:, 0:1], axis=0, keepdims=True)
        used = jnp.where(j[0:1, :] < total, 1.0, 0.0)
        row8 = lax.broadcasted_iota(I32, (8, n_tile_lanes), 0)
        tile_ref[...] = jnp.where(row8 == 0, owner, jnp.where(row8 == 1, used, 0.0)).astype(I32)

    @pl.when(ph == 1)
    def _():
        tr = lax.broadcasted_iota(I32, (tb, tb), 0)
        tc = lax.broadcasted_iota(I32, (tb, tb), 1)
        earlier = jnp.where(tr < tc, 1.0, 0.0).astype(BF16)
        seen = jnp.dot(a.astype(BF16), earlier, preferred_element_type=F32)
        dest = seen + carry_ref[:, 0:1] + off_ref[:, 0:1]
        p1 = jnp.sum(a1 * dest, axis=0, keepdims=True)
        p2 = jnp.sum(a2 * dest, axis=0, keepdims=True)
        row8 = lax.broadcasted_iota(I32, (8, tb), 0)
        pos_ref[...] = jnp.where(row8 == 0, p1, jnp.where(row8 == 1, p2, 0.0)).astype(I32)
        carry_ref[...] += n_here


def _moe_plan(route_t):
    _, T = route_t.shape
    n_tiles = _moe_rows(T) // MOE_TM
    n_tile_lanes = -(-n_tiles // LANES) * LANES
    nb = T // PLAN_TB
    return pl.pallas_call(
        functools.partial(_plan_kernel, n_tile_lanes=n_tile_lanes),
        out_shape=(jax.ShapeDtypeStruct((8, T), I32), jax.ShapeDtypeStruct((8, n_tile_lanes), I32)),
        grid=(2, nb),
        in_specs=[pl.BlockSpec((8, PLAN_TB), lambda ph, b: (0, b))],
        out_specs=(pl.BlockSpec((8, PLAN_TB), lambda ph, b: (0, b * ph)),
                   pl.BlockSpec((8, n_tile_lanes), lambda ph, b: (0, 0))),
        scratch_shapes=[pltpu.VMEM((N_EXPERTS, LANES), F32),
                        pltpu.VMEM((N_EXPERTS, LANES), F32),
                        pltpu.VMEM((N_EXPERTS, LANES), F32)],
        compiler_params=pltpu.CompilerParams(dimension_semantics=("arbitrary", "arbitrary")),
        name="moe_plan",
    )(route_t)


def _row_copy(src_ref, src_row, dst_ref, dst_row, sem):
    return pltpu.make_async_copy(src_ref.at[pl.ds(src_row, 1)], dst_ref.at[pl.ds(dst_row, 1)], sem)


def _dispatch_kernel(p1_ref, p2_ref, h_ref, init_hbm, out_hbm, sem, *, tb):
    del init_hbm

    def start(t, carry):
        _row_copy(h_ref, t, out_hbm, p1_ref[t], sem).start()
        _row_copy(h_ref, t, out_hbm, p2_ref[t], sem).start()
        return carry

    lax.fori_loop(0, tb, start, 0, unroll=GATHER_UNROLL)

    def wait(t, carry):
        _row_copy(h_ref, 0, out_hbm, 0, sem).wait()
        _row_copy(h_ref, 0, out_hbm, 0, sem).wait()
        return carry

    lax.fori_loop(0, tb, wait, 0, unroll=GATHER_UNROLL)


def _moe_dispatch(h, pos1, pos2, *, tb=512):
    T, D = h.shape
    rows = _moe_rows(T)
    return pl.pallas_call(
        functools.partial(_dispatch_kernel, tb=tb),
        out_shape=jax.ShapeDtypeStruct((rows, D), h.dtype),
        grid=(T // tb,),
        in_specs=[pl.BlockSpec((tb,), lambda i: (i,), memory_space=pltpu.SMEM),
                  pl.BlockSpec((tb,), lambda i: (i,), memory_space=pltpu.SMEM),
                  pl.BlockSpec((tb, D), lambda i: (i, 0)),
                  pl.BlockSpec(memory_space=pl.ANY)],
        out_specs=pl.BlockSpec(memory_space=pl.ANY),
        scratch_shapes=[pltpu.SemaphoreType.DMA(())],
        input_output_aliases={3: 0},
        compiler_params=pltpu.CompilerParams(dimension_semantics=("arbitrary",)),
        name="moe_dispatch",
    )(pos1, pos2, h, jnp.zeros((rows, D), h.dtype))


def _ffn_kernel(owner_ref, used_ref, x_ref, wg_ref, wu_ref, wd_ref, o_ref):
    j = pl.program_id(0)

    @pl.when(used_ref[j] == 1)
    def _():
        xb = x_ref[...].astype(BF16)
        gate = jnp.dot(xb, wg_ref[0].astype(BF16), preferred_element_type=F32)
        up = jnp.dot(xb, wu_ref[0].astype(BF16), preferred_element_type=F32)
        a = gate * jax.nn.sigmoid(gate) * up
        o_ref[...] = jnp.dot(a.astype(BF16), wd_ref[0].astype(BF16), preferred_element_type=F32)

    @pl.when(used_ref[j] == 0)
    def _():
        o_ref[...] = jnp.zeros(o_ref.shape, F32)


def _moe_ffn(xs, owner, used, w_gate, w_up, w_down):
    rows, D = xs.shape
    E, _, F = w_gate.shape
    return pl.pallas_call(
        _ffn_kernel,
        out_shape=jax.ShapeDtypeStruct((rows, D), F32),
        grid_spec=pltpu.PrefetchScalarGridSpec(
            num_scalar_prefetch=2,
            grid=(rows // MOE_TM,),
            in_specs=[pl.BlockSpec((MOE_TM, D), lambda j, ow, us: (j, 0)),
                      pl.BlockSpec((1, D, F), lambda j, ow, us: (ow[j], 0, 0)),
                      pl.BlockSpec((1, D, F), lambda j, ow, us: (ow[j], 0, 0)),
                      pl.BlockSpec((1, F, D), lambda j, ow, us: (ow[j], 0, 0))],
            out_specs=pl.BlockSpec((MOE_TM, D), lambda j, ow, us: (j, 0)),
        ),
        compiler_params=pltpu.CompilerParams(dimension_semantics=("arbitrary",), vmem_limit_bytes=VMEM_LIMIT),
        name="moe_ffn",
    )(owner, used, xs, w_gate, w_up, w_down)


def _combine_kernel(p1_ref, p2_ref, p1n_ref, p2n_ref, rw_ref, h_ref, g_ref, b_ref, y_hbm, o_ref,
                    buf_ref, sem, *, tm):
    s = pl.program_id(0)
    slot = s % 2

    def fetch(pa_ref, pb_ref, into):
        def start(t, carry):
            pltpu.make_async_copy(y_hbm.at[pl.ds(pa_ref[t], 1)], buf_ref.at[into, 0, pl.ds(t, 1)],
                                  sem.at[into]).start()
            pltpu.make_async_copy(y_hbm.at[pl.ds(pb_ref[t], 1)], buf_ref.at[into, 1, pl.ds(t, 1)],
                                  sem.at[into]).start()
            return carry
        lax.fori_loop(0, tm, start, 0, unroll=GATHER_UNROLL)

    @pl.when(s == 0)
    def _():
        fetch(p1_ref, p2_ref, 0)

    @pl.when(s + 1 < pl.num_programs(0))
    def _():
        fetch(p1n_ref, p2n_ref, 1 - slot)

    def wait(t, carry):
        for half in range(2):
            pltpu.make_async_copy(y_hbm.at[pl.ds(0, 1)], buf_ref.at[slot, half, pl.ds(0, 1)], sem.at[slot]).wait()
        return carry

    lax.fori_loop(0, tm, wait, 0, unroll=GATHER_UNROLL)
    y = rw_ref[:, 0:1] * buf_ref[slot, 0] + rw_ref[:, 1:2] * buf_ref[slot, 1]
    o_ref[...] = _layer_norm(DEEPNORM_ALPHA * h_ref[...] + y, g_ref[...], b_ref[...])


def _moe_combine(h, y_sorted, pos1, pos2, rw, ln_g, ln_b, *, tm=256):
    T, D = h.shape
    last = T // tm - 1
    smem = lambda imap: pl.BlockSpec((tm,), imap, memory_space=pltpu.SMEM)
    return pl.pallas_call(
        functools.partial(_combine_kernel, tm=tm),
        out_shape=jax.ShapeDtypeStruct((T, D), F32),
        grid=(T // tm,),
        in_specs=[smem(lambda i: (i,)), smem(lambda i: (i,)),
                  smem(lambda i: (jnp.minimum(i + 1, last),)), smem(lambda i: (jnp.minimum(i + 1, last),)),
                  pl.BlockSpec((tm, 8), lambda i: (i, 0)),
                  pl.BlockSpec((tm, D), lambda i: (i, 0)),
                  pl.BlockSpec((1, D), lambda i: (0, 0)),
                  pl.BlockSpec((1, D), lambda i: (0, 0)),
                  pl.BlockSpec(memory_space=pl.ANY)],
        out_specs=pl.BlockSpec((tm, D), lambda i: (i, 0)),
        scratch_shapes=[pltpu.VMEM((2, 2, tm, D), F32), pltpu.SemaphoreType.DMA((2,))],
        compiler_params=pltpu.CompilerParams(dimension_semantics=("arbitrary",), vmem_limit_bytes=VMEM_LIMIT),
        name="moe_combine",
    )(pos1, pos2, pos1, pos2, rw, h, ln_g[None, :], ln_b[None, :], y_sorted)


def _moe_norm(h, route_t, rw, w_gate, w_up, w_down, ln_g, ln_b):
    pos, tiles = _moe_plan(route_t)
    xs = _moe_dispatch(h, pos[0], pos[1])
    n_tiles = xs.shape[0] // MOE_TM
    ys = _moe_ffn(xs, tiles[0, :n_tiles], tiles[1, :n_tiles], w_gate, w_up, w_down)
    return _moe_combine(h, ys, pos[0], pos[1], rw, ln_g, ln_b)


def kernel(x, w_in, w_pool, pool_scale, w_out, rel_bias, ln1_g, ln1_b, w_r1, b_r1, w_r2, b_r2,
           w_gate, w_up, w_down, ln2_g, ln2_b):
    B, S, D = x.shape
    assert w_in.shape[0] == DEPTH == 1 and S % CHUNK == 0
    bias = _bias_tiles(rel_bias)
    q2t, vt, iqt, iwt, k, ik, u = _in_proj(x, w_in[0])
    attn = _dsa_attention(q2t, iqt, iwt, k, vt, ik, bias)
    h, route_t, rw = _mix_norm_route(attn, u, x, w_pool[0], pool_scale[0], w_out[0], ln1_g[0], ln1_b[0],
                                     w_r1[0], b_r1[0], w_r2[0], b_r2[0])
    out = _moe_norm(h.reshape(B * S, D), route_t, rw.reshape(B * S, 8),
                    w_gate[0], w_up[0], w_down[0], ln2_g[0], ln2_b[0])
    return out.reshape(B, S, D)
```

```python
import functools
import math

import numpy as np
import jax
import jax.numpy as jnp
from jax import lax
from jax.experimental import pallas as pl
from jax.experimental.pallas import tpu as pltpu

F32 = jnp.float32
BF16 = jnp.bfloat16
I32 = jnp.int32

ATTN_HEADS = 8
HEAD_DIM = 64
ATTN_WIDTH = ATTN_HEADS * HEAD_DIM
IDX_HEADS = 8
IDX_DIM = 64
TOPK_MAX = 256
POOL_WINDOWS = (2, 4, 8, 16)
POOL_GROUP_DIM = 128
POOL_WIDTH = len(POOL_WINDOWS) * POOL_GROUP_DIM
POOL_HALO = 16
REL_BUCKETS = 32
REL_MAX_DIST = 128
N_GROUPS = 4
EXPERTS_PER_GROUP = 8
N_EXPERTS = N_GROUPS * EXPERTS_PER_GROUP
LN_EPS = 1e-5
DEPTH = 1
DEEPNORM_ALPHA = (2 * DEPTH) ** 0.25
LOG2E = math.log2(math.e)

LANES = 128
CHUNK = 256
INT_MIN = -2 ** 31
NEG_BIG = -1e30
VMEM_LIMIT = 56 * 1024 * 1024


def _rel_bucket_table(n):
    max_exact = REL_BUCKETS // 2
    d = np.arange(n)
    nf = np.maximum(d, 1).astype(np.float32)
    ratio = np.log(nf / np.float32(max_exact)) / np.float32(math.log(REL_MAX_DIST / max_exact))
    large = max_exact + (ratio * np.float32(REL_BUCKETS - max_exact)).astype(np.int32)
    large = np.minimum(large, REL_BUCKETS - 1)
    return np.where(d < max_exact, d, large).astype(np.int32)


def _near_bucket_tiles():
    tbl = _rel_bucket_table(2 * CHUNK)
    a = np.arange(CHUNK)[:, None]
    b = np.arange(CHUNK)[None, :]
    tiles = [tbl[np.maximum(delta * CHUNK + b - a, 0)] for delta in (0, 1)]
    return np.stack(tiles).astype(np.int32)


FAR_BUCKET = REL_BUCKETS - 1
assert int(_rel_bucket_table(2 * CHUNK)[CHUNK + 1:].min()) == FAR_BUCKET


def _bias_kernel(rb_ref, bucket_ref, o_ref):
    h = pl.program_id(1)
    bk = bucket_ref[0]
    far = rb_ref[FAR_BUCKET, h]
    acc = jnp.zeros(bk.shape, F32)
    for n in range(REL_BUCKETS):
        acc = jnp.where(bk == n, rb_ref[n, h] - far, acc)
    o_ref[0, 0] = acc * LOG2E


def _bias_tiles(rel_bias):
    buckets = jnp.asarray(_near_bucket_tiles())
    return pl.pallas_call(
        _bias_kernel,
        out_shape=jax.ShapeDtypeStruct((2, ATTN_HEADS, CHUNK, CHUNK), F32),
        grid=(2, ATTN_HEADS),
        in_specs=[pl.BlockSpec(memory_space=pltpu.SMEM),
                  pl.BlockSpec((1, CHUNK, CHUNK), lambda d, h: (d, 0, 0))],
        out_specs=pl.BlockSpec((1, 1, CHUNK, CHUNK), lambda d, h: (d, h, 0, 0)),
        name="bias_tiles",
    )(rel_bias, buckets)


Q2_ROWS = 2 * ATTN_WIDTH
V_SLOT = HEAD_DIM + 16
VT_ROWS = ATTN_HEADS * V_SLOT
WT_Q2, WT_V, WT_IQ, WT_IW = 0, Q2_ROWS, Q2_ROWS + VT_ROWS, Q2_ROWS + VT_ROWS + ATTN_WIDTH
WT_ROWS = WT_IW + 16
WN_K, WN_U, WN_IK = 0, ATTN_WIDTH, ATTN_WIDTH + POOL_WIDTH
WN_COLS = WN_IK + LANES


def _proj_kernel(x_ref, wt_ref, wn_ref, q2t_ref, vt_ref, iqt_ref, iwt_ref, k_ref, ik_ref, u_ref, *, tm):
    xb = x_ref[0].astype(BF16)
    t = lax.dot_general(wt_ref[...], xb, (((1,), (1,)), ((), ())),
                        preferred_element_type=F32)
    q2t_ref[0] = (t[WT_Q2:WT_V] * (HEAD_DIM ** -0.5 * LOG2E)).astype(BF16)
    iqt_ref[0] = (t[WT_IQ:WT_IW] * (IDX_DIM ** -0.5)).astype(BF16)
    iwt_ref[0] = t[WT_IW:WT_IW + IDX_HEADS] * (IDX_HEADS ** -0.5)
    slot_row = lax.broadcasted_iota(I32, (VT_ROWS, tm), 0) % V_SLOT
    vt = jnp.where(slot_row >= HEAD_DIM, 1.0, t[WT_V:WT_IQ]).astype(BF16)
    for j in range(tm // CHUNK):
        vt_ref[0, j] = vt[:, j * CHUNK:(j + 1) * CHUNK]
    n = jnp.dot(xb, wn_ref[...], preferred_element_type=F32)
    k_ref[0] = n[:, WN_K:WN_U].astype(BF16)
    u_ref[0] = n[:, WN_U:WN_IK]
    ik_ref[0] = n[:, WN_IK:WN_IK + IDX_DIM].astype(BF16)


def _in_proj(x, w_in, *, tm=512):
    B, S, D = x.shape
    q_off, k_off, v_off = 0, ATTN_WIDTH, 2 * ATTN_WIDTH
    iq_off = 3 * ATTN_WIDTH
    ik_off = iq_off + IDX_HEADS * IDX_DIM
    iw_off = ik_off + IDX_DIM
    p_off = iw_off + IDX_HEADS
    wq = w_in[:, q_off:k_off].T.reshape(ATTN_HEADS // 2, 2, HEAD_DIM, D)
    zq = jnp.zeros((ATTN_HEADS // 2, HEAD_DIM, D), w_in.dtype)
    wq2 = jnp.stack([wq[:, 0], zq, zq, wq[:, 1]], axis=1).reshape(Q2_ROWS, D)
    wv = w_in[:, v_off:iq_off].T.reshape(ATTN_HEADS, HEAD_DIM, D)
    wv = jnp.pad(wv, ((0, 0), (0, V_SLOT - HEAD_DIM), (0, 0))).reshape(VT_ROWS, D)
    wt = jnp.concatenate([
        wq2, wv, w_in[:, iq_off:ik_off].T, w_in[:, iw_off:p_off].T,
        jnp.zeros((WT_ROWS - WT_IW - IDX_HEADS, D), w_in.dtype)], axis=0).astype(BF16)
    wn = jnp.concatenate([
        w_in[:, k_off:v_off], w_in[:, p_off:], w_in[:, ik_off:iw_off],
        jnp.zeros((D, WN_COLS - WN_IK - IDX_DIM), w_in.dtype)], axis=1).astype(BF16)
    nt = S // tm
    cpt = tm // CHUNK
    outs = pl.pallas_call(
        functools.partial(_proj_kernel, tm=tm),
        out_shape=(
            jax.ShapeDtypeStruct((B, Q2_ROWS, S), BF16),
            jax.ShapeDtypeStruct((B, S // CHUNK, VT_ROWS, CHUNK), BF16),
            jax.ShapeDtypeStruct((B, IDX_HEADS * IDX_DIM, S), BF16),
            jax.ShapeDtypeStruct((B, IDX_HEADS, S), F32),
            jax.ShapeDtypeStruct((B, S, ATTN_WIDTH), BF16),
            jax.ShapeDtypeStruct((B, S, IDX_DIM), BF16),
            jax.ShapeDtypeStruct((B, S, POOL_WIDTH), F32),
        ),
        grid=(B, nt),
        in_specs=[pl.BlockSpec((1, tm, D), lambda b, i: (b, i, 0)),
                  pl.BlockSpec((WT_ROWS, D), lambda b, i: (0, 0)),
                  pl.BlockSpec((D, WN_COLS), lambda b, i: (0, 0))],
        out_specs=(
            pl.BlockSpec((1, Q2_ROWS, tm), lambda b, i: (b, 0, i)),
            pl.BlockSpec((1, cpt, VT_ROWS, CHUNK), lambda b, i: (b, i, 0, 0)),
            pl.BlockSpec((1, IDX_HEADS * IDX_DIM, tm), lambda b, i: (b, 0, i)),
            pl.BlockSpec((1, IDX_HEADS, tm), lambda b, i: (b, 0, i)),
            pl.BlockSpec((1, tm, ATTN_WIDTH), lambda b, i: (b, i, 0)),
            pl.BlockSpec((1, tm, IDX_DIM), lambda b, i: (b, i, 0)),
            pl.BlockSpec((1, tm, POOL_WIDTH), lambda b, i: (b, i, 0)),
        ),
        compiler_params=pltpu.CompilerParams(
            dimension_semantics=("arbitrary", "arbitrary"), vmem_limit_bytes=VMEM_LIMIT),
        name="in_proj",
    )(x, wt, wn)
    return outs


SNAP_ROUND = 7
SNAP_PERIOD = 3
MAX_ROUNDS = SNAP_ROUND + SNAP_PERIOD * 17
ROW_LO, ROW_HI, ROW_CNT, ROW_PROBE = 0, 1, 2, 3


def _order_key(bits):
    return bits ^ ((bits >> 31) & 0x7FFFFFFF)


def _attn_kernel(q2t_ref, iqt_ref, iwt_ref, k_ref, vt_ref, ik_ref, bias_ref, o_ref,
                 sc_ref, st_ref, neg_ref, lg_ref, p_ref, m_ref, al_ref, acc_ref,
                 *, topk):
    i = pl.program_id(1)
    nch = i + 1
    C = CHUNK
    row = lax.broadcasted_iota(I32, (C, C), 0)
    col = lax.broadcasted_iota(I32, (C, C), 1)

    def fold_rows(op, x):
        return op(x.reshape(C // 8, 8, C), axis=0)

    def score_body(c, carry):
        smin, smax = carry
        ikc = ik_ref[0, c]
        s = jnp.zeros((C, C), F32)
        for j in range(IDX_HEADS):
            d = jnp.dot(ikc, iqt_ref[0, j * IDX_DIM:(j + 1) * IDX_DIM, :],
                        preferred_element_type=F32)
            s = s + iwt_ref[0, j:j + 1, :] * jnp.maximum(d, 0.0)
        causal = (c * C + row) <= (i * C + col)
        sc_ref[c] = jnp.where(causal, s, -jnp.inf)
        smin = jnp.minimum(smin, fold_rows(jnp.min, jnp.where(causal, s, jnp.inf)))
        smax = jnp.maximum(smax, fold_rows(jnp.max, jnp.where(causal, s, -jnp.inf)))
        return smin, smax

    smin, smax = lax.fori_loop(0, nch, score_body,
                               (jnp.full((8, C), jnp.inf, F32), jnp.full((8, C), -jnp.inf, F32)))
    smin = jnp.min(smin, axis=0, keepdims=True)
    smax = jnp.max(smax, axis=0, keepdims=True)

    pos = i * C + lax.broadcasted_iota(I32, (1, C), 1)
    n_keys = (pos + 1).astype(F32)
    k_eff = jnp.minimum(pos + 1, topk).astype(F32)

    @pl.when(nch % 2 == 1)
    def _():
        sc_ref[nch] = jnp.full((C, C), -jnp.inf, F32)

    def count_ge(mid):
        def body(cp, acc):
            for c in (2 * cp, 2 * cp + 1):
                acc = acc + fold_rows(jnp.sum, jnp.where(sc_ref[c] >= mid, 1.0, 0.0))
            return acc
        part = lax.fori_loop(0, lax.shift_right_logical(nch + 1, 1), body, jnp.zeros((8, C), F32))
        return jnp.sum(part, axis=0, keepdims=True)

    def load_state():
        return (st_ref[ROW_LO:ROW_LO + 1, :], st_ref[ROW_HI:ROW_HI + 1, :],
                st_ref[ROW_CNT:ROW_CNT + 1, :], st_ref[ROW_PROBE:ROW_PROBE + 1, :])

    def store_state(lo, hi, cnt_lo, probe):
        st_ref[ROW_LO:ROW_LO + 1, :] = lo
        st_ref[ROW_HI:ROW_HI + 1, :] = hi
        st_ref[ROW_CNT:ROW_CNT + 1, :] = cnt_lo
        st_ref[ROW_PROBE:ROW_PROBE + 1, :] = probe

    def open_cols(lo, hi, cnt_lo):
        return jnp.where(cnt_lo > k_eff, jnp.where(hi > lo, 1.0, 0.0), 0.0)

    def bisect(halve_image, lo, hi, cnt_lo, probe):
        klo = _order_key(pltpu.bitcast(lo, I32))
        khi = _order_key(pltpu.bitcast(hi, I32))
        i_mid = pltpu.bitcast(_order_key((klo >> 1) + (khi >> 1) + (klo & khi & 1)), F32)
        mid = jnp.where(halve_image, i_mid, 0.5 * lo + 0.5 * hi)
        probing = probe > lo
        mid = jnp.where(probing, probe, mid)
        ok = open_cols(lo, hi, cnt_lo) * jnp.where(mid > lo, jnp.where(mid < hi, 1.0, 0.0), 0.0)
        cnt = count_ge(mid)
        up = ok * jnp.where(cnt >= k_eff, 1.0, 0.0)
        dn = ok - up
        new_hi = jnp.where(dn + up * jnp.where(probing, 1.0, 0.0) > 0.0, mid, hi)
        return (jnp.where(up > 0.0, mid, lo), new_hi, jnp.where(up > 0.0, cnt, cnt_lo),
                jnp.full((1, C), -jnp.inf, F32))

    def snap(lo, hi, cnt_lo, probe):
        def body(c, carry):
            vmin, vmax = carry
            sc = sc_ref[c]
            vmin = jnp.minimum(vmin, fold_rows(jnp.min, jnp.where(sc >= lo, sc, jnp.inf)))
            vmax = jnp.maximum(vmax, fold_rows(jnp.max, jnp.where(sc < hi, sc, -jnp.inf)))
            return vmin, vmax
        vmin, vmax = lax.fori_loop(0, nch, body, (jnp.full((8, C), jnp.inf, F32),
                                                  jnp.full((8, C), -jnp.inf, F32)))
        vmin = jnp.min(vmin, axis=0, keepdims=True)
        vmax = jnp.max(vmax, axis=0, keepdims=True)
        is_open = open_cols(lo, hi, cnt_lo) > 0.0
        single = vmin >= vmax
        return (jnp.where(is_open, vmin, lo),
                jnp.where(is_open, jnp.where(single, vmin, hi), hi),
                cnt_lo,
                jnp.where(is_open, jnp.where(single, -jnp.inf, vmax), -jnp.inf))

    def search_cond(st):
        rnd, n_open = st
        return jnp.logical_and(rnd < MAX_ROUNDS, n_open > 0.0)

    def search_body(st):
        rnd, _ = st
        late = rnd >= SNAP_ROUND
        do_snap = jnp.logical_and(late, (rnd - SNAP_ROUND) % SNAP_PERIOD == 0)

        @pl.when(do_snap)
        def _():
            store_state(*snap(*load_state()))

        @pl.when(jnp.logical_not(do_snap))
        def _():
            store_state(*bisect(False, *load_state()))

        store_state(*bisect(late, *load_state()))
        lo, hi, cnt_lo, _ = load_state()
        return rnd + 1, jnp.max(open_cols(lo, hi, cnt_lo))

    kmax = _order_key(pltpu.bitcast(smax, I32))
    hi0 = pltpu.bitcast(_order_key(jnp.where(kmax == 2 ** 31 - 1, kmax, kmax + 1)), F32)
    store_state(smin, hi0, n_keys, jnp.full((1, C), -jnp.inf, F32))
    lax.while_loop(search_cond, search_body, (jnp.int32(0), jnp.max(open_cols(smin, hi0, n_keys))))
    th, _, cnt_th, _ = load_state()

    surplus = cnt_th - k_eff

    @pl.when(jnp.max(surplus) > 0.0)
    def _():
        later = jnp.where(col > row, 1.0, 0.0).astype(BF16)

        def drop_body(r, after):
            c = nch - 1 - r
            sc = sc_ref[c]
            tied = jnp.where(sc == th, 1.0, 0.0)
            follow = jnp.dot(later, tied.astype(BF16), preferred_element_type=F32) + after
            sc_ref[c] = jnp.where(tied * jnp.where(follow < surplus, 1.0, 0.0) > 0.0, -jnp.inf, sc)
            return after + jnp.sum(fold_rows(jnp.sum, tied), axis=0, keepdims=True)

        lax.fori_loop(0, nch, drop_body, jnp.zeros((1, C), F32))

    m_ref[...] = jnp.full(m_ref.shape, NEG_BIG, F32)
    acc_ref[...] = jnp.zeros(acc_ref.shape, F32)

    def attend(c, near):
        neg = jnp.where(sc_ref[c] >= th, 0.0, -jnp.inf)
        if near == 0:
            neg = jnp.where(row <= col, neg, -jnp.inf)
        neg_ref[...] = neg
        for h in range(ATTN_HEADS):
            p2 = (h // 2) * 2 * HEAD_DIM
            lg = jnp.dot(k_ref[0, c, :, p2:p2 + 2 * HEAD_DIM],
                         q2t_ref[0, 2 * HEAD_DIM * h:2 * HEAD_DIM * (h + 1), :],
                         preferred_element_type=F32)
            if near is not None:
                lg = lg + bias_ref[near, h]
            lg = lg + neg_ref[...]
            lg_ref[h] = lg
            m_old = m_ref[h:h + 1, :]
            m_new = jnp.maximum(m_old, jnp.max(lg, axis=0, keepdims=True))
            al_ref[h:h + 1, :] = jnp.exp2(m_old - m_new)
            m_ref[h:h + 1, :] = m_new
        for h in range(ATTN_HEADS):
            p_ref[h] = jnp.exp2(lg_ref[h] - m_ref[h:h + 1, :]).astype(BF16)
        for h in range(ATTN_HEADS):
            hs = slice(h * V_SLOT, (h + 1) * V_SLOT)
            pv = jnp.dot(vt_ref[0, c, hs, :], p_ref[h], preferred_element_type=F32)
            acc_ref[hs, :] = al_ref[h:h + 1, :] * acc_ref[hs, :] + pv

    def far_body(c, carry):
        attend(c, None)
        return carry

    lax.fori_loop(0, jnp.maximum(i - 1, 0), far_body, 0)

    @pl.when(i >= 1)
    def _():
        attend(i - 1, 1)

    attend(i, 0)

    out_t = jnp.concatenate(
        [acc_ref[h * V_SLOT:h * V_SLOT + HEAD_DIM, :] / acc_ref[h * V_SLOT + HEAD_DIM:h * V_SLOT + HEAD_DIM + 1, :]
         for h in range(ATTN_HEADS)], axis=0)
    o_ref[0] = out_t.T.astype(o_ref.dtype)


def _dsa_attention(q2t, iqt, iwt, k, vt, ik, bias):
    B, S, _ = k.shape
    nq = S // CHUNK
    assert nq % 2 == 0
    topk = min(TOPK_MAX, S // 4)
    k4 = k.reshape(B, nq, CHUNK, ATTN_WIDTH)
    ik4 = ik.reshape(B, nq, CHUNK, IDX_DIM)
    return pl.pallas_call(
        functools.partial(_attn_kernel, topk=topk),
        out_shape=jax.ShapeDtypeStruct((B, S, ATTN_WIDTH), BF16),
        grid=(B, nq),
        in_specs=[
            pl.BlockSpec((1, Q2_ROWS, CHUNK), lambda b, i: (b, 0, i)),
            pl.BlockSpec((1, IDX_HEADS * IDX_DIM, CHUNK), lambda b, i: (b, 0, i)),
            pl.BlockSpec((1, IDX_HEADS, CHUNK), lambda b, i: (b, 0, i)),
            pl.BlockSpec((1, nq, CHUNK, ATTN_WIDTH), lambda b, i: (b, 0, 0, 0)),
            pl.BlockSpec((1, nq, VT_ROWS, CHUNK), lambda b, i: (b, 0, 0, 0)),
            pl.BlockSpec((1, nq, CHUNK, IDX_DIM), lambda b, i: (b, 0, 0, 0)),
            pl.BlockSpec((2, ATTN_HEADS, CHUNK, CHUNK), lambda b, i: (0, 0, 0, 0)),
        ],
        out_specs=pl.BlockSpec((1, CHUNK, ATTN_WIDTH), lambda b, i: (b, i, 0)),
        scratch_shapes=[
            pltpu.VMEM((nq, CHUNK, CHUNK), F32),
            pltpu.VMEM((8, CHUNK), F32),
            pltpu.VMEM((CHUNK, CHUNK), F32),
            pltpu.VMEM((ATTN_HEADS, CHUNK, CHUNK), F32),
            pltpu.VMEM((ATTN_HEADS, CHUNK, CHUNK), BF16),
            pltpu.VMEM((ATTN_HEADS, CHUNK), F32),
            pltpu.VMEM((ATTN_HEADS, CHUNK), F32),
            pltpu.VMEM((VT_ROWS, CHUNK), F32),
        ],
        compiler_params=pltpu.CompilerParams(
            dimension_semantics=("arbitrary", "arbitrary"), vmem_limit_bytes=VMEM_LIMIT),
        name="dsa_attn",
    )(q2t, iqt, iwt, k4, vt, ik4, bias)


ROUTE_LANES = LANES


def _layer_norm(y, g, b):
    mu = jnp.mean(y, axis=-1, keepdims=True)
    yc = y - mu
    var = jnp.mean(yc * yc, axis=-1, keepdims=True)
    return yc * lax.rsqrt(var + LN_EPS) * g + b


def _mix_kernel(attn_ref, u_ref, halo_ref, x_ref, wpool_ref, pscale_ref, wout_ref, g_ref, b_ref,
                wr_ref, br_ref, h_ref, route_ref, rw_ref, *, tm):
    i = pl.program_id(1)
    halo = jnp.where(i > 0, halo_ref[0], 0.0)
    ue = jnp.concatenate([halo, u_ref[0]], axis=0)
    pos = i * tm + lax.broadcasted_iota(I32, (tm, 1), 0)
    mixed = []
    for g, w in enumerate(POOL_WINDOWS):
        gs = slice(g * POOL_GROUP_DIM, (g + 1) * POOL_GROUP_DIM)
        ch = ue[:, gs]
        win = ch
        step = 1
        while step < w:
            win = win + pltpu.roll(win, step, axis=0)
            step *= 2
        cnt = jnp.minimum(pos + 1, w).astype(F32)
        pooled = win[POOL_HALO:] / cnt - ch[POOL_HALO:]
        mg = jnp.dot(pooled.astype(BF16), wpool_ref[g], preferred_element_type=F32)
        mixed.append((mg * pscale_ref[:, gs]).astype(BF16))
    cat = jnp.concatenate([attn_ref[0]] + mixed, axis=-1)
    mix = jnp.dot(cat, wout_ref[...], preferred_element_type=F32)
    h = _layer_norm(DEEPNORM_ALPHA * x_ref[0] + mix, g_ref[...], b_ref[...])
    h_ref[0] = h
    hb = h.astype(BF16)

    lg = jnp.dot(hb, wr_ref[...], preferred_element_type=F32) + br_ref[...]
    lane = lax.broadcasted_iota(I32, (tm, ROUTE_LANES), 1)
    gl = jnp.where(lane >= N_EXPERTS, jnp.where(lane < N_EXPERTS + N_GROUPS, lg, -jnp.inf), -jnp.inf)
    ge = jnp.exp(gl - jnp.max(gl, axis=-1, keepdims=True))
    pg = ge / jnp.sum(ge, axis=-1, keepdims=True)
    pg_top = jnp.max(pg, axis=-1, keepdims=True)
    g_lane = jnp.min(jnp.where(pg == pg_top, lane, ROUTE_LANES), axis=-1, keepdims=True)
    e_lo = (g_lane - N_EXPERTS) * EXPERTS_PER_GROUP
    fl = jnp.where(lane >= e_lo, jnp.where(lane < e_lo + EXPERTS_PER_GROUP, lg, -jnp.inf), -jnp.inf)
    fe = jnp.exp(fl - jnp.max(fl, axis=-1, keepdims=True))
    pf = fe / jnp.sum(fe, axis=-1, keepdims=True)
    p1 = jnp.max(pf, axis=-1, keepdims=True)
    i1 = jnp.min(jnp.where(pf == p1, lane, ROUTE_LANES), axis=-1, keepdims=True)
    pr = jnp.where(lane == i1, -1.0, jnp.where(fl == -jnp.inf, -1.0, pf))
    p2 = jnp.max(pr, axis=-1, keepdims=True)
    i2 = jnp.min(jnp.where(pr == p2, lane, ROUTE_LANES), axis=-1, keepdims=True)
    psum = p1 + p2
    w1 = pg_top * p1 / psum
    w2 = pg_top * p2 / psum
    ids = jnp.where(lane == 0, i1, jnp.where(lane == 1, i2, 0)).astype(F32)
    route_ref[...] = ids.T[0:8, :].astype(I32)
    lane8 = lax.broadcasted_iota(I32, (tm, 8), 1)
    rw_ref[0] = jnp.where(lane8 == 0, w1, jnp.where(lane8 == 1, w2, 0.0))


def _mix_norm_route(attn, u, x, w_pool, pool_scale, w_out, ln_g, ln_b, w_r1, b_r1, w_r2, b_r2, *, tm=256):
    B, S, D = x.shape
    wr = jnp.concatenate([w_r2, w_r1, jnp.zeros((D, ROUTE_LANES - N_EXPERTS - N_GROUPS), w_r1.dtype)],
                         axis=1).astype(BF16)
    br = jnp.concatenate([b_r2, b_r1, jnp.zeros((ROUTE_LANES - N_EXPERTS - N_GROUPS,), b_r1.dtype)])[None, :]
    hpt = tm // POOL_HALO
    return pl.pallas_call(
        functools.partial(_mix_kernel, tm=tm),
        out_shape=(jax.ShapeDtypeStruct((B, S, D), F32),
                   jax.ShapeDtypeStruct((8, B * S), I32),
                   jax.ShapeDtypeStruct((B, S, 8), F32)),
        grid=(B, S // tm),
        in_specs=[
            pl.BlockSpec((1, tm, ATTN_WIDTH), lambda b, i: (b, i, 0)),
            pl.BlockSpec((1, tm, POOL_WIDTH), lambda b, i: (b, i, 0)),
            pl.BlockSpec((1, POOL_HALO, POOL_WIDTH), lambda b, i: (b, jnp.maximum(i * hpt - 1, 0), 0)),
            pl.BlockSpec((1, tm, D), lambda b, i: (b, i, 0)),
            pl.BlockSpec((len(POOL_WINDOWS), POOL_GROUP_DIM, POOL_GROUP_DIM), lambda b, i: (0, 0, 0)),
            pl.BlockSpec((1, POOL_WIDTH), lambda b, i: (0, 0)),
            pl.BlockSpec((D, D), lambda b, i: (0, 0)),
            pl.BlockSpec((1, D), lambda b, i: (0, 0)),
            pl.BlockSpec((1, D), lambda b, i: (0, 0)),
            pl.BlockSpec((D, ROUTE_LANES), lambda b, i: (0, 0)),
            pl.BlockSpec((1, ROUTE_LANES), lambda b, i: (0, 0)),
        ],
        out_specs=(pl.BlockSpec((1, tm, D), lambda b, i: (b, i, 0)),
                   pl.BlockSpec((8, tm), lambda b, i: (0, b * (S // tm) + i)),
                   pl.BlockSpec((1, tm, 8), lambda b, i: (b, i, 0))),
        compiler_params=pltpu.CompilerParams(
            dimension_semantics=("arbitrary", "arbitrary"), vmem_limit_bytes=VMEM_LIMIT),
        name="mix_norm",
    )(attn, u, u, x, w_pool.astype(BF16), pool_scale[None, :], w_out.astype(BF16),
      ln_g[None, :], ln_b[None, :], wr, br)


MOE_TM = 256
PLAN_TB = 1024
GATHER_UNROLL = 8


def _moe_rows(n_tokens):
    return 2 * n_tokens + N_EXPERTS * MOE_TM


def _plan_kernel(route_ref, pos_ref, tile_ref, cnt_ref, off_ref, carry_ref, *, n_tile_lanes):
    ph = pl.program_id(0)
    b = pl.program_id(1)
    tb = route_ref.shape[1]
    esub = lax.broadcasted_iota(I32, (N_EXPERTS, tb), 0)
    a1 = jnp.where(esub == route_ref[0:1, :], 1.0, 0.0)
    a2 = jnp.where(esub == route_ref[1:2, :], 1.0, 0.0)
    a = a1 + a2
    n_here = jnp.sum(a, axis=1, keepdims=True)

    @pl.when(jnp.logical_and(ph == 0, b == 0))
    def _():
        cnt_ref[...] = jnp.zeros(cnt_ref.shape, F32)

    @pl.when(ph == 0)
    def _():
        cnt_ref[...] += n_here

    @pl.when(jnp.logical_and(ph == 1, b == 0))
    def _():
        n_tile = jnp.floor((cnt_ref[...] + (MOE_TM - 1)) * (1.0 / MOE_TM))
        er = lax.broadcasted_iota(I32, (N_EXPERTS, N_EXPERTS), 0)
        ec = lax.broadcasted_iota(I32, (N_EXPERTS, N_EXPERTS), 1)
        before = jnp.where(ec < er, 1.0, 0.0).astype(BF16)
        t_off = jnp.dot(before, n_tile.astype(BF16), preferred_element_type=F32)
        off_ref[...] = t_off * MOE_TM
        carry_ref[...] = jnp.zeros(carry_ref.shape, F32)
        j = lax.broadcasted_iota(I32, (N_EXPERTS, n_tile_lanes), 1).astype(F32)
        owner = jnp.sum(jnp.where(t_off[:, 0:1] <= j, 1.0, 0.0), axis=0, keepdims=True) - 1.0
        total = jnp.sum(n_tile[:, 0:1], axis=0, keepdims=True)
        used = jnp.where(j[0:1, :] < total, 1.0, 0.0)
        row8 = lax.broadcasted_iota(I32, (8, n_tile_lanes), 0)
        tile_ref[...] = jnp.where(row8 == 0, owner, jnp.where(row8 == 1, used, 0.0)).astype(I32)

    @pl.when(ph == 1)
    def _():
        tr = lax.broadcasted_iota(I32, (tb, tb), 0)
        tc = lax.broadcasted_iota(I32, (tb, tb), 1)
        earlier = jnp.where(tr < tc, 1.0, 0.0).astype(BF16)
        seen = jnp.dot(a.astype(BF16), earlier, preferred_element_type=F32)
        dest = seen + carry_ref[:, 0:1] + off_ref[:, 0:1]
        p1 = jnp.sum(a1 * dest, axis=0, keepdims=True)
        p2 = jnp.sum(a2 * dest, axis=0, keepdims=True)
        row8 = lax.broadcasted_iota(I32, (8, tb), 0)
        pos_ref[...] = jnp.where(row8 == 0, p1, jnp.where(row8 == 1, p2, 0.0)).astype(I32)
        carry_ref[...] += n_here


def _moe_plan(route_t):
    _, T = route_t.shape
    n_tiles = _moe_rows(T) // MOE_TM
    n_tile_lanes = -(-n_tiles // LANES) * LANES
    nb = T // PLAN_TB
    return pl.pallas_call(
        functools.partial(_plan_kernel, n_tile_lanes=n_tile_lanes),
        out_shape=(jax.ShapeDtypeStruct((8, T), I32), jax.ShapeDtypeStruct((8, n_tile_lanes), I32)),
        grid=(2, nb),
        in_specs=[pl.BlockSpec((8, PLAN_TB), lambda ph, b: (0, b))],
        out_specs=(pl.BlockSpec((8, PLAN_TB), lambda ph, b: (0, b * ph)),
                   pl.BlockSpec((8, n_tile_lanes), lambda ph, b: (0, 0))),
        scratch_shapes=[pltpu.VMEM((N_EXPERTS, LANES), F32),
                        pltpu.VMEM((N_EXPERTS, LANES), F32),
                        pltpu.VMEM((N_EXPERTS, LANES), F32)],
        compiler_params=pltpu.CompilerParams(dimension_semantics=("arbitrary", "arbitrary")),
        name="moe_plan",
    )(route_t)


def _row_copy(src_ref, src_row, dst_ref, dst_row, sem):
    return pltpu.make_async_copy(src_ref.at[pl.ds(src_row, 1)], dst_ref.at[pl.ds(dst_row, 1)], sem)


def _dispatch_kernel(p1_ref, p2_ref, h_ref, init_hbm, out_hbm, sem, *, tb):
    del init_hbm

    def start(t, carry):
        _row_copy(h_ref, t, out_hbm, p1_ref[t], sem).start()
        _row_copy(h_ref, t, out_hbm, p2_ref[t], sem).start()
        return carry

    lax.fori_loop(0, tb, start, 0, unroll=GATHER_UNROLL)

    def wait(t, carry):
        _row_copy(h_ref, 0, out_hbm, 0, sem).wait()
        _row_copy(h_ref, 0, out_hbm, 0, sem).wait()
        return carry

    lax.fori_loop(0, tb, wait, 0, unroll=GATHER_UNROLL)


def _moe_dispatch(h, pos1, pos2, *, tb=512):
    T, D = h.shape
    rows = _moe_rows(T)
    return pl.pallas_call(
        functools.partial(_dispatch_kernel, tb=tb),
        out_shape=jax.ShapeDtypeStruct((rows, D), h.dtype),
        grid=(T // tb,),
        in_specs=[pl.BlockSpec((tb,), lambda i: (i,), memory_space=pltpu.SMEM),
                  pl.BlockSpec((tb,), lambda i: (i,), memory_space=pltpu.SMEM),
                  pl.BlockSpec((tb, D), lambda i: (i, 0)),
                  pl.BlockSpec(memory_space=pl.ANY)],
        out_specs=pl.BlockSpec(memory_space=pl.ANY),
        scratch_shapes=[pltpu.SemaphoreType.DMA(())],
        input_output_aliases={3: 0},
        compiler_params=pltpu.CompilerParams(dimension_semantics=("arbitrary",)),
        name="moe_dispatch",
    )(pos1, pos2, h, jnp.zeros((rows, D), h.dtype))


def _ffn_kernel(owner_ref, used_ref, x_ref, wg_ref, wu_ref, wd_ref, o_ref):
    j = pl.program_id(0)

    @pl.when(used_ref[j] == 1)
    def _():
        xb = x_ref[...].astype(BF16)
        gate = jnp.dot(xb, wg_ref[0].astype(BF16), preferred_element_type=F32)
        up = jnp.dot(xb, wu_ref[0].astype(BF16), preferred_element_type=F32)
        a = gate * jax.nn.sigmoid(gate) * up
        o_ref[...] = jnp.dot(a.astype(BF16), wd_ref[0].astype(BF16), preferred_element_type=F32)

    @pl.when(used_ref[j] == 0)
    def _():
        o_ref[...] = jnp.zeros(o_ref.shape, F32)


def _moe_ffn(xs, owner, used, w_gate, w_up, w_down):
    rows, D = xs.shape
    E, _, F = w_gate.shape
    return pl.pallas_call(
        _ffn_kernel,
        out_shape=jax.ShapeDtypeStruct((rows, D), F32),
        grid_spec=pltpu.PrefetchScalarGridSpec(
            num_scalar_prefetch=2,
            grid=(rows // MOE_TM,),
            in_specs=[pl.BlockSpec((MOE_TM, D), lambda j, ow, us: (j, 0)),
                      pl.BlockSpec((1, D, F), lambda j, ow, us: (ow[j], 0, 0)),
                      pl.BlockSpec((1, D, F), lambda j, ow, us: (ow[j], 0, 0)),
                      pl.BlockSpec((1, F, D), lambda j, ow, us: (ow[j], 0, 0))],
            out_specs=pl.BlockSpec((MOE_TM, D), lambda j, ow, us: (j, 0)),
        ),
        compiler_params=pltpu.CompilerParams(dimension_semantics=("arbitrary",), vmem_limit_bytes=VMEM_LIMIT),
        name="moe_ffn",
    )(owner, used, xs, w_gate, w_up, w_down)


def _combine_kernel(p1_ref, p2_ref, p1n_ref, p2n_ref, rw_ref, h_ref, g_ref, b_ref, y_hbm, o_ref,
                    buf_ref, sem, *, tm):
    s = pl.program_id(0)
    slot = s % 2

    def fetch(pa_ref, pb_ref, into):
        def start(t, carry):
            pltpu.make_async_copy(y_hbm.at[pl.ds(pa_ref[t], 1)], buf_ref.at[into, 0, pl.ds(t, 1)],
                                  sem.at[into]).start()
            pltpu.make_async_copy(y_hbm.at[pl.ds(pb_ref[t], 1)], buf_ref.at[into, 1, pl.ds(t, 1)],
                                  sem.at[into]).start()
            return carry
        lax.fori_loop(0, tm, start, 0, unroll=GATHER_UNROLL)

    @pl.when(s == 0)
    def _():
        fetch(p1_ref, p2_ref, 0)

    @pl.when(s + 1 < pl.num_programs(0))
    def _():
        fetch(p1n_ref, p2n_ref, 1 - slot)

    def wait(t, carry):
        for half in range(2):
            pltpu.make_async_copy(y_hbm.at[pl.ds(0, 1)], buf_ref.at[slot, half, pl.ds(0, 1)], sem.at[slot]).wait()
        return carry

    lax.fori_loop(0, tm, wait, 0, unroll=GATHER_UNROLL)
    y = rw_ref[:, 0:1] * buf_ref[slot, 0] + rw_ref[:, 1:2] * buf_ref[slot, 1]
    o_ref[...] = _layer_norm(DEEPNORM_ALPHA * h_ref[...] + y, g_ref[...], b_ref[...])


def _moe_combine(h, y_sorted, pos1, pos2, rw, ln_g, ln_b, *, tm=256):
    T, D = h.shape
    last = T // tm - 1
    smem = lambda imap: pl.BlockSpec((tm,), imap, memory_space=pltpu.SMEM)
    return pl.pallas_call(
        functools.partial(_combine_kernel, tm=tm),
        out_shape=jax.ShapeDtypeStruct((T, D), F32),
        grid=(T // tm,),
        in_specs=[smem(lambda i: (i,)), smem(lambda i: (i,)),
                  smem(lambda i: (jnp.minimum(i + 1, last),)), smem(lambda i: (jnp.minimum(i + 1, last),)),
                  pl.BlockSpec((tm, 8), lambda i: (i, 0)),
                  pl.BlockSpec((tm, D), lambda i: (i, 0)),
                  pl.BlockSpec((1, D), lambda i: (0, 0)),
                  pl.BlockSpec((1, D), lambda i: (0, 0)),
                  pl.BlockSpec(memory_space=pl.ANY)],
        out_specs=pl.BlockSpec((tm, D), lambda i: (i, 0)),
        scratch_shapes=[pltpu.VMEM((2, 2, tm, D), F32), pltpu.SemaphoreType.DMA((2,))],
        compiler_params=pltpu.CompilerParams(dimension_semantics=("arbitrary",), vmem_limit_bytes=VMEM_LIMIT),
        name="moe_combine",
    )(pos1, pos2, pos1, pos2, rw, h, ln_g[None, :], ln_b[None, :], y_sorted)


def _moe_norm(h, route_t, rw, w_gate, w_up, w_down, ln_g, ln_b):
    pos, tiles = _moe_plan(route_t)
    xs = _moe_dispatch(h, pos[0], pos[1])
    n_tiles = xs.shape[0] // MOE_TM
    ys = _moe_ffn(xs, tiles[0, :n_tiles], tiles[1, :n_tiles], w_gate, w_up, w_down)
    return _moe_combine(h, ys, pos[0], pos[1], rw, ln_g, ln_b)


def kernel(x, w_in, w_pool, pool_scale, w_out, rel_bias, ln1_g, ln1_b, w_r1, b_r1, w_r2, b_r2,
           w_gate, w_up, w_down, ln2_g, ln2_b):
    B, S, D = x.shape
    assert w_in.shape[0] == DEPTH == 1 and S % CHUNK == 0
    bias = _bias_tiles(rel_bias)
    q2t, vt, iqt, iwt, k, ik, u = _in_proj(x, w_in[0])
    attn = _dsa_attention(q2t, iqt, iwt, k, vt, ik, bias)
    h, route_t, rw = _mix_norm_route(attn, u, x, w_pool[0], pool_scale[0], w_out[0], ln1_g[0], ln1_b[0],
                                     w_r1[0], b_r1[0], w_r2[0], b_r2[0])
    out = _moe_norm(h.reshape(B * S, D), route_t, rw.reshape(B * S, 8),
                    w_gate[0], w_up[0], w_down[0], ln2_g[0], ln2_b[0])
    return out.reshape(B, S, D)
```

```python
import functools
import math

import numpy as np
import jax
import jax.numpy as jnp
from jax import lax
from jax.experimental import pallas as pl
from jax.experimental.pallas import tpu as pltpu

F32 = jnp.float32
BF16 = jnp.bfloat16
I32 = jnp.int32

ATTN_HEADS = 8
HEAD_DIM = 64
ATTN_WIDTH = ATTN_HEADS * HEAD_DIM
IDX_HEADS = 8
IDX_DIM = 64
TOPK_MAX = 256
POOL_WINDOWS = (2, 4, 8, 16)
POOL_GROUP_DIM = 128
POOL_WIDTH = len(POOL_WINDOWS) * POOL_GROUP_DIM
POOL_HALO = 16
REL_BUCKETS = 32
REL_MAX_DIST = 128
N_GROUPS = 4
EXPERTS_PER_GROUP = 8
N_EXPERTS = N_GROUPS * EXPERTS_PER_GROUP
LN_EPS = 1e-5
DEPTH = 1
DEEPNORM_ALPHA = (2 * DEPTH) ** 0.25
LOG2E = math.log2(math.e)

LANES = 128
CHUNK = 256
INT_MIN = -2 ** 31
NEG_BIG = -1e30
VMEM_LIMIT = 56 * 1024 * 1024


def _rel_bucket_table(n):
    max_exact = REL_BUCKETS // 2
    d = np.arange(n)
    nf = np.maximum(d, 1).astype(np.float32)
    ratio = np.log(nf / np.float32(max_exact)) / np.float32(math.log(REL_MAX_DIST / max_exact))
    large = max_exact + (ratio * np.float32(REL_BUCKETS - max_exact)).astype(np.int32)
    large = np.minimum(large, REL_BUCKETS - 1)
    return np.where(d < max_exact, d, large).astype(np.int32)


def _near_bucket_tiles():
    tbl = _rel_bucket_table(2 * CHUNK)
    a = np.arange(CHUNK)[:, None]
    b = np.arange(CHUNK)[None, :]
    tiles = [tbl[np.maximum(delta * CHUNK + b - a, 0)] for delta in (0, 1)]
    return np.stack(tiles).astype(np.int32)


FAR_BUCKET = REL_BUCKETS - 1
assert int(_rel_bucket_table(2 * CHUNK)[CHUNK + 1:].min()) == FAR_BUCKET


def _bias_kernel(rb_ref, bucket_ref, o_ref):
    h = pl.program_id(1)
    bk = bucket_ref[0]
    far = rb_ref[FAR_BUCKET, h]
    acc = jnp.zeros(bk.shape, F32)
    for n in range(REL_BUCKETS):
        acc = jnp.where(bk == n, rb_ref[n, h] - far, acc)
    o_ref[0, 0] = acc * LOG2E


def _bias_tiles(rel_bias):
    buckets = jnp.asarray(_near_bucket_tiles())
    return pl.pallas_call(
        _bias_kernel,
        out_shape=jax.ShapeDtypeStruct((2, ATTN_HEADS, CHUNK, CHUNK), F32),
        grid=(2, ATTN_HEADS),
        in_specs=[pl.BlockSpec(memory_space=pltpu.SMEM),
                  pl.BlockSpec((1, CHUNK, CHUNK), lambda d, h: (d, 0, 0))],
        out_specs=pl.BlockSpec((1, 1, CHUNK, CHUNK), lambda d, h: (d, h, 0, 0)),
        name="bias_tiles",
    )(rel_bias, buckets)


V_SLOT = HEAD_DIM + 16
VT_ROWS = ATTN_HEADS * V_SLOT
WT_Q, WT_V, WT_IQ, WT_IW = 0, ATTN_WIDTH, ATTN_WIDTH + VT_ROWS, 2 * ATTN_WIDTH + VT_ROWS
WT_ROWS = WT_IW + 16
WN_K, WN_U, WN_IK = 0, ATTN_WIDTH, ATTN_WIDTH + POOL_WIDTH
WN_COLS = WN_IK + LANES


def _proj_kernel(x_ref, wt_ref, wn_ref, qt_ref, vt_ref, iqt_ref, iwt_ref, k_ref, ik_ref, u_ref, *, tm):
    xb = x_ref[0].astype(BF16)
    t = lax.dot_general(wt_ref[...], xb, (((1,), (1,)), ((), ())),
                        preferred_element_type=F32)
    qt_ref[0] = (t[WT_Q:WT_V] * (HEAD_DIM ** -0.5 * LOG2E)).astype(BF16)
    iqt_ref[0] = (t[WT_IQ:WT_IW] * (IDX_DIM ** -0.5)).astype(BF16)
    iwt_ref[0] = t[WT_IW:WT_IW + IDX_HEADS] * (IDX_HEADS ** -0.5)
    slot_row = lax.broadcasted_iota(I32, (VT_ROWS, tm), 0) % V_SLOT
    vt = jnp.where(slot_row >= HEAD_DIM, 1.0, t[WT_V:WT_IQ]).astype(BF16)
    for j in range(tm // CHUNK):
        vt_ref[0, j] = vt[:, j * CHUNK:(j + 1) * CHUNK]
    n = jnp.dot(xb, wn_ref[...], preferred_element_type=F32)
    k_ref[0] = n[:, WN_K:WN_U].astype(BF16)
    u_ref[0] = n[:, WN_U:WN_IK]
    ik_ref[0] = n[:, WN_IK:WN_IK + IDX_DIM].astype(BF16)


def _in_proj(x, w_in, *, tm=512):
    B, S, D = x.shape
    q_off, k_off, v_off = 0, ATTN_WIDTH, 2 * ATTN_WIDTH
    iq_off = 3 * ATTN_WIDTH
    ik_off = iq_off + IDX_HEADS * IDX_DIM
    iw_off = ik_off + IDX_DIM
    p_off = iw_off + IDX_HEADS
    wv = w_in[:, v_off:iq_off].T.reshape(ATTN_HEADS, HEAD_DIM, D)
    wv = jnp.pad(wv, ((0, 0), (0, V_SLOT - HEAD_DIM), (0, 0))).reshape(VT_ROWS, D)
    wt = jnp.concatenate([
        w_in[:, q_off:k_off].T, wv, w_in[:, iq_off:ik_off].T, w_in[:, iw_off:p_off].T,
        jnp.zeros((WT_ROWS - WT_IW - IDX_HEADS, D), w_in.dtype)], axis=0).astype(BF16)
    wn = jnp.concatenate([
        w_in[:, k_off:v_off], w_in[:, p_off:], w_in[:, ik_off:iw_off],
        jnp.zeros((D, WN_COLS - WN_IK - IDX_DIM), w_in.dtype)], axis=1).astype(BF16)
    nt = S // tm
    cpt = tm // CHUNK
    outs = pl.pallas_call(
        functools.partial(_proj_kernel, tm=tm),
        out_shape=(
            jax.ShapeDtypeStruct((B, ATTN_WIDTH, S), BF16),
            jax.ShapeDtypeStruct((B, S // CHUNK, VT_ROWS, CHUNK), BF16),
            jax.ShapeDtypeStruct((B, IDX_HEADS * IDX_DIM, S), BF16),
            jax.ShapeDtypeStruct((B, IDX_HEADS, S), F32),
            jax.ShapeDtypeStruct((B, S, ATTN_WIDTH), BF16),
            jax.ShapeDtypeStruct((B, S, IDX_DIM), BF16),
            jax.ShapeDtypeStruct((B, S, POOL_WIDTH), F32),
        ),
        grid=(B, nt),
        in_specs=[pl.BlockSpec((1, tm, D), lambda b, i: (b, i, 0)),
                  pl.BlockSpec((WT_ROWS, D), lambda b, i: (0, 0)),
                  pl.BlockSpec((D, WN_COLS), lambda b, i: (0, 0))],
        out_specs=(
            pl.BlockSpec((1, ATTN_WIDTH, tm), lambda b, i: (b, 0, i)),
            pl.BlockSpec((1, cpt, VT_ROWS, CHUNK), lambda b, i: (b, i, 0, 0)),
            pl.BlockSpec((1, IDX_HEADS * IDX_DIM, tm), lambda b, i: (b, 0, i)),
            pl.BlockSpec((1, IDX_HEADS, tm), lambda b, i: (b, 0, i)),
            pl.BlockSpec((1, tm, ATTN_WIDTH), lambda b, i: (b, i, 0)),
            pl.BlockSpec((1, tm, IDX_DIM), lambda b, i: (b, i, 0)),
            pl.BlockSpec((1, tm, POOL_WIDTH), lambda b, i: (b, i, 0)),
        ),
        compiler_params=pltpu.CompilerParams(
            dimension_semantics=("arbitrary", "arbitrary"), vmem_limit_bytes=VMEM_LIMIT),
        name="in_proj",
    )(x, wt, wn)
    return outs


SNAP_ROUND = 7
SNAP_PERIOD = 3
MAX_ROUNDS = SNAP_ROUND + SNAP_PERIOD * 17
ROW_LO, ROW_HI, ROW_CNT, ROW_PROBE = 0, 1, 2, 3
FAR_SPAN = 2


def _order_key(bits):
    return bits ^ ((bits >> 31) & 0x7FFFFFFF)


Q2_ROWS = 2 * ATTN_WIDTH


def _attn_kernel(qt_ref, iqt_ref, iwt_ref, k_ref, vt_ref, ik_ref, bias_ref, o_ref,
                 sc_ref, st_ref, q2t_ref, neg_ref, lg_ref, p_ref, m_ref, al_ref, acc_ref,
                 *, topk):
    i = pl.program_id(1)
    nch = i + 1
    C = CHUNK

    @pl.when(jnp.logical_and(pl.program_id(0) == 0, i == 0))
    def _():
        q2t_ref[...] = jnp.zeros(q2t_ref.shape, BF16)

    for h in range(ATTN_HEADS):
        lo = 2 * HEAD_DIM * h + HEAD_DIM * (h % 2)
        q2t_ref[lo:lo + HEAD_DIM, :] = qt_ref[0, h * HEAD_DIM:(h + 1) * HEAD_DIM, :]
    row = lax.broadcasted_iota(I32, (C, C), 0)
    col = lax.broadcasted_iota(I32, (C, C), 1)

    def fold_rows(op, x):
        return op(x.reshape(C // 8, 8, C), axis=0)

    def score_body(c, carry):
        smin, smax = carry
        ikc = ik_ref[0, c]
        s = jnp.zeros((C, C), F32)
        for j in range(IDX_HEADS):
            d = jnp.dot(ikc, iqt_ref[0, j * IDX_DIM:(j + 1) * IDX_DIM, :],
                        preferred_element_type=F32)
            s = s + iwt_ref[0, j:j + 1, :] * jnp.maximum(d, 0.0)
        causal = (c * C + row) <= (i * C + col)
        sc_ref[c] = jnp.where(causal, s, -jnp.inf)
        smin = jnp.minimum(smin, fold_rows(jnp.min, jnp.where(causal, s, jnp.inf)))
        smax = jnp.maximum(smax, fold_rows(jnp.max, jnp.where(causal, s, -jnp.inf)))
        return smin, smax

    smin, smax = lax.fori_loop(0, nch, score_body,
                               (jnp.full((8, C), jnp.inf, F32), jnp.full((8, C), -jnp.inf, F32)))
    smin = jnp.min(smin, axis=0, keepdims=True)
    smax = jnp.max(smax, axis=0, keepdims=True)

    pos = i * C + lax.broadcasted_iota(I32, (1, C), 1)
    n_keys = (pos + 1).astype(F32)
    k_eff = jnp.minimum(pos + 1, topk).astype(F32)

    @pl.when(nch % 2 == 1)
    def _():
        sc_ref[nch] = jnp.full((C, C), -jnp.inf, F32)

    def count_ge(mid):
        def body(cp, acc):
            for c in (2 * cp, 2 * cp + 1):
                acc = acc + fold_rows(jnp.sum, jnp.where(sc_ref[c] >= mid, 1.0, 0.0))
            return acc
        part = lax.fori_loop(0, lax.shift_right_logical(nch + 1, 1), body, jnp.zeros((8, C), F32))
        return jnp.sum(part, axis=0, keepdims=True)

    def load_state():
        return (st_ref[ROW_LO:ROW_LO + 1, :], st_ref[ROW_HI:ROW_HI + 1, :],
                st_ref[ROW_CNT:ROW_CNT + 1, :], st_ref[ROW_PROBE:ROW_PROBE + 1, :])

    def store_state(lo, hi, cnt_lo, probe):
        st_ref[ROW_LO:ROW_LO + 1, :] = lo
        st_ref[ROW_HI:ROW_HI + 1, :] = hi
        st_ref[ROW_CNT:ROW_CNT + 1, :] = cnt_lo
        st_ref[ROW_PROBE:ROW_PROBE + 1, :] = probe

    def open_cols(lo, hi, cnt_lo):
        return jnp.where(cnt_lo > k_eff, jnp.where(hi > lo, 1.0, 0.0), 0.0)

    def bisect(halve_image, lo, hi, cnt_lo, probe):
        klo = _order_key(pltpu.bitcast(lo, I32))
        khi = _order_key(pltpu.bitcast(hi, I32))
        i_mid = pltpu.bitcast(_order_key((klo >> 1) + (khi >> 1) + (klo & khi & 1)), F32)
        mid = jnp.where(halve_image, i_mid, 0.5 * lo + 0.5 * hi)
        probing = probe > lo
        mid = jnp.where(probing, probe, mid)
        ok = open_cols(lo, hi, cnt_lo) * jnp.where(mid > lo, jnp.where(mid < hi, 1.0, 0.0), 0.0)
        cnt = count_ge(mid)
        up = ok * jnp.where(cnt >= k_eff, 1.0, 0.0)
        dn = ok - up
        new_hi = jnp.where(dn + up * jnp.where(probing, 1.0, 0.0) > 0.0, mid, hi)
        return (jnp.where(up > 0.0, mid, lo), new_hi, jnp.where(up > 0.0, cnt, cnt_lo),
                jnp.full((1, C), -jnp.inf, F32))

    def snap(lo, hi, cnt_lo, probe):
        def body(c, carry):
            vmin, vmax = carry
            sc = sc_ref[c]
            vmin = jnp.minimum(vmin, fold_rows(jnp.min, jnp.where(sc >= lo, sc, jnp.inf)))
            vmax = jnp.maximum(vmax, fold_rows(jnp.max, jnp.where(sc < hi, sc, -jnp.inf)))
            return vmin, vmax
        vmin, vmax = lax.fori_loop(0, nch, body, (jnp.full((8, C), jnp.inf, F32),
                                                  jnp.full((8, C), -jnp.inf, F32)))
        vmin = jnp.min(vmin, axis=0, keepdims=True)
        vmax = jnp.max(vmax, axis=0, keepdims=True)
        is_open = open_cols(lo, hi, cnt_lo) > 0.0
        single = vmin >= vmax
        return (jnp.where(is_open, vmin, lo),
                jnp.where(is_open, jnp.where(single, vmin, hi), hi),
                cnt_lo,
                jnp.where(is_open, jnp.where(single, -jnp.inf, vmax), -jnp.inf))

    def search_cond(st):
        rnd, n_open = st
        return jnp.logical_and(rnd < MAX_ROUNDS, n_open > 0.0)

    def search_body(st):
        rnd, _ = st
        late = rnd >= SNAP_ROUND
        do_snap = jnp.logical_and(late, (rnd - SNAP_ROUND) % SNAP_PERIOD == 0)

        @pl.when(do_snap)
        def _():
            store_state(*snap(*load_state()))

        @pl.when(jnp.logical_not(do_snap))
        def _():
            store_state(*bisect(False, *load_state()))

        store_state(*bisect(late, *load_state()))
        lo, hi, cnt_lo, _ = load_state()
        return rnd + 1, jnp.max(open_cols(lo, hi, cnt_lo))

    kmax = _order_key(pltpu.bitcast(smax, I32))
    hi0 = pltpu.bitcast(_order_key(jnp.where(kmax == 2 ** 31 - 1, kmax, kmax + 1)), F32)
    store_state(smin, hi0, n_keys, jnp.full((1, C), -jnp.inf, F32))
    lax.while_loop(search_cond, search_body, (jnp.int32(0), jnp.max(open_cols(smin, hi0, n_keys))))
    th, _, cnt_th, _ = load_state()

    surplus = cnt_th - k_eff

    @pl.when(jnp.max(surplus) > 0.0)
    def _():
        later = jnp.where(col > row, 1.0, 0.0).astype(BF16)

        def drop_body(r, after):
            c = nch - 1 - r
            sc = sc_ref[c]
            tied = jnp.where(sc == th, 1.0, 0.0)
            follow = jnp.dot(later, tied.astype(BF16), preferred_element_type=F32) + after
            sc_ref[c] = jnp.where(tied * jnp.where(follow < surplus, 1.0, 0.0) > 0.0, -jnp.inf, sc)
            return after + jnp.sum(fold_rows(jnp.sum, tied), axis=0, keepdims=True)

        lax.fori_loop(0, nch, drop_body, jnp.zeros((1, C), F32))

    m_ref[...] = jnp.full(m_ref.shape, NEG_BIG, F32)
    acc_ref[...] = jnp.zeros(acc_ref.shape, F32)

    def attend(c, near, span=1):
        n = span * C
        neg = jnp.where(sc_ref[pl.ds(c, span)].reshape(n, C) >= th, 0.0, -jnp.inf)
        if near == 0:
            neg = jnp.where(row <= col, neg, -jnp.inf)
        neg_ref[0:n, :] = neg
        for h in range(ATTN_HEADS):
            p2 = (h // 2) * 2 * HEAD_DIM
            lg = jnp.dot(k_ref[0, pl.ds(c, span), :, p2:p2 + 2 * HEAD_DIM].reshape(n, 2 * HEAD_DIM),
                         q2t_ref[2 * HEAD_DIM * h:2 * HEAD_DIM * (h + 1), :],
                         preferred_element_type=F32)
            if near is not None:
                lg = lg + bias_ref[near, h]
            lg = lg + neg_ref[0:n, :]
            lg_ref[h, 0:n, :] = lg
            m_old = m_ref[h:h + 1, :]
            m_new = jnp.maximum(m_old, jnp.max(lg, axis=0, keepdims=True))
            al_ref[h:h + 1, :] = jnp.exp2(m_old - m_new)
            m_ref[h:h + 1, :] = m_new
        for h in range(ATTN_HEADS):
            p_ref[h, 0:n, :] = jnp.exp2(lg_ref[h, 0:n, :] - m_ref[h:h + 1, :]).astype(BF16)
        for h in range(ATTN_HEADS):
            hs = slice(h * V_SLOT, (h + 1) * V_SLOT)
            pv = jnp.dot(vt_ref[0, c, hs, :], p_ref[h, 0:C, :], preferred_element_type=F32)
            for j in range(1, span):
                pv = pv + jnp.dot(vt_ref[0, c + j, hs, :], p_ref[h, j * C:(j + 1) * C, :],
                                  preferred_element_type=F32)
            acc_ref[hs, :] = al_ref[h:h + 1, :] * acc_ref[hs, :] + pv

    n_far = jnp.maximum(i - 1, 0)

    def far_body(cp, carry):
        attend(2 * cp, None, span=FAR_SPAN)
        return carry

    lax.fori_loop(0, lax.shift_right_logical(n_far, 1), far_body, 0)

    @pl.when(n_far % 2 == 1)
    def _():
        attend(n_far - 1, None)

    @pl.when(i >= 1)
    def _():
        attend(i - 1, 1)

    attend(i, 0)

    out_t = jnp.concatenate(
        [acc_ref[h * V_SLOT:h * V_SLOT + HEAD_DIM, :] / acc_ref[h * V_SLOT + HEAD_DIM:h * V_SLOT + HEAD_DIM + 1, :]
         for h in range(ATTN_HEADS)], axis=0)
    o_ref[0] = out_t.T.astype(o_ref.dtype)


def _dsa_attention(qt, iqt, iwt, k, vt, ik, bias):
    B, S, _ = k.shape
    nq = S // CHUNK
    assert nq % 2 == 0
    topk = min(TOPK_MAX, S // 4)
    k4 = k.reshape(B, nq, CHUNK, ATTN_WIDTH)
    ik4 = ik.reshape(B, nq, CHUNK, IDX_DIM)
    return pl.pallas_call(
        functools.partial(_attn_kernel, topk=topk),
        out_shape=jax.ShapeDtypeStruct((B, S, ATTN_WIDTH), BF16),
        grid=(B, nq),
        in_specs=[
            pl.BlockSpec((1, ATTN_WIDTH, CHUNK), lambda b, i: (b, 0, i)),
            pl.BlockSpec((1, IDX_HEADS * IDX_DIM, CHUNK), lambda b, i: (b, 0, i)),
            pl.BlockSpec((1, IDX_HEADS, CHUNK), lambda b, i: (b, 0, i)),
            pl.BlockSpec((1, nq, CHUNK, ATTN_WIDTH), lambda b, i: (b, 0, 0, 0)),
            pl.BlockSpec((1, nq, VT_ROWS, CHUNK), lambda b, i: (b, 0, 0, 0)),
            pl.BlockSpec((1, nq, CHUNK, IDX_DIM), lambda b, i: (b, 0, 0, 0)),
            pl.BlockSpec((2, ATTN_HEADS, CHUNK, CHUNK), lambda b, i: (0, 0, 0, 0)),
        ],
        out_specs=pl.BlockSpec((1, CHUNK, ATTN_WIDTH), lambda b, i: (b, i, 0)),
        scratch_shapes=[
            pltpu.VMEM((nq, CHUNK, CHUNK), F32),
            pltpu.VMEM((8, CHUNK), F32),
            pltpu.VMEM((Q2_ROWS, CHUNK), BF16),
            pltpu.VMEM((FAR_SPAN * CHUNK, CHUNK), F32),
            pltpu.VMEM((ATTN_HEADS, FAR_SPAN * CHUNK, CHUNK), F32),
            pltpu.VMEM((ATTN_HEADS, FAR_SPAN * CHUNK, CHUNK), BF16),
            pltpu.VMEM((ATTN_HEADS, CHUNK), F32),
            pltpu.VMEM((ATTN_HEADS, CHUNK), F32),
            pltpu.VMEM((VT_ROWS, CHUNK), F32),
        ],
        compiler_params=pltpu.CompilerParams(
            dimension_semantics=("arbitrary", "arbitrary"), vmem_limit_bytes=VMEM_LIMIT),
        name="dsa_attn",
    )(qt, iqt, iwt, k4, vt, ik4, bias)


ROUTE_LANES = LANES
ROUTE_ROWS = 40


def _layer_norm(y, g, b):
    mu = jnp.mean(y, axis=-1, keepdims=True)
    yc = y - mu
    var = jnp.mean(yc * yc, axis=-1, keepdims=True)
    return yc * lax.rsqrt(var + LN_EPS) * g + b


def _mix_kernel(attn_ref, u_ref, halo_ref, x_ref, wpool_ref, pscale_ref, wout_ref, g_ref, b_ref,
                wr_ref, br_ref, h_ref, route_ref, rw_ref, *, tm):
    i = pl.program_id(1)
    halo = jnp.where(i > 0, halo_ref[0], 0.0)
    ue = jnp.concatenate([halo, u_ref[0]], axis=0)
    pos = i * tm + lax.broadcasted_iota(I32, (tm, 1), 0)
    mixed = []
    for g, w in enumerate(POOL_WINDOWS):
        gs = slice(g * POOL_GROUP_DIM, (g + 1) * POOL_GROUP_DIM)
        ch = ue[:, gs]
        win = ch
        step = 1
        while step < w:
            win = win + pltpu.roll(win, step, axis=0)
            step *= 2
        cnt = jnp.minimum(pos + 1, w).astype(F32)
        pooled = win[POOL_HALO:] / cnt - ch[POOL_HALO:]
        mg = jnp.dot(pooled.astype(BF16), wpool_ref[g], preferred_element_type=F32)
        mixed.append((mg * pscale_ref[:, gs]).astype(BF16))
    cat = jnp.concatenate([attn_ref[0]] + mixed, axis=-1)
    mix = jnp.dot(cat, wout_ref[...], preferred_element_type=F32)
    h = _layer_norm(DEEPNORM_ALPHA * x_ref[0] + mix, g_ref[...], b_ref[...])
    h_ref[0] = h
    hb = h.astype(BF16)

    lg = jnp.dot(hb, wr_ref[...], preferred_element_type=F32) + br_ref[...]
    lt = lg.T[0:ROUTE_ROWS, :]
    sub = lax.broadcasted_iota(I32, (ROUTE_ROWS, tm), 0).astype(F32)
    gl = jnp.where(sub >= N_EXPERTS, jnp.where(sub < N_EXPERTS + N_GROUPS, lt, -jnp.inf), -jnp.inf)
    ge = jnp.exp(gl - jnp.max(gl, axis=0, keepdims=True))
    pg = ge / jnp.sum(ge, axis=0, keepdims=True)
    pg_top = jnp.max(pg, axis=0, keepdims=True)
    g_row = jnp.min(jnp.where(pg == pg_top, sub, float(ROUTE_LANES)), axis=0, keepdims=True)
    e_lo = (g_row - N_EXPERTS) * EXPERTS_PER_GROUP
    fl = jnp.where(sub >= e_lo, jnp.where(sub < e_lo + EXPERTS_PER_GROUP, lt, -jnp.inf), -jnp.inf)
    fe = jnp.exp(fl - jnp.max(fl, axis=0, keepdims=True))
    pf = fe / jnp.sum(fe, axis=0, keepdims=True)
    p1 = jnp.max(pf, axis=0, keepdims=True)
    i1 = jnp.min(jnp.where(pf == p1, sub, float(ROUTE_LANES)), axis=0, keepdims=True)
    pr = jnp.where(sub == i1, -1.0, jnp.where(fl == -jnp.inf, -1.0, pf))
    p2 = jnp.max(pr, axis=0, keepdims=True)
    i2 = jnp.min(jnp.where(pr == p2, sub, float(ROUTE_LANES)), axis=0, keepdims=True)
    psum = p1 + p2
    w1 = pg_top * p1 / psum
    w2 = pg_top * p2 / psum
    row8 = lax.broadcasted_iota(I32, (8, tm), 0)
    route_ref[...] = jnp.where(row8 == 0, i1, jnp.where(row8 == 1, i2, 0.0)).astype(I32)
    rw_t = jnp.where(sub == 0.0, w1, jnp.where(sub == 1.0, w2, 0.0))
    rw_t = jnp.concatenate([rw_t, jnp.zeros((ROUTE_LANES - ROUTE_ROWS, tm), F32)], axis=0)
    rw_ref[0] = rw_t.T[:, 0:8]


def _mix_norm_route(attn, u, x, w_pool, pool_scale, w_out, ln_g, ln_b, w_r1, b_r1, w_r2, b_r2, *, tm=256):
    B, S, D = x.shape
    wr = jnp.concatenate([w_r2, w_r1, jnp.zeros((D, ROUTE_LANES - N_EXPERTS - N_GROUPS), w_r1.dtype)],
                         axis=1).astype(BF16)
    br = jnp.concatenate([b_r2, b_r1, jnp.zeros((ROUTE_LANES - N_EXPERTS - N_GROUPS,), b_r1.dtype)])[None, :]
    hpt = tm // POOL_HALO
    return pl.pallas_call(
        functools.partial(_mix_kernel, tm=tm),
        out_shape=(jax.ShapeDtypeStruct((B, S, D), F32),
                   jax.ShapeDtypeStruct((8, B * S), I32),
                   jax.ShapeDtypeStruct((B, S, 8), F32)),
        grid=(B, S // tm),
        in_specs=[
            pl.BlockSpec((1, tm, ATTN_WIDTH), lambda b, i: (b, i, 0)),
            pl.BlockSpec((1, tm, POOL_WIDTH), lambda b, i: (b, i, 0)),
            pl.BlockSpec((1, POOL_HALO, POOL_WIDTH), lambda b, i: (b, jnp.maximum(i * hpt - 1, 0), 0)),
            pl.BlockSpec((1, tm, D), lambda b, i: (b, i, 0)),
            pl.BlockSpec((len(POOL_WINDOWS), POOL_GROUP_DIM, POOL_GROUP_DIM), lambda b, i: (0, 0, 0)),
            pl.BlockSpec((1, POOL_WIDTH), lambda b, i: (0, 0)),
            pl.BlockSpec((D, D), lambda b, i: (0, 0)),
            pl.BlockSpec((1, D), lambda b, i: (0, 0)),
            pl.BlockSpec((1, D), lambda b, i: (0, 0)),
            pl.BlockSpec((D, ROUTE_LANES), lambda b, i: (0, 0)),
            pl.BlockSpec((1, ROUTE_LANES), lambda b, i: (0, 0)),
        ],
        out_specs=(pl.BlockSpec((1, tm, D), lambda b, i: (b, i, 0)),
                   pl.BlockSpec((8, tm), lambda b, i: (0, b * (S // tm) + i)),
                   pl.BlockSpec((1, tm, 8), lambda b, i: (b, i, 0))),
        compiler_params=pltpu.CompilerParams(
            dimension_semantics=("arbitrary", "arbitrary"), vmem_limit_bytes=VMEM_LIMIT),
        name="mix_norm",
    )(attn, u, u, x, w_pool.astype(BF16), pool_scale[None, :], w_out.astype(BF16),
      ln_g[None, :], ln_b[None, :], wr, br)


MOE_TM = 256
PLAN_TB = 1024
GATHER_UNROLL = 8


def _moe_rows(n_tokens):
    return 2 * n_tokens + N_EXPERTS * MOE_TM


def _plan_kernel(route_ref, pos_ref, tile_ref, cnt_ref, off_ref, carry_ref, *, n_tile_lanes):
    ph = pl.program_id(0)
    b = pl.program_id(1)
    tb = route_ref.shape[1]
    esub = lax.broadcasted_iota(I32, (N_EXPERTS, tb), 0)
    a1 = jnp.where(esub == route_ref[0:1, :], 1.0, 0.0)
    a2 = jnp.where(esub == route_ref[1:2, :], 1.0, 0.0)
    a = a1 + a2
    n_here = jnp.sum(a, axis=1, keepdims=True)

    @pl.when(jnp.logical_and(ph == 0, b == 0))
    def _():
        cnt_ref[...] = jnp.zeros(cnt_ref.shape, F32)

    @pl.when(ph == 0)
    def _():
        cnt_ref[...] += n_here

    @pl.when(jnp.logical_and(ph == 1, b == 0))
    def _():
        n_tile = jnp.floor((cnt_ref[...] + (MOE_TM - 1)) * (1.0 / MOE_TM))
        er = lax.broadcasted_iota(I32, (N_EXPERTS, N_EXPERTS), 0)
        ec = lax.broadcasted_iota(I32, (N_EXPERTS, N_EXPERTS), 1)
        before = jnp.where(ec < er, 1.0, 0.0).astype(BF16)
        t_off = jnp.dot(before, n_tile.astype(BF16), preferred_element_type=F32)
        off_ref[...] = t_off * MOE_TM
        carry_ref[...] = jnp.zeros(carry_ref.shape, F32)
        j = lax.broadcasted_iota(I32, (N_EXPERTS, n_tile_lanes), 1).astype(F32)
        owner = jnp.sum(jnp.where(t_off[:, 0:1] <= j, 1.0, 0.0), axis=0, keepdims=True) - 1.0
        total = jnp.sum(n_tile[:, 0:1], axis=0, keepdims=True)
        used = jnp.where(j[0:1, :] < total, 1.0, 0.0)
        row8 = lax.broadcasted_iota(I32, (8, n_tile_lanes), 0)
        tile_ref[...] = jnp.where(row8 == 0, owner, jnp.where(row8 == 1, used, 0.0)).astype(I32)

    @pl.when(ph == 1)
    def _():
        tr = lax.broadcasted_iota(I32, (tb, tb), 0)
        tc = lax.broadcasted_iota(I32, (tb, tb), 1)
        earlier = jnp.where(tr < tc, 1.0, 0.0).astype(BF16)
        seen = jnp.dot(a.astype(BF16), earlier, preferred_element_type=F32)
        dest = seen + carry_ref[:, 0:1] + off_ref[:, 0:1]
        p1 = jnp.sum(a1 * dest, axis=0, keepdims=True)
        p2 = jnp.sum(a2 * dest, axis=0, keepdims=True)
        row8 = lax.broadcasted_iota(I32, (8, tb), 0)
        pos_ref[...] = jnp.where(row8 == 0, p1, jnp.where(row8 == 1, p2, 0.0)).astype(I32)
        carry_ref[...] += n_here


def _moe_plan(route_t):
    _, T = route_t.shape
    n_tiles = _moe_rows(T) // MOE_TM
    n_tile_lanes = -(-n_tiles // LANES) * LANES
    nb = T // PLAN_TB
    return pl.pallas_call(
        functools.partial(_plan_kernel, n_tile_lanes=n_tile_lanes),
        out_shape=(jax.ShapeDtypeStruct((8, T), I32), jax.ShapeDtypeStruct((8, n_tile_lanes), I32)),
        grid=(2, nb),
        in_specs=[pl.BlockSpec((8, PLAN_TB), lambda ph, b: (0, b))],
        out_specs=(pl.BlockSpec((8, PLAN_TB), lambda ph, b: (0, b * ph)),
                   pl.BlockSpec((8, n_tile_lanes), lambda ph, b: (0, 0))),
        scratch_shapes=[pltpu.VMEM((N_EXPERTS, LANES), F32),
                        pltpu.VMEM((N_EXPERTS, LANES), F32),
                        pltpu.VMEM((N_EXPERTS, LANES), F32)],
        compiler_params=pltpu.CompilerParams(dimension_semantics=("arbitrary", "arbitrary")),
        name="moe_plan",
    )(route_t)


def _row_copy(src_ref, src_row, dst_ref, dst_row, sem):
    return pltpu.make_async_copy(src_ref.at[pl.ds(src_row, 1)], dst_ref.at[pl.ds(dst_row, 1)], sem)


def _dispatch_kernel(p1_ref, p2_ref, h_ref, init_hbm, out_hbm, sem, *, tb):
    del init_hbm

    def start(t, carry):
        _row_copy(h_ref, t, out_hbm, p1_ref[t], sem).start()
        _row_copy(h_ref, t, out_hbm, p2_ref[t], sem).start()
        return carry

    lax.fori_loop(0, tb, start, 0, unroll=GATHER_UNROLL)

    def wait(t, carry):
        _row_copy(h_ref, 0, out_hbm, 0, sem).wait()
        _row_copy(h_ref, 0, out_hbm, 0, sem).wait()
        return carry

    lax.fori_loop(0, tb, wait, 0, unroll=GATHER_UNROLL)


def _moe_dispatch(h, pos1, pos2, *, tb=512):
    T, D = h.shape
    rows = _moe_rows(T)
    return pl.pallas_call(
        functools.partial(_dispatch_kernel, tb=tb),
        out_shape=jax.ShapeDtypeStruct((rows, D), h.dtype),
        grid=(T // tb,),
        in_specs=[pl.BlockSpec((tb,), lambda i: (i,), memory_space=pltpu.SMEM),
                  pl.BlockSpec((tb,), lambda i: (i,), memory_space=pltpu.SMEM),
                  pl.BlockSpec((tb, D), lambda i: (i, 0)),
                  pl.BlockSpec(memory_space=pl.ANY)],
        out_specs=pl.BlockSpec(memory_space=pl.ANY),
        scratch_shapes=[pltpu.SemaphoreType.DMA(())],
        input_output_aliases={3: 0},
        compiler_params=pltpu.CompilerParams(dimension_semantics=("arbitrary",)),
        name="moe_dispatch",
    )(pos1, pos2, h, jnp.zeros((rows, D), h.dtype))


def _ffn_kernel(owner_ref, used_ref, x_ref, wg_ref, wu_ref, wd_ref, o_ref):
    j = pl.program_id(0)

    @pl.when(used_ref[j] == 1)
    def _():
        xb = x_ref[...].astype(BF16)
        gate = jnp.dot(xb, wg_ref[0].astype(BF16), preferred_element_type=F32)
        up = jnp.dot(xb, wu_ref[0].astype(BF16), preferred_element_type=F32)
        a = gate * jax.nn.sigmoid(gate) * up
        o_ref[...] = jnp.dot(a.astype(BF16), wd_ref[0].astype(BF16), preferred_element_type=F32)

    @pl.when(used_ref[j] == 0)
    def _():
        o_ref[...] = jnp.zeros(o_ref.shape, F32)


def _moe_ffn(xs, owner, used, w_gate, w_up, w_down):
    rows, D = xs.shape
    E, _, F = w_gate.shape
    return pl.pallas_call(
        _ffn_kernel,
        out_shape=jax.ShapeDtypeStruct((rows, D), F32),
        grid_spec=pltpu.PrefetchScalarGridSpec(
            num_scalar_prefetch=2,
            grid=(rows // MOE_TM,),
            in_specs=[pl.BlockSpec((MOE_TM, D), lambda j, ow, us: (j, 0)),
                      pl.BlockSpec((1, D, F), lambda j, ow, us: (ow[j], 0, 0)),
                      pl.BlockSpec((1, D, F), lambda j, ow, us: (ow[j], 0, 0)),
                      pl.BlockSpec((1, F, D), lambda j, ow, us: (ow[j], 0, 0))],
            out_specs=pl.BlockSpec((MOE_TM, D), lambda j, ow, us: (j, 0)),
        ),
        compiler_params=pltpu.CompilerParams(dimension_semantics=("arbitrary",), vmem_limit_bytes=VMEM_LIMIT),
        name="moe_ffn",
    )(owner, used, xs, w_gate, w_up, w_down)


def _combine_kernel(p1_ref, p2_ref, p1n_ref, p2n_ref, rw_ref, h_ref, g_ref, b_ref, y_hbm, o_ref,
                    buf_ref, sem, *, tm):
    s = pl.program_id(0)
    slot = s % 2

    def fetch(pa_ref, pb_ref, into):
        def start(t, carry):
            pltpu.make_async_copy(y_hbm.at[pl.ds(pa_ref[t], 1)], buf_ref.at[into, 0, pl.ds(t, 1)],
                                  sem.at[into]).start()
            pltpu.make_async_copy(y_hbm.at[pl.ds(pb_ref[t], 1)], buf_ref.at[into, 1, pl.ds(t, 1)],
                                  sem.at[into]).start()
            return carry
        lax.fori_loop(0, tm, start, 0, unroll=GATHER_UNROLL)

    @pl.when(s == 0)
    def _():
        fetch(p1_ref, p2_ref, 0)

    @pl.when(s + 1 < pl.num_programs(0))
    def _():
        fetch(p1n_ref, p2n_ref, 1 - slot)

    def wait(t, carry):
        for half in range(2):
            pltpu.make_async_copy(y_hbm.at[pl.ds(0, 1)], buf_ref.at[slot, half, pl.ds(0, 1)], sem.at[slot]).wait()
        return carry

    lax.fori_loop(0, tm, wait, 0, unroll=GATHER_UNROLL)
    y = rw_ref[:, 0:1] * buf_ref[slot, 0] + rw_ref[:, 1:2] * buf_ref[slot, 1]
    o_ref[...] = _layer_norm(DEEPNORM_ALPHA * h_ref[...] + y, g_ref[...], b_ref[...])


def _moe_combine(h, y_sorted, pos1, pos2, rw, ln_g, ln_b, *, tm=256):
    T, D = h.shape
    last = T // tm - 1
    smem = lambda imap: pl.BlockSpec((tm,), imap, memory_space=pltpu.SMEM)
    return pl.pallas_call(
        functools.partial(_combine_kernel, tm=tm),
        out_shape=jax.ShapeDtypeStruct((T, D), F32),
        grid=(T // tm,),
        in_specs=[smem(lambda i: (i,)), smem(lambda i: (i,)),
                  smem(lambda i: (jnp.minimum(i + 1, last),)), smem(lambda i: (jnp.minimum(i + 1, last),)),
                  pl.BlockSpec((tm, 8), lambda i: (i, 0)),
                  pl.BlockSpec((tm, D), lambda i: (i, 0)),
                  pl.BlockSpec((1, D), lambda i: (0, 0)),
                  pl.BlockSpec((1, D), lambda i: (0, 0)),
                  pl.BlockSpec(memory_space=pl.ANY)],
        out_specs=pl.BlockSpec((tm, D), lambda i: (i, 0)),
        scratch_shapes=[pltpu.VMEM((2, 2, tm, D), F32), pltpu.SemaphoreType.DMA((2,))],
        compiler_params=pltpu.CompilerParams(dimension_semantics=("arbitrary",), vmem_limit_bytes=VMEM_LIMIT),
        name="moe_combine",
    )(pos1, pos2, pos1, pos2, rw, h, ln_g[None, :], ln_b[None, :], y_sorted)


def _moe_norm(h, route_t, rw, w_gate, w_up, w_down, ln_g, ln_b):
    pos, tiles = _moe_plan(route_t)
    xs = _moe_dispatch(h, pos[0], pos[1])
    n_tiles = xs.shape[0] // MOE_TM
    ys = _moe_ffn(xs, tiles[0, :n_tiles], tiles[1, :n_tiles], w_gate, w_up, w_down)
    return _moe_combine(h, ys, pos[0], pos[1], rw, ln_g, ln_b)


def kernel(x, w_in, w_pool, pool_scale, w_out, rel_bias, ln1_g, ln1_b, w_r1, b_r1, w_r2, b_r2,
           w_gate, w_up, w_down, ln2_g, ln2_b):
    B, S, D = x.shape
    assert w_in.shape[0] == DEPTH == 1 and S % CHUNK == 0
    bias = _bias_tiles(rel_bias)
    qt, vt, iqt, iwt, k, ik, u = _in_proj(x, w_in[0])
    attn = _dsa_attention(qt, iqt, iwt, k, vt, ik, bias)
    h, route_t, rw = _mix_norm_route(attn, u, x, w_pool[0], pool_scale[0], w_out[0], ln1_g[0], ln1_b[0],
                                     w_r1[0], b_r1[0], w_r2[0], b_r2[0])
    out = _moe_norm(h.reshape(B * S, D), route_t, rw.reshape(B * S, 8),
                    w_gate[0], w_up[0], w_down[0], ln2_g[0], ln2_b[0])
    return out.reshape(B, S, D)
```

```python
import functools
import math

import numpy as np
import jax
import jax.numpy as jnp
from jax import lax
from jax.experimental import pallas as pl
from jax.experimental.pallas import tpu as pltpu

F32 = jnp.float32
BF16 = jnp.bfloat16
I32 = jnp.int32

ATTN_HEADS = 8
HEAD_DIM = 64
ATTN_WIDTH = ATTN_HEADS * HEAD_DIM
IDX_HEADS = 8
IDX_DIM = 64
TOPK_MAX = 256
POOL_WINDOWS = (2, 4, 8, 16)
POOL_GROUP_DIM = 128
POOL_WIDTH = len(POOL_WINDOWS) * POOL_GROUP_DIM
POOL_HALO = 16
REL_BUCKETS = 32
REL_MAX_DIST = 128
N_GROUPS = 4
EXPERTS_PER_GROUP = 8
N_EXPERTS = N_GROUPS * EXPERTS_PER_GROUP
LN_EPS = 1e-5
DEPTH = 1
DEEPNORM_ALPHA = (2 * DEPTH) ** 0.25
LOG2E = math.log2(math.e)

LANES = 128
CHUNK = 256
INT_MIN = -2 ** 31
NEG_BIG = -1e30
VMEM_LIMIT = 56 * 1024 * 1024


def _rel_bucket_table(n):
    max_exact = REL_BUCKETS // 2
    d = np.arange(n)
    nf = np.maximum(d, 1).astype(np.float32)
    ratio = np.log(nf / np.float32(max_exact)) / np.float32(math.log(REL_MAX_DIST / max_exact))
    large = max_exact + (ratio * np.float32(REL_BUCKETS - max_exact)).astype(np.int32)
    large = np.minimum(large, REL_BUCKETS - 1)
    return np.where(d < max_exact, d, large).astype(np.int32)


def _near_bucket_tiles():
    tbl = _rel_bucket_table(2 * CHUNK)
    a = np.arange(CHUNK)[:, None]
    b = np.arange(CHUNK)[None, :]
    tiles = [tbl[np.maximum(delta * CHUNK + b - a, 0)] for delta in (0, 1)]
    return np.stack(tiles).astype(np.int32)


FAR_BUCKET = REL_BUCKETS - 1
assert int(_rel_bucket_table(2 * CHUNK)[CHUNK + 1:].min()) == FAR_BUCKET


def _bias_kernel(rb_ref, bucket_ref, o_ref):
    h = pl.program_id(1)
    bk = bucket_ref[0]
    far = rb_ref[FAR_BUCKET, h]
    acc = jnp.zeros(bk.shape, F32)
    for n in range(REL_BUCKETS):
        acc = jnp.where(bk == n, rb_ref[n, h] - far, acc)
    o_ref[0, 0] = acc * LOG2E


def _bias_tiles(rel_bias):
    buckets = jnp.asarray(_near_bucket_tiles())
    return pl.pallas_call(
        _bias_kernel,
        out_shape=jax.ShapeDtypeStruct((2, ATTN_HEADS, CHUNK, CHUNK), F32),
        grid=(2, ATTN_HEADS),
        in_specs=[pl.BlockSpec(memory_space=pltpu.SMEM),
                  pl.BlockSpec((1, CHUNK, CHUNK), lambda d, h: (d, 0, 0))],
        out_specs=pl.BlockSpec((1, 1, CHUNK, CHUNK), lambda d, h: (d, h, 0, 0)),
        name="bias_tiles",
    )(rel_bias, buckets)


V_SLOT = HEAD_DIM + 16
VT_ROWS = ATTN_HEADS * V_SLOT
WT_Q, WT_V, WT_IQ, WT_IW = 0, ATTN_WIDTH, ATTN_WIDTH + VT_ROWS, 2 * ATTN_WIDTH + VT_ROWS
WT_ROWS = WT_IW + 16
WN_K, WN_U, WN_IK = 0, ATTN_WIDTH, ATTN_WIDTH + POOL_WIDTH
WN_COLS = WN_IK + LANES


def _proj_kernel(x_ref, wt_ref, wn_ref, qt_ref, vt_ref, iqt_ref, iwt_ref, k_ref, ik_ref, u_ref, *, tm):
    xb = x_ref[0].astype(BF16)
    t = lax.dot_general(wt_ref[...], xb, (((1,), (1,)), ((), ())),
                        preferred_element_type=F32)
    qt_ref[0] = (t[WT_Q:WT_V] * (HEAD_DIM ** -0.5 * LOG2E)).astype(BF16)
    iqt_ref[0] = (t[WT_IQ:WT_IW] * (IDX_DIM ** -0.5)).astype(BF16)
    iwt_ref[0] = t[WT_IW:WT_IW + IDX_HEADS] * (IDX_HEADS ** -0.5)
    slot_row = lax.broadcasted_iota(I32, (VT_ROWS, tm), 0) % V_SLOT
    vt = jnp.where(slot_row >= HEAD_DIM, 1.0, t[WT_V:WT_IQ]).astype(BF16)
    for j in range(tm // CHUNK):
        vt_ref[0, j] = vt[:, j * CHUNK:(j + 1) * CHUNK]
    n = jnp.dot(xb, wn_ref[...], preferred_element_type=F32)
    k_ref[0] = n[:, WN_K:WN_U].astype(BF16)
    u_ref[0] = n[:, WN_U:WN_IK]
    ik_ref[0] = n[:, WN_IK:WN_IK + IDX_DIM].astype(BF16)


def _in_proj(x, w_in, *, tm=512):
    B, S, D = x.shape
    q_off, k_off, v_off = 0, ATTN_WIDTH, 2 * ATTN_WIDTH
    iq_off = 3 * ATTN_WIDTH
    ik_off = iq_off + IDX_HEADS * IDX_DIM
    iw_off = ik_off + IDX_DIM
    p_off = iw_off + IDX_HEADS
    wv = w_in[:, v_off:iq_off].T.reshape(ATTN_HEADS, HEAD_DIM, D)
    wv = jnp.pad(wv, ((0, 0), (0, V_SLOT - HEAD_DIM), (0, 0))).reshape(VT_ROWS, D)
    wt = jnp.concatenate([
        w_in[:, q_off:k_off].T, wv, w_in[:, iq_off:ik_off].T, w_in[:, iw_off:p_off].T,
        jnp.zeros((WT_ROWS - WT_IW - IDX_HEADS, D), w_in.dtype)], axis=0).astype(BF16)
    wn = jnp.concatenate([
        w_in[:, k_off:v_off], w_in[:, p_off:], w_in[:, ik_off:iw_off],
        jnp.zeros((D, WN_COLS - WN_IK - IDX_DIM), w_in.dtype)], axis=1).astype(BF16)
    nt = S // tm
    cpt = tm // CHUNK
    outs = pl.pallas_call(
        functools.partial(_proj_kernel, tm=tm),
        out_shape=(
            jax.ShapeDtypeStruct((B, ATTN_WIDTH, S), BF16),
            jax.ShapeDtypeStruct((B, S // CHUNK, VT_ROWS, CHUNK), BF16),
            jax.ShapeDtypeStruct((B, IDX_HEADS * IDX_DIM, S), BF16),
            jax.ShapeDtypeStruct((B, IDX_HEADS, S), F32),
            jax.ShapeDtypeStruct((B, S, ATTN_WIDTH), BF16),
            jax.ShapeDtypeStruct((B, S, IDX_DIM), BF16),
            jax.ShapeDtypeStruct((B, S, POOL_WIDTH), F32),
        ),
        grid=(B, nt),
        in_specs=[pl.BlockSpec((1, tm, D), lambda b, i: (b, i, 0)),
                  pl.BlockSpec((WT_ROWS, D), lambda b, i: (0, 0)),
                  pl.BlockSpec((D, WN_COLS), lambda b, i: (0, 0))],
        out_specs=(
            pl.BlockSpec((1, ATTN_WIDTH, tm), lambda b, i: (b, 0, i)),
            pl.BlockSpec((1, cpt, VT_ROWS, CHUNK), lambda b, i: (b, i, 0, 0)),
            pl.BlockSpec((1, IDX_HEADS * IDX_DIM, tm), lambda b, i: (b, 0, i)),
            pl.BlockSpec((1, IDX_HEADS, tm), lambda b, i: (b, 0, i)),
            pl.BlockSpec((1, tm, ATTN_WIDTH), lambda b, i: (b, i, 0)),
            pl.BlockSpec((1, tm, IDX_DIM), lambda b, i: (b, i, 0)),
            pl.BlockSpec((1, tm, POOL_WIDTH), lambda b, i: (b, i, 0)),
        ),
        compiler_params=pltpu.CompilerParams(
            dimension_semantics=("arbitrary", "arbitrary"), vmem_limit_bytes=VMEM_LIMIT),
        name="in_proj",
    )(x, wt, wn)
    return outs


SNAP_ROUND = 7
SNAP_PERIOD = 3
WARM_ROUNDS = 6
MAX_ROUNDS = SNAP_ROUND + SNAP_PERIOD * 17
ROW_LO, ROW_HI, ROW_CNT, ROW_PROBE = 0, 1, 2, 3
FAR_SPAN = 2


def _order_key(bits):
    return bits ^ ((bits >> 31) & 0x7FFFFFFF)


Q2_ROWS = 2 * ATTN_WIDTH


def _attn_kernel(qt_ref, iqt_ref, iwt_ref, k_ref, vt_ref, ik_ref, bias_ref, o_ref,
                 sc_ref, st_ref, q2t_ref, neg_ref, lg_ref, p_ref, m_ref, al_ref, acc_ref,
                 *, topk):
    i = pl.program_id(1)
    nch = i + 1
    C = CHUNK

    @pl.when(jnp.logical_and(pl.program_id(0) == 0, i == 0))
    def _():
        q2t_ref[...] = jnp.zeros(q2t_ref.shape, BF16)

    for h in range(ATTN_HEADS):
        lo = 2 * HEAD_DIM * h + HEAD_DIM * (h % 2)
        q2t_ref[lo:lo + HEAD_DIM, :] = qt_ref[0, h * HEAD_DIM:(h + 1) * HEAD_DIM, :]
    row = lax.broadcasted_iota(I32, (C, C), 0)
    col = lax.broadcasted_iota(I32, (C, C), 1)

    def fold_rows(op, x):
        return op(x.reshape(C // 8, 8, C), axis=0)

    n_pair = lax.shift_right_logical(nch + 1, 1)

    def score_body(cp, carry):
        smin, smax = carry
        for c in (2 * cp, 2 * cp + 1):
            ikc = ik_ref[0, c]
            s = jnp.zeros((C, C), F32)
            for j in range(IDX_HEADS):
                d = jnp.dot(ikc, iqt_ref[0, j * IDX_DIM:(j + 1) * IDX_DIM, :],
                            preferred_element_type=F32)
                s = s + iwt_ref[0, j:j + 1, :] * jnp.maximum(d, 0.0)
            causal = (c * C + row) <= (i * C + col)
            sc_ref[c] = jnp.where(causal, s, -jnp.inf)
            smin = jnp.minimum(smin, fold_rows(jnp.min, jnp.where(causal, s, jnp.inf)))
            smax = jnp.maximum(smax, fold_rows(jnp.max, jnp.where(causal, s, -jnp.inf)))
        return smin, smax

    smin, smax = lax.fori_loop(0, n_pair, score_body,
                               (jnp.full((8, C), jnp.inf, F32), jnp.full((8, C), -jnp.inf, F32)))
    smin = jnp.min(smin, axis=0, keepdims=True)
    smax = jnp.max(smax, axis=0, keepdims=True)

    pos = i * C + lax.broadcasted_iota(I32, (1, C), 1)
    n_keys = (pos + 1).astype(F32)
    k_eff = jnp.minimum(pos + 1, topk).astype(F32)

    def count_ge(mid):
        def body(cp, acc):
            for c in (2 * cp, 2 * cp + 1):
                acc = acc + fold_rows(jnp.sum, jnp.where(sc_ref[c] >= mid, 1.0, 0.0))
            return acc
        part = lax.fori_loop(0, n_pair, body, jnp.zeros((8, C), F32))
        return jnp.sum(part, axis=0, keepdims=True)

    def load_state():
        return (st_ref[ROW_LO:ROW_LO + 1, :], st_ref[ROW_HI:ROW_HI + 1, :],
                st_ref[ROW_CNT:ROW_CNT + 1, :], st_ref[ROW_PROBE:ROW_PROBE + 1, :])

    def store_state(lo, hi, cnt_lo, probe):
        st_ref[ROW_LO:ROW_LO + 1, :] = lo
        st_ref[ROW_HI:ROW_HI + 1, :] = hi
        st_ref[ROW_CNT:ROW_CNT + 1, :] = cnt_lo
        st_ref[ROW_PROBE:ROW_PROBE + 1, :] = probe

    def open_cols(lo, hi, cnt_lo):
        return jnp.where(cnt_lo > k_eff, jnp.where(hi > lo, 1.0, 0.0), 0.0)

    def bisect(halve_image, lo, hi, cnt_lo, probe):
        klo = _order_key(pltpu.bitcast(lo, I32))
        khi = _order_key(pltpu.bitcast(hi, I32))
        i_mid = pltpu.bitcast(_order_key((klo >> 1) + (khi >> 1) + (klo & khi & 1)), F32)
        mid = jnp.where(halve_image, i_mid, 0.5 * lo + 0.5 * hi)
        probing = probe > lo
        mid = jnp.where(probing, probe, mid)
        ok = open_cols(lo, hi, cnt_lo) * jnp.where(mid > lo, jnp.where(mid < hi, 1.0, 0.0), 0.0)
        cnt = count_ge(mid)
        up = ok * jnp.where(cnt >= k_eff, 1.0, 0.0)
        dn = ok - up
        new_hi = jnp.where(dn + up * jnp.where(probing, 1.0, 0.0) > 0.0, mid, hi)
        return (jnp.where(up > 0.0, mid, lo), new_hi, jnp.where(up > 0.0, cnt, cnt_lo),
                jnp.full((1, C), -jnp.inf, F32))

    def snap(lo, hi, cnt_lo, probe):
        def body(c, carry):
            vmin, vmax = carry
            sc = sc_ref[c]
            vmin = jnp.minimum(vmin, fold_rows(jnp.min, jnp.where(sc >= lo, sc, jnp.inf)))
            vmax = jnp.maximum(vmax, fold_rows(jnp.max, jnp.where(sc < hi, sc, -jnp.inf)))
            return vmin, vmax
        vmin, vmax = lax.fori_loop(0, nch, body, (jnp.full((8, C), jnp.inf, F32),
                                                  jnp.full((8, C), -jnp.inf, F32)))
        vmin = jnp.min(vmin, axis=0, keepdims=True)
        vmax = jnp.max(vmax, axis=0, keepdims=True)
        is_open = open_cols(lo, hi, cnt_lo) > 0.0
        single = vmin >= vmax
        return (jnp.where(is_open, vmin, lo),
                jnp.where(is_open, jnp.where(single, vmin, hi), hi),
                cnt_lo,
                jnp.where(is_open, jnp.where(single, -jnp.inf, vmax), -jnp.inf))

    def search_cond(st):
        rnd, n_open = st
        return jnp.logical_and(rnd < MAX_ROUNDS, n_open > 0.0)

    def search_body(st):
        rnd, _ = st
        late = rnd >= SNAP_ROUND
        do_snap = jnp.logical_and(late, (rnd - SNAP_ROUND) % SNAP_PERIOD == 0)

        @pl.when(do_snap)
        def _():
            store_state(*snap(*load_state()))

        @pl.when(jnp.logical_not(do_snap))
        def _():
            store_state(*bisect(False, *load_state()))

        store_state(*bisect(late, *load_state()))
        lo, hi, cnt_lo, _ = load_state()
        return rnd + 1, jnp.max(open_cols(lo, hi, cnt_lo))

    kmax = _order_key(pltpu.bitcast(smax, I32))
    hi0 = pltpu.bitcast(_order_key(jnp.where(kmax == 2 ** 31 - 1, kmax, kmax + 1)), F32)
    store_state(smin, hi0, n_keys, jnp.full((1, C), -jnp.inf, F32))

    @pl.when(jnp.max(open_cols(smin, hi0, n_keys)) > 0.0)
    def _():
        def warm(r, carry):
            store_state(*bisect(False, *load_state()))
            store_state(*bisect(False, *load_state()))
            return carry
        lax.fori_loop(0, WARM_ROUNDS, warm, 0)

    lo_w, hi_w, cnt_w, _ = load_state()
    lax.while_loop(search_cond, search_body, (jnp.int32(WARM_ROUNDS), jnp.max(open_cols(lo_w, hi_w, cnt_w))))
    th, _, cnt_th, _ = load_state()

    surplus = cnt_th - k_eff

    @pl.when(jnp.max(surplus) > 0.0)
    def _():
        later = jnp.where(col > row, 1.0, 0.0).astype(BF16)

        def drop_body(r, after):
            c = nch - 1 - r
            sc = sc_ref[c]
            tied = jnp.where(sc == th, 1.0, 0.0)
            follow = jnp.dot(later, tied.astype(BF16), preferred_element_type=F32) + after
            sc_ref[c] = jnp.where(tied * jnp.where(follow < surplus, 1.0, 0.0) > 0.0, -jnp.inf, sc)
            return after + jnp.sum(fold_rows(jnp.sum, tied), axis=0, keepdims=True)

        lax.fori_loop(0, nch, drop_body, jnp.zeros((1, C), F32))

    m_ref[...] = jnp.full(m_ref.shape, NEG_BIG, F32)
    acc_ref[...] = jnp.zeros(acc_ref.shape, F32)

    def attend(c, near, span=1):
        n = span * C
        neg = jnp.where(sc_ref[pl.ds(c, span)].reshape(n, C) >= th, 0.0, -jnp.inf)
        if near == 0:
            neg = jnp.where(row <= col, neg, -jnp.inf)
        neg_ref[0:n, :] = neg
        for h in range(ATTN_HEADS):
            p2 = (h // 2) * 2 * HEAD_DIM
            lg = jnp.dot(k_ref[0, pl.ds(c, span), :, p2:p2 + 2 * HEAD_DIM].reshape(n, 2 * HEAD_DIM),
                         q2t_ref[2 * HEAD_DIM * h:2 * HEAD_DIM * (h + 1), :],
                         preferred_element_type=F32)
            if near is not None:
                lg = lg + bias_ref[near, h]
            lg = lg + neg_ref[0:n, :]
            lg_ref[h, 0:n, :] = lg
            m_old = m_ref[h:h + 1, :]
            m_new = jnp.maximum(m_old, jnp.max(lg, axis=0, keepdims=True))
            al_ref[h:h + 1, :] = jnp.exp2(m_old - m_new)
            m_ref[h:h + 1, :] = m_new
        for h in range(ATTN_HEADS):
            p_ref[h, 0:n, :] = jnp.exp2(lg_ref[h, 0:n, :] - m_ref[h:h + 1, :]).astype(BF16)
        for h in range(ATTN_HEADS):
            hs = slice(h * V_SLOT, (h + 1) * V_SLOT)
            pv = jnp.dot(vt_ref[0, c, hs, :], p_ref[h, 0:C, :], preferred_element_type=F32)
            for j in range(1, span):
                pv = pv + jnp.dot(vt_ref[0, c + j, hs, :], p_ref[h, j * C:(j + 1) * C, :],
                                  preferred_element_type=F32)
            acc_ref[hs, :] = al_ref[h:h + 1, :] * acc_ref[hs, :] + pv

    n_far = jnp.maximum(i - 1, 0)

    def far_body(cp, carry):
        attend(2 * cp, None, span=FAR_SPAN)
        return carry

    lax.fori_loop(0, lax.shift_right_logical(n_far, 1), far_body, 0)

    @pl.when(n_far % 2 == 1)
    def _():
        attend(n_far - 1, None)

    @pl.when(i >= 1)
    def _():
        attend(i - 1, 1)

    attend(i, 0)

    out_t = jnp.concatenate(
        [acc_ref[h * V_SLOT:h * V_SLOT + HEAD_DIM, :] / acc_ref[h * V_SLOT + HEAD_DIM:h * V_SLOT + HEAD_DIM + 1, :]
         for h in range(ATTN_HEADS)], axis=0)
    o_ref[0] = out_t.T.astype(o_ref.dtype)


def _dsa_attention(qt, iqt, iwt, k, vt, ik, bias):
    B, S, _ = k.shape
    nq = S // CHUNK
    assert nq % 2 == 0
    topk = min(TOPK_MAX, S // 4)
    k4 = k.reshape(B, nq, CHUNK, ATTN_WIDTH)
    ik4 = ik.reshape(B, nq, CHUNK, IDX_DIM)
    return pl.pallas_call(
        functools.partial(_attn_kernel, topk=topk),
        out_shape=jax.ShapeDtypeStruct((B, S, ATTN_WIDTH), BF16),
        grid=(B, nq),
        in_specs=[
            pl.BlockSpec((1, ATTN_WIDTH, CHUNK), lambda b, i: (b, 0, i)),
            pl.BlockSpec((1, IDX_HEADS * IDX_DIM, CHUNK), lambda b, i: (b, 0, i)),
            pl.BlockSpec((1, IDX_HEADS, CHUNK), lambda b, i: (b, 0, i)),
            pl.BlockSpec((1, nq, CHUNK, ATTN_WIDTH), lambda b, i: (b, 0, 0, 0)),
            pl.BlockSpec((1, nq, VT_ROWS, CHUNK), lambda b, i: (b, 0, 0, 0)),
            pl.BlockSpec((1, nq, CHUNK, IDX_DIM), lambda b, i: (b, 0, 0, 0)),
            pl.BlockSpec((2, ATTN_HEADS, CHUNK, CHUNK), lambda b, i: (0, 0, 0, 0)),
        ],
        out_specs=pl.BlockSpec((1, CHUNK, ATTN_WIDTH), lambda b, i: (b, i, 0)),
        scratch_shapes=[
            pltpu.VMEM((nq, CHUNK, CHUNK), F32),
            pltpu.VMEM((8, CHUNK), F32),
            pltpu.VMEM((Q2_ROWS, CHUNK), BF16),
            pltpu.VMEM((FAR_SPAN * CHUNK, CHUNK), F32),
            pltpu.VMEM((ATTN_HEADS, FAR_SPAN * CHUNK, CHUNK), F32),
            pltpu.VMEM((ATTN_HEADS, FAR_SPAN * CHUNK, CHUNK), BF16),
            pltpu.VMEM((ATTN_HEADS, CHUNK), F32),
            pltpu.VMEM((ATTN_HEADS, CHUNK), F32),
            pltpu.VMEM((VT_ROWS, CHUNK), F32),
        ],
        compiler_params=pltpu.CompilerParams(
            dimension_semantics=("arbitrary", "arbitrary"), vmem_limit_bytes=VMEM_LIMIT),
        name="dsa_attn",
    )(qt, iqt, iwt, k4, vt, ik4, bias)


ROUTE_LANES = LANES
ROUTE_ROWS = 40


def _layer_norm(y, g, b):
    mu = jnp.mean(y, axis=-1, keepdims=True)
    yc = y - mu
    var = jnp.mean(yc * yc, axis=-1, keepdims=True)
    return yc * lax.rsqrt(var + LN_EPS) * g + b


def _mix_kernel(attn_ref, u_ref, halo_ref, x_ref, wpool_ref, pscale_ref, wout_ref, g_ref, b_ref,
                wr_ref, br_ref, h_ref, route_ref, rw_ref, *, tm):
    i = pl.program_id(1)
    halo = jnp.where(i > 0, halo_ref[0], 0.0)
    ue = jnp.concatenate([halo, u_ref[0]], axis=0)
    pos = i * tm + lax.broadcasted_iota(I32, (tm, 1), 0)
    mixed = []
    for g, w in enumerate(POOL_WINDOWS):
        gs = slice(g * POOL_GROUP_DIM, (g + 1) * POOL_GROUP_DIM)
        ch = ue[:, gs]
        win = ch
        step = 1
        while step < w:
            win = win + pltpu.roll(win, step, axis=0)
            step *= 2
        cnt = jnp.minimum(pos + 1, w).astype(F32)
        pooled = win[POOL_HALO:] / cnt - ch[POOL_HALO:]
        mg = jnp.dot(pooled.astype(BF16), wpool_ref[g], preferred_element_type=F32)
        mixed.append((mg * pscale_ref[:, gs]).astype(BF16))
    cat = jnp.concatenate([attn_ref[0]] + mixed, axis=-1)
    mix = jnp.dot(cat, wout_ref[...], preferred_element_type=F32)
    h = _layer_norm(DEEPNORM_ALPHA * x_ref[0] + mix, g_ref[...], b_ref[...])
    h_ref[0] = h
    hb = h.astype(BF16)

    lg = jnp.dot(hb, wr_ref[...], preferred_element_type=F32) + br_ref[...]
    lt = lg.T[0:ROUTE_ROWS, :]
    sub = lax.broadcasted_iota(I32, (ROUTE_ROWS, tm), 0).astype(F32)
    gl = jnp.where(sub >= N_EXPERTS, jnp.where(sub < N_EXPERTS + N_GROUPS, lt, -jnp.inf), -jnp.inf)
    ge = jnp.exp(gl - jnp.max(gl, axis=0, keepdims=True))
    pg = ge / jnp.sum(ge, axis=0, keepdims=True)
    pg_top = jnp.max(pg, axis=0, keepdims=True)
    g_row = jnp.min(jnp.where(pg == pg_top, sub, float(ROUTE_LANES)), axis=0, keepdims=True)
    e_lo = (g_row - N_EXPERTS) * EXPERTS_PER_GROUP
    fl = jnp.where(sub >= e_lo, jnp.where(sub < e_lo + EXPERTS_PER_GROUP, lt, -jnp.inf), -jnp.inf)
    fe = jnp.exp(fl - jnp.max(fl, axis=0, keepdims=True))
    pf = fe / jnp.sum(fe, axis=0, keepdims=True)
    p1 = jnp.max(pf, axis=0, keepdims=True)
    i1 = jnp.min(jnp.where(pf == p1, sub, float(ROUTE_LANES)), axis=0, keepdims=True)
    pr = jnp.where(sub == i1, -1.0, jnp.where(fl == -jnp.inf, -1.0, pf))
    p2 = jnp.max(pr, axis=0, keepdims=True)
    i2 = jnp.min(jnp.where(pr == p2, sub, float(ROUTE_LANES)), axis=0, keepdims=True)
    psum = p1 + p2
    w1 = pg_top * p1 / psum
    w2 = pg_top * p2 / psum
    row8 = lax.broadcasted_iota(I32, (8, tm), 0)
    route_ref[...] = jnp.where(row8 == 0, i1, jnp.where(row8 == 1, i2, 0.0)).astype(I32)
    rw_t = jnp.where(sub == 0.0, w1, jnp.where(sub == 1.0, w2, 0.0))
    rw_t = jnp.concatenate([rw_t, jnp.zeros((ROUTE_LANES - ROUTE_ROWS, tm), F32)], axis=0)
    rw_ref[0] = rw_t.T[:, 0:8]


def _mix_norm_route(attn, u, x, w_pool, pool_scale, w_out, ln_g, ln_b, w_r1, b_r1, w_r2, b_r2, *, tm=256):
    B, S, D = x.shape
    wr = jnp.concatenate([w_r2, w_r1, jnp.zeros((D, ROUTE_LANES - N_EXPERTS - N_GROUPS), w_r1.dtype)],
                         axis=1).astype(BF16)
    br = jnp.concatenate([b_r2, b_r1, jnp.zeros((ROUTE_LANES - N_EXPERTS - N_GROUPS,), b_r1.dtype)])[None, :]
    hpt = tm // POOL_HALO
    return pl.pallas_call(
        functools.partial(_mix_kernel, tm=tm),
        out_shape=(jax.ShapeDtypeStruct((B, S, D), F32),
                   jax.ShapeDtypeStruct((8, B * S), I32),
                   jax.ShapeDtypeStruct((B, S, 8), F32)),
        grid=(B, S // tm),
        in_specs=[
            pl.BlockSpec((1, tm, ATTN_WIDTH), lambda b, i: (b, i, 0)),
            pl.BlockSpec((1, tm, POOL_WIDTH), lambda b, i: (b, i, 0)),
            pl.BlockSpec((1, POOL_HALO, POOL_WIDTH), lambda b, i: (b, jnp.maximum(i * hpt - 1, 0), 0)),
            pl.BlockSpec((1, tm, D), lambda b, i: (b, i, 0)),
            pl.BlockSpec((len(POOL_WINDOWS), POOL_GROUP_DIM, POOL_GROUP_DIM), lambda b, i: (0, 0, 0)),
            pl.BlockSpec((1, POOL_WIDTH), lambda b, i: (0, 0)),
            pl.BlockSpec((D, D), lambda b, i: (0, 0)),
            pl.BlockSpec((1, D), lambda b, i: (0, 0)),
            pl.BlockSpec((1, D), lambda b, i: (0, 0)),
            pl.BlockSpec((D, ROUTE_LANES), lambda b, i: (0, 0)),
            pl.BlockSpec((1, ROUTE_LANES), lambda b, i: (0, 0)),
        ],
        out_specs=(pl.BlockSpec((1, tm, D), lambda b, i: (b, i, 0)),
                   pl.BlockSpec((8, tm), lambda b, i: (0, b * (S // tm) + i)),
                   pl.BlockSpec((1, tm, 8), lambda b, i: (b, i, 0))),
        compiler_params=pltpu.CompilerParams(
            dimension_semantics=("arbitrary", "arbitrary"), vmem_limit_bytes=VMEM_LIMIT),
        name="mix_norm",
    )(attn, u, u, x, w_pool.astype(BF16), pool_scale[None, :], w_out.astype(BF16),
      ln_g[None, :], ln_b[None, :], wr, br)


MOE_TM = 256
PLAN_TB = 1024
GATHER_UNROLL = 8


def _moe_rows(n_tokens):
    return 2 * n_tokens + N_EXPERTS * MOE_TM


def _plan_kernel(route_ref, pos_ref, tile_ref, cnt_ref, off_ref, carry_ref, *, n_tile_lanes):
    ph = pl.program_id(0)
    b = pl.program_id(1)
    tb = route_ref.shape[1]
    esub = lax.broadcasted_iota(I32, (N_EXPERTS, tb), 0)
    a1 = jnp.where(esub == route_ref[0:1, :], 1.0, 0.0)
    a2 = jnp.where(esub == route_ref[1:2, :], 1.0, 0.0)
    a = a1 + a2
    n_here = jnp.sum(a, axis=1, keepdims=True)

    @pl.when(jnp.logical_and(ph == 0, b == 0))
    def _():
        cnt_ref[...] = jnp.zeros(cnt_ref.shape, F32)

    @pl.when(ph == 0)
    def _():
        cnt_ref[...] += n_here

    @pl.when(jnp.logical_and(ph == 1, b == 0))
    def _():
        n_tile = jnp.floor((cnt_ref[...] + (MOE_TM - 1)) * (1.0 / MOE_TM))
        er = lax.broadcasted_iota(I32, (N_EXPERTS, N_EXPERTS), 0)
        ec = lax.broadcasted_iota(I32, (N_EXPERTS, N_EXPERTS), 1)
        before = jnp.where(ec < er, 1.0, 0.0).astype(BF16)
        t_off = jnp.dot(before, n_tile.astype(BF16), preferred_element_type=F32)
        off_ref[...] = t_off * MOE_TM
        carry_ref[...] = jnp.zeros(carry_ref.shape, F32)
        j = lax.broadcasted_iota(I32, (N_EXPERTS, n_tile_lanes), 1).astype(F32)
        owner = jnp.sum(jnp.where(t_off[:, 0:1] <= j, 1.0, 0.0), axis=0, keepdims=True) - 1.0
        total = jnp.sum(n_tile[:, 0:1], axis=0, keepdims=True)
        used = jnp.where(j[0:1, :] < total, 1.0, 0.0)
        row8 = lax.broadcasted_iota(I32, (8, n_tile_lanes), 0)
        tile_ref[...] = jnp.where(row8 == 0, owner, jnp.where(row8 == 1, used, 0.0)).astype(I32)

    @pl.when(ph == 1)
    def _():
        tr = lax.broadcasted_iota(I32, (tb, tb), 0)
        tc = lax.broadcasted_iota(I32, (tb, tb), 1)
        earlier = jnp.where(tr < tc, 1.0, 0.0).astype(BF16)
        seen = jnp.dot(a.astype(BF16), earlier, preferred_element_type=F32)
        dest = seen + carry_ref[:, 0:1] + off_ref[:, 0:1]
        p1 = jnp.sum(a1 * dest, axis=0, keepdims=True)
        p2 = jnp.sum(a2 * dest, axis=0, keepdims=True)
        row8 = lax.broadcasted_iota(I32, (8, tb), 0)
        pos_ref[...] = jnp.where(row8 == 0, p1, jnp.where(row8 == 1, p2, 0.0)).astype(I32)
        carry_ref[...] += n_here


def _moe_plan(route_t):
    _, T = route_t.shape
    n_tiles = _moe_rows(T) // MOE_TM
    n_tile_lanes = -(-n_tiles // LANES) * LANES
    nb = T // PLAN_TB
    return pl.pallas_call(
        functools.partial(_plan_kernel, n_tile_lanes=n_tile_lanes),
        out_shape=(jax.ShapeDtypeStruct((8, T), I32), jax.ShapeDtypeStruct((8, n_tile_lanes), I32)),
        grid=(2, nb),
        in_specs=[pl.BlockSpec((8, PLAN_TB), lambda ph, b: (0, b))],
        out_specs=(pl.BlockSpec((8, PLAN_TB), lambda ph, b: (0, b * ph)),
                   pl.BlockSpec((8, n_tile_lanes), lambda ph, b: (0, 0))),
        scratch_shapes=[pltpu.VMEM((N_EXPERTS, LANES), F32),
                        pltpu.VMEM((N_EXPERTS, LANES), F32),
                        pltpu.VMEM((N_EXPERTS, LANES), F32)],
        compiler_params=pltpu.CompilerParams(dimension_semantics=("arbitrary", "arbitrary")),
        name="moe_plan",
    )(route_t)


def _row_copy(src_ref, src_row, dst_ref, dst_row, sem):
    return pltpu.make_async_copy(src_ref.at[pl.ds(src_row, 1)], dst_ref.at[pl.ds(dst_row, 1)], sem)


def _dispatch_kernel(p1_ref, p2_ref, h_ref, init_hbm, out_hbm, sem, *, tb):
    del init_hbm

    def start(t, carry):
        _row_copy(h_ref, t, out_hbm, p1_ref[t], sem).start()
        _row_copy(h_ref, t, out_hbm, p2_ref[t], sem).start()
        return carry

    lax.fori_loop(0, tb, start, 0, unroll=GATHER_UNROLL)

    def wait(t, carry):
        _row_copy(h_ref, 0, out_hbm, 0, sem).wait()
        _row_copy(h_ref, 0, out_hbm, 0, sem).wait()
        return carry

    lax.fori_loop(0, tb, wait, 0, unroll=GATHER_UNROLL)


def _moe_dispatch(h, pos1, pos2, *, tb=512):
    T, D = h.shape
    rows = _moe_rows(T)
    return pl.pallas_call(
        functools.partial(_dispatch_kernel, tb=tb),
        out_shape=jax.ShapeDtypeStruct((rows, D), h.dtype),
        grid=(T // tb,),
        in_specs=[pl.BlockSpec((tb,), lambda i: (i,), memory_space=pltpu.SMEM),
                  pl.BlockSpec((tb,), lambda i: (i,), memory_space=pltpu.SMEM),
                  pl.BlockSpec((tb, D), lambda i: (i, 0)),
                  pl.BlockSpec(memory_space=pl.ANY)],
        out_specs=pl.BlockSpec(memory_space=pl.ANY),
        scratch_shapes=[pltpu.SemaphoreType.DMA(())],
        input_output_aliases={3: 0},
        compiler_params=pltpu.CompilerParams(dimension_semantics=("arbitrary",)),
        name="moe_dispatch",
    )(pos1, pos2, h, jnp.zeros((rows, D), h.dtype))


def _ffn_kernel(owner_ref, used_ref, x_ref, wg_ref, wu_ref, wd_ref, o_ref):
    j = pl.program_id(0)

    @pl.when(used_ref[j] == 1)
    def _():
        xb = x_ref[...].astype(BF16)
        gate = jnp.dot(xb, wg_ref[0].astype(BF16), preferred_element_type=F32)
        up = jnp.dot(xb, wu_ref[0].astype(BF16), preferred_element_type=F32)
        a = gate * jax.nn.sigmoid(gate) * up
        o_ref[...] = jnp.dot(a.astype(BF16), wd_ref[0].astype(BF16), preferred_element_type=F32)

    @pl.when(used_ref[j] == 0)
    def _():
        o_ref[...] = jnp.zeros(o_ref.shape, F32)


def _moe_ffn(xs, owner, used, w_gate, w_up, w_down):
    rows, D = xs.shape
    E, _, F = w_gate.shape
    return pl.pallas_call(
        _ffn_kernel,
        out_shape=jax.ShapeDtypeStruct((rows, D), F32),
        grid_spec=pltpu.PrefetchScalarGridSpec(
            num_scalar_prefetch=2,
            grid=(rows // MOE_TM,),
            in_specs=[pl.BlockSpec((MOE_TM, D), lambda j, ow, us: (j, 0)),
                      pl.BlockSpec((1, D, F), lambda j, ow, us: (ow[j], 0, 0)),
                      pl.BlockSpec((1, D, F), lambda j, ow, us: (ow[j], 0, 0)),
                      pl.BlockSpec((1, F, D), lambda j, ow, us: (ow[j], 0, 0))],
            out_specs=pl.BlockSpec((MOE_TM, D), lambda j, ow, us: (j, 0)),
        ),
        compiler_params=pltpu.CompilerParams(dimension_semantics=("arbitrary",), vmem_limit_bytes=VMEM_LIMIT),
        name="moe_ffn",
    )(owner, used, xs, w_gate, w_up, w_down)


def _combine_kernel(p1_ref, p2_ref, p1n_ref, p2n_ref, rw_ref, h_ref, g_ref, b_ref, y_hbm, o_ref,
                    buf_ref, sem, *, tm):
    s = pl.program_id(0)
    slot = s % 2

    def fetch(pa_ref, pb_ref, into):
        def start(t, carry):
            pltpu.make_async_copy(y_hbm.at[pl.ds(pa_ref[t], 1)], buf_ref.at[into, 0, pl.ds(t, 1)],
                                  sem.at[into]).start()
            pltpu.make_async_copy(y_hbm.at[pl.ds(pb_ref[t], 1)], buf_ref.at[into, 1, pl.ds(t, 1)],
                                  sem.at[into]).start()
            return carry
        lax.fori_loop(0, tm, start, 0, unroll=GATHER_UNROLL)

    @pl.when(s == 0)
    def _():
        fetch(p1_ref, p2_ref, 0)

    @pl.when(s + 1 < pl.num_programs(0))
    def _():
        fetch(p1n_ref, p2n_ref, 1 - slot)

    def wait(t, carry):
        for half in range(2):
            pltpu.make_async_copy(y_hbm.at[pl.ds(0, 1)], buf_ref.at[slot, half, pl.ds(0, 1)], sem.at[slot]).wait()
        return carry

    lax.fori_loop(0, tm, wait, 0, unroll=GATHER_UNROLL)
    y = rw_ref[:, 0:1] * buf_ref[slot, 0] + rw_ref[:, 1:2] * buf_ref[slot, 1]
    o_ref[...] = _layer_norm(DEEPNORM_ALPHA * h_ref[...] + y, g_ref[...], b_ref[...])


def _moe_combine(h, y_sorted, pos1, pos2, rw, ln_g, ln_b, *, tm=256):
    T, D = h.shape
    last = T // tm - 1
    smem = lambda imap: pl.BlockSpec((tm,), imap, memory_space=pltpu.SMEM)
    return pl.pallas_call(
        functools.partial(_combine_kernel, tm=tm),
        out_shape=jax.ShapeDtypeStruct((T, D), F32),
        grid=(T // tm,),
        in_specs=[smem(lambda i: (i,)), smem(lambda i: (i,)),
                  smem(lambda i: (jnp.minimum(i + 1, last),)), smem(lambda i: (jnp.minimum(i + 1, last),)),
                  pl.BlockSpec((tm, 8), lambda i: (i, 0)),
                  pl.BlockSpec((tm, D), lambda i: (i, 0)),
                  pl.BlockSpec((1, D), lambda i: (0, 0)),
                  pl.BlockSpec((1, D), lambda i: (0, 0)),
                  pl.BlockSpec(memory_space=pl.ANY)],
        out_specs=pl.BlockSpec((tm, D), lambda i: (i, 0)),
        scratch_shapes=[pltpu.VMEM((2, 2, tm, D), F32), pltpu.SemaphoreType.DMA((2,))],
        compiler_params=pltpu.CompilerParams(dimension_semantics=("arbitrary",), vmem_limit_bytes=VMEM_LIMIT),
        name="moe_combine",
    )(pos1, pos2, pos1, pos2, rw, h, ln_g[None, :], ln_b[None, :], y_sorted)


def _moe_norm(h, route_t, rw, w_gate, w_up, w_down, ln_g, ln_b):
    pos, tiles = _moe_plan(route_t)
    xs = _moe_dispatch(h, pos[0], pos[1])
    n_tiles = xs.shape[0] // MOE_TM
    ys = _moe_ffn(xs, tiles[0, :n_tiles], tiles[1, :n_tiles], w_gate, w_up, w_down)
    return _moe_combine(h, ys, pos[0], pos[1], rw, ln_g, ln_b)


def kernel(x, w_in, w_pool, pool_scale, w_out, rel_bias, ln1_g, ln1_b, w_r1, b_r1, w_r2, b_r2,
           w_gate, w_up, w_down, ln2_g, ln2_b):
    B, S, D = x.shape
    assert w_in.shape[0] == DEPTH == 1 and S % CHUNK == 0
    bias = _bias_tiles(rel_bias)
    qt, vt, iqt, iwt, k, ik, u = _in_proj(x, w_in[0])
    attn = _dsa_attention(qt, iqt, iwt, k, vt, ik, bias)
    h, route_t, rw = _mix_norm_route(attn, u, x, w_pool[0], pool_scale[0], w_out[0], ln1_g[0], ln1_b[0],
                                     w_r1[0], b_r1[0], w_r2[0], b_r2[0])
    out = _moe_norm(h.reshape(B * S, D), route_t, rw.reshape(B * S, 8),
                    w_gate[0], w_up[0], w_down[0], ln2_g[0], ln2_b[0])
    return out.reshape(B, S, D)
```

```python
import functools
import math

import numpy as np
import jax
import jax.numpy as jnp
from jax import lax
from jax.experimental import pallas as pl
from jax.experimental.pallas import tpu as pltpu

F32 = jnp.float32
BF16 = jnp.bfloat16
I32 = jnp.int32

ATTN_HEADS = 8
HEAD_DIM = 64
ATTN_WIDTH = ATTN_HEADS * HEAD_DIM
IDX_HEADS = 8
IDX_DIM = 64
TOPK_MAX = 256
POOL_WINDOWS = (2, 4, 8, 16)
POOL_GROUP_DIM = 128
POOL_WIDTH = len(POOL_WINDOWS) * POOL_GROUP_DIM
POOL_HALO = 16
REL_BUCKETS = 32
REL_MAX_DIST = 128
N_GROUPS = 4
EXPERTS_PER_GROUP = 8
N_EXPERTS = N_GROUPS * EXPERTS_PER_GROUP
LN_EPS = 1e-5
DEPTH = 1
DEEPNORM_ALPHA = (2 * DEPTH) ** 0.25
LOG2E = math.log2(math.e)

LANES = 128
CHUNK = 256
INT_MAX = 2 ** 31 - 1
BF16_ROWS = 16
NEG_BIG = -1e30
VMEM_LIMIT = 56 * 1024 * 1024


def _rel_bucket_table(n):
    max_exact = REL_BUCKETS // 2
    d = np.arange(n)
    nf = np.maximum(d, 1).astype(np.float32)
    ratio = np.log(nf / np.float32(max_exact)) / np.float32(math.log(REL_MAX_DIST / max_exact))
    large = max_exact + (ratio * np.float32(REL_BUCKETS - max_exact)).astype(np.int32)
    large = np.minimum(large, REL_BUCKETS - 1)
    return np.where(d < max_exact, d, large).astype(np.int32)


def _near_bucket_tiles():
    tbl = _rel_bucket_table(2 * CHUNK)
    a = np.arange(CHUNK)[:, None]
    b = np.arange(CHUNK)[None, :]
    tiles = [tbl[np.maximum(delta * CHUNK + b - a, 0)] for delta in (0, 1)]
    return np.stack(tiles).astype(np.int32)


FAR_BUCKET = REL_BUCKETS - 1
assert int(_rel_bucket_table(2 * CHUNK)[CHUNK + 1:].min()) == FAR_BUCKET


def _bias_kernel(rb_ref, bucket_ref, o_ref):
    h = pl.program_id(1)
    bk = bucket_ref[0]
    far = rb_ref[FAR_BUCKET, h]
    acc = jnp.zeros(bk.shape, F32)
    for n in range(REL_BUCKETS):
        acc = jnp.where(bk == n, rb_ref[n, h] - far, acc)
    o_ref[0, 0] = acc * LOG2E


def _bias_tiles(rel_bias):
    buckets = jnp.asarray(_near_bucket_tiles())
    return pl.pallas_call(
        _bias_kernel,
        out_shape=jax.ShapeDtypeStruct((2, ATTN_HEADS, CHUNK, CHUNK), F32),
        grid=(2, ATTN_HEADS),
        in_specs=[pl.BlockSpec(memory_space=pltpu.SMEM),
                  pl.BlockSpec((1, CHUNK, CHUNK), lambda d, h: (d, 0, 0))],
        out_specs=pl.BlockSpec((1, 1, CHUNK, CHUNK), lambda d, h: (d, h, 0, 0)),
        name="bias_tiles",
    )(rel_bias, buckets)


V_SLOT = HEAD_DIM + BF16_ROWS
VT_ROWS = ATTN_HEADS * V_SLOT
WT_Q, WT_V, WT_IQ, WT_IW = 0, ATTN_WIDTH, ATTN_WIDTH + VT_ROWS, 2 * ATTN_WIDTH + VT_ROWS
WT_ROWS = WT_IW + BF16_ROWS
WN_K, WN_U, WN_IK = 0, ATTN_WIDTH, ATTN_WIDTH + POOL_WIDTH
WN_COLS = WN_IK + LANES


def _proj_kernel(x_ref, wt_ref, wn_ref, qt_ref, vt_ref, iqt_ref, iwt_ref, k_ref, ik_ref, u_ref, *, tm):
    xb = x_ref[0].astype(BF16)
    t = lax.dot_general(wt_ref[...], xb, (((1,), (1,)), ((), ())),
                        preferred_element_type=F32)
    qt_ref[0] = (t[WT_Q:WT_V] * (HEAD_DIM ** -0.5 * LOG2E)).astype(BF16)
    iqt_ref[0] = (t[WT_IQ:WT_IW] * (IDX_DIM ** -0.5)).astype(BF16)
    iwt_ref[0] = t[WT_IW:WT_IW + IDX_HEADS] * (IDX_HEADS ** -0.5)
    slot_row = lax.broadcasted_iota(I32, (VT_ROWS, tm), 0) % V_SLOT
    vt = jnp.where(slot_row >= HEAD_DIM, 1.0, t[WT_V:WT_IQ]).astype(BF16)
    for j in range(tm // CHUNK):
        vt_ref[0, j] = vt[:, j * CHUNK:(j + 1) * CHUNK]
    n = jnp.dot(xb, wn_ref[...], preferred_element_type=F32)
    k_ref[0] = n[:, WN_K:WN_U].astype(BF16)
    u_ref[0] = n[:, WN_U:WN_IK]
    ik_ref[0] = n[:, WN_IK:WN_IK + IDX_DIM].astype(BF16)


def _in_proj(x, w_in, *, tm=512):
    B, S, D = x.shape
    q_off, k_off, v_off = 0, ATTN_WIDTH, 2 * ATTN_WIDTH
    iq_off = 3 * ATTN_WIDTH
    ik_off = iq_off + IDX_HEADS * IDX_DIM
    iw_off = ik_off + IDX_DIM
    p_off = iw_off + IDX_HEADS
    wv = w_in[:, v_off:iq_off].T.reshape(ATTN_HEADS, HEAD_DIM, D)
    wv = jnp.pad(wv, ((0, 0), (0, V_SLOT - HEAD_DIM), (0, 0))).reshape(VT_ROWS, D)
    wt = jnp.concatenate([
        w_in[:, q_off:k_off].T, wv, w_in[:, iq_off:ik_off].T, w_in[:, iw_off:p_off].T,
        jnp.zeros((WT_ROWS - WT_IW - IDX_HEADS, D), w_in.dtype)], axis=0).astype(BF16)
    wn = jnp.concatenate([
        w_in[:, k_off:v_off], w_in[:, p_off:], w_in[:, ik_off:iw_off],
        jnp.zeros((D, WN_COLS - WN_IK - IDX_DIM), w_in.dtype)], axis=1).astype(BF16)
    nt = S // tm
    cpt = tm // CHUNK
    outs = pl.pallas_call(
        functools.partial(_proj_kernel, tm=tm),
        out_shape=(
            jax.ShapeDtypeStruct((B, ATTN_WIDTH, S), BF16),
            jax.ShapeDtypeStruct((B, S // CHUNK, VT_ROWS, CHUNK), BF16),
            jax.ShapeDtypeStruct((B, IDX_HEADS * IDX_DIM, S), BF16),
            jax.ShapeDtypeStruct((B, IDX_HEADS, S), F32),
            jax.ShapeDtypeStruct((B, S, ATTN_WIDTH), BF16),
            jax.ShapeDtypeStruct((B, S, IDX_DIM), BF16),
            jax.ShapeDtypeStruct((B, S, POOL_WIDTH), F32),
        ),
        grid=(B, nt),
        in_specs=[pl.BlockSpec((1, tm, D), lambda b, i: (b, i, 0)),
                  pl.BlockSpec((WT_ROWS, D), lambda b, i: (0, 0)),
                  pl.BlockSpec((D, WN_COLS), lambda b, i: (0, 0))],
        out_specs=(
            pl.BlockSpec((1, ATTN_WIDTH, tm), lambda b, i: (b, 0, i)),
            pl.BlockSpec((1, cpt, VT_ROWS, CHUNK), lambda b, i: (b, i, 0, 0)),
            pl.BlockSpec((1, IDX_HEADS * IDX_DIM, tm), lambda b, i: (b, 0, i)),
            pl.BlockSpec((1, IDX_HEADS, tm), lambda b, i: (b, 0, i)),
            pl.BlockSpec((1, tm, ATTN_WIDTH), lambda b, i: (b, i, 0)),
            pl.BlockSpec((1, tm, IDX_DIM), lambda b, i: (b, i, 0)),
            pl.BlockSpec((1, tm, POOL_WIDTH), lambda b, i: (b, i, 0)),
        ),
        compiler_params=pltpu.CompilerParams(
            dimension_semantics=("arbitrary", "arbitrary"), vmem_limit_bytes=VMEM_LIMIT),
        name="in_proj",
    )(x, wt, wn)
    return outs


SNAP_ROUND = 7
SNAP_PERIOD = 3
WARM_ROUNDS = 6
MAX_ROUNDS = SNAP_ROUND + SNAP_PERIOD * 17
ROW_LO, ROW_HI, ROW_CNT, ROW_PROBE = 0, 1, 2, 3
FAR_SPAN = 2


def _order_key(bits):
    return bits ^ ((bits >> 31) & 0x7FFFFFFF)


Q2_ROWS = 2 * ATTN_WIDTH


def _attn_kernel(qt_ref, iqt_ref, iwt_ref, k_ref, vt_ref, ik_ref, bias_ref, o_ref,
                 sc_ref, st_ref, q2t_ref, neg_ref, lg_ref, p_ref, m_ref, al_ref, acc_ref,
                 *, topk):
    i = pl.program_id(1)
    nch = i + 1
    C = CHUNK

    @pl.when(jnp.logical_and(pl.program_id(0) == 0, i == 0))
    def _():
        q2t_ref[...] = jnp.zeros(q2t_ref.shape, BF16)

    for h in range(ATTN_HEADS):
        lo = 2 * HEAD_DIM * h + HEAD_DIM * (h % 2)
        q2t_ref[lo:lo + HEAD_DIM, :] = qt_ref[0, h * HEAD_DIM:(h + 1) * HEAD_DIM, :]
    row = lax.broadcasted_iota(I32, (C, C), 0)
    col = lax.broadcasted_iota(I32, (C, C), 1)

    def fold_rows(op, x):
        return op(x.reshape(C // 8, 8, C), axis=0)

    n_pair = lax.shift_right_logical(nch + 1, 1)

    def score_body(cp, carry):
        smin, smax = carry
        for c in (2 * cp, 2 * cp + 1):
            ikc = ik_ref[0, c]
            s = jnp.zeros((C, C), F32)
            for j in range(IDX_HEADS):
                d = jnp.dot(ikc, iqt_ref[0, j * IDX_DIM:(j + 1) * IDX_DIM, :],
                            preferred_element_type=F32)
                s = s + iwt_ref[0, j:j + 1, :] * jnp.maximum(d, 0.0)
            causal = (c * C + row) <= (i * C + col)
            sc_ref[c] = jnp.where(causal, s, -jnp.inf)
            smin = jnp.minimum(smin, fold_rows(jnp.min, jnp.where(causal, s, jnp.inf)))
            smax = jnp.maximum(smax, fold_rows(jnp.max, jnp.where(causal, s, -jnp.inf)))
        return smin, smax

    smin, smax = lax.fori_loop(0, n_pair, score_body,
                               (jnp.full((8, C), jnp.inf, F32), jnp.full((8, C), -jnp.inf, F32)))
    smin = jnp.min(smin, axis=0, keepdims=True)
    smax = jnp.max(smax, axis=0, keepdims=True)

    pos = i * C + lax.broadcasted_iota(I32, (1, C), 1)
    n_keys = (pos + 1).astype(F32)
    k_eff = jnp.minimum(pos + 1, topk).astype(F32)

    def count_ge(mid):
        def body(cp, acc):
            for c in (2 * cp, 2 * cp + 1):
                acc = acc + fold_rows(jnp.sum, jnp.where(sc_ref[c] >= mid, 1.0, 0.0))
            return acc
        part = lax.fori_loop(0, n_pair, body, jnp.zeros((8, C), F32))
        return jnp.sum(part, axis=0, keepdims=True)

    def load_state():
        return (st_ref[ROW_LO:ROW_LO + 1, :], st_ref[ROW_HI:ROW_HI + 1, :],
                st_ref[ROW_CNT:ROW_CNT + 1, :], st_ref[ROW_PROBE:ROW_PROBE + 1, :])

    def store_state(lo, hi, cnt_lo, probe):
        st_ref[ROW_LO:ROW_LO + 1, :] = lo
        st_ref[ROW_HI:ROW_HI + 1, :] = hi
        st_ref[ROW_CNT:ROW_CNT + 1, :] = cnt_lo
        st_ref[ROW_PROBE:ROW_PROBE + 1, :] = probe

    def open_cols(lo, hi, cnt_lo):
        return jnp.where(cnt_lo > k_eff, jnp.where(hi > lo, 1.0, 0.0), 0.0)

    def bisect(halve_image, lo, hi, cnt_lo, probe):
        klo = _order_key(pltpu.bitcast(lo, I32))
        khi = _order_key(pltpu.bitcast(hi, I32))
        i_mid = pltpu.bitcast(_order_key((klo >> 1) + (khi >> 1) + (klo & khi & 1)), F32)
        mid = jnp.where(halve_image, i_mid, 0.5 * lo + 0.5 * hi)
        probing = probe > lo
        mid = jnp.where(probing, probe, mid)
        ok = open_cols(lo, hi, cnt_lo) * jnp.where(mid > lo, jnp.where(mid < hi, 1.0, 0.0), 0.0)
        cnt = count_ge(mid)
        up = ok * jnp.where(cnt >= k_eff, 1.0, 0.0)
        dn = ok - up
        new_hi = jnp.where(dn + up * jnp.where(probing, 1.0, 0.0) > 0.0, mid, hi)
        return (jnp.where(up > 0.0, mid, lo), new_hi, jnp.where(up > 0.0, cnt, cnt_lo),
                jnp.full((1, C), -jnp.inf, F32))

    def snap(lo, hi, cnt_lo, probe):
        def body(c, carry):
            vmin, vmax = carry
            sc = sc_ref[c]
            vmin = jnp.minimum(vmin, fold_rows(jnp.min, jnp.where(sc >= lo, sc, jnp.inf)))
            vmax = jnp.maximum(vmax, fold_rows(jnp.max, jnp.where(sc < hi, sc, -jnp.inf)))
            return vmin, vmax
        vmin, vmax = lax.fori_loop(0, nch, body, (jnp.full((8, C), jnp.inf, F32),
                                                  jnp.full((8, C), -jnp.inf, F32)))
        vmin = jnp.min(vmin, axis=0, keepdims=True)
        vmax = jnp.max(vmax, axis=0, keepdims=True)
        is_open = open_cols(lo, hi, cnt_lo) > 0.0
        single = vmin >= vmax
        return (jnp.where(is_open, vmin, lo),
                jnp.where(is_open, jnp.where(single, vmin, hi), hi),
                cnt_lo,
                jnp.where(is_open, jnp.where(single, -jnp.inf, vmax), -jnp.inf))

    def search_cond(st):
        rnd, n_open = st
        return jnp.logical_and(rnd < MAX_ROUNDS, n_open > 0.0)

    def search_body(st):
        rnd, _ = st
        late = rnd >= SNAP_ROUND
        do_snap = jnp.logical_and(late, (rnd - SNAP_ROUND) % SNAP_PERIOD == 0)

        @pl.when(do_snap)
        def _():
            store_state(*snap(*load_state()))

        @pl.when(jnp.logical_not(do_snap))
        def _():
            store_state(*bisect(False, *load_state()))

        store_state(*bisect(late, *load_state()))
        lo, hi, cnt_lo, _ = load_state()
        return rnd + 1, jnp.max(open_cols(lo, hi, cnt_lo))

    kmax = _order_key(pltpu.bitcast(smax, I32))
    hi0 = pltpu.bitcast(_order_key(jnp.where(kmax == INT_MAX, kmax, kmax + 1)), F32)
    store_state(smin, hi0, n_keys, jnp.full((1, C), -jnp.inf, F32))

    @pl.when(jnp.max(open_cols(smin, hi0, n_keys)) > 0.0)
    def _():
        def warm(r, carry):
            store_state(*bisect(False, *load_state()))
            store_state(*bisect(False, *load_state()))
            return carry
        lax.fori_loop(0, WARM_ROUNDS, warm, 0)

    lo_w, hi_w, cnt_w, _ = load_state()
    lax.while_loop(search_cond, search_body, (jnp.int32(WARM_ROUNDS), jnp.max(open_cols(lo_w, hi_w, cnt_w))))
    th, _, cnt_th, _ = load_state()

    surplus = cnt_th - k_eff

    @pl.when(jnp.max(surplus) > 0.0)
    def _():
        later = jnp.where(col > row, 1.0, 0.0).astype(BF16)

        def drop_body(r, after):
            cp = n_pair - 1 - r
            for c in (2 * cp + 1, 2 * cp):
                sc = sc_ref[c]
                tied = jnp.where(sc == th, 1.0, 0.0)
                follow = jnp.dot(later, tied.astype(BF16), preferred_element_type=F32) + after
                sc_ref[c] = jnp.where(tied * jnp.where(follow < surplus, 1.0, 0.0) > 0.0, -jnp.inf, sc)
                after = after + jnp.sum(fold_rows(jnp.sum, tied), axis=0, keepdims=True)
            return after

        lax.fori_loop(0, n_pair, drop_body, jnp.zeros((1, C), F32))

    m_ref[...] = jnp.full(m_ref.shape, NEG_BIG, F32)
    acc_ref[...] = jnp.zeros(acc_ref.shape, F32)

    def attend(c, near, span=1):
        n = span * C
        neg = jnp.where(sc_ref[pl.ds(c, span)].reshape(n, C) >= th, 0.0, -jnp.inf)
        if near == 0:
            neg = jnp.where(row <= col, neg, -jnp.inf)
        neg_ref[0:n, :] = neg
        for h in range(ATTN_HEADS):
            p2 = (h // 2) * 2 * HEAD_DIM
            lg = jnp.dot(k_ref[0, pl.ds(c, span), :, p2:p2 + 2 * HEAD_DIM].reshape(n, 2 * HEAD_DIM),
                         q2t_ref[2 * HEAD_DIM * h:2 * HEAD_DIM * (h + 1), :],
                         preferred_element_type=F32)
            if near is not None:
                lg = lg + bias_ref[near, h]
            lg = lg + neg_ref[0:n, :]
            lg_ref[h, 0:n, :] = lg
            m_old = m_ref[h:h + 1, :]
            m_new = jnp.maximum(m_old, jnp.max(lg, axis=0, keepdims=True))
            al_ref[h:h + 1, :] = jnp.exp2(m_old - m_new)
            m_ref[h:h + 1, :] = m_new
        for h in range(ATTN_HEADS):
            p_ref[h, 0:n, :] = jnp.exp2(lg_ref[h, 0:n, :] - m_ref[h:h + 1, :]).astype(BF16)
        for h in range(ATTN_HEADS):
            hs = slice(h * V_SLOT, (h + 1) * V_SLOT)
            pv = jnp.dot(vt_ref[0, c, hs, :], p_ref[h, 0:C, :], preferred_element_type=F32)
            for j in range(1, span):
                pv = pv + jnp.dot(vt_ref[0, c + j, hs, :], p_ref[h, j * C:(j + 1) * C, :],
                                  preferred_element_type=F32)
            acc_ref[hs, :] = al_ref[h:h + 1, :] * acc_ref[hs, :] + pv

    n_far = jnp.maximum(i - 1, 0)

    def far_body(cp, carry):
        attend(2 * cp, None, span=FAR_SPAN)
        return carry

    lax.fori_loop(0, lax.shift_right_logical(n_far, 1), far_body, 0)

    @pl.when(n_far % 2 == 1)
    def _():
        attend(n_far - 1, None)

    @pl.when(i >= 1)
    def _():
        attend(i - 1, 1)

    attend(i, 0)

    out_t = jnp.concatenate(
        [acc_ref[h * V_SLOT:h * V_SLOT + HEAD_DIM, :] / acc_ref[h * V_SLOT + HEAD_DIM:h * V_SLOT + HEAD_DIM + 1, :]
         for h in range(ATTN_HEADS)], axis=0)
    o_ref[0] = out_t.T.astype(o_ref.dtype)


def _dsa_attention(qt, iqt, iwt, k, vt, ik, bias):
    B, S, _ = k.shape
    nq = S // CHUNK
    assert nq % 2 == 0
    topk = min(TOPK_MAX, S // 4)
    k4 = k.reshape(B, nq, CHUNK, ATTN_WIDTH)
    ik4 = ik.reshape(B, nq, CHUNK, IDX_DIM)
    return pl.pallas_call(
        functools.partial(_attn_kernel, topk=topk),
        out_shape=jax.ShapeDtypeStruct((B, S, ATTN_WIDTH), BF16),
        grid=(B, nq),
        in_specs=[
            pl.BlockSpec((1, ATTN_WIDTH, CHUNK), lambda b, i: (b, 0, i)),
            pl.BlockSpec((1, IDX_HEADS * IDX_DIM, CHUNK), lambda b, i: (b, 0, i)),
            pl.BlockSpec((1, IDX_HEADS, CHUNK), lambda b, i: (b, 0, i)),
            pl.BlockSpec((1, nq, CHUNK, ATTN_WIDTH), lambda b, i: (b, 0, 0, 0)),
            pl.BlockSpec((1, nq, VT_ROWS, CHUNK), lambda b, i: (b, 0, 0, 0)),
            pl.BlockSpec((1, nq, CHUNK, IDX_DIM), lambda b, i: (b, 0, 0, 0)),
            pl.BlockSpec((2, ATTN_HEADS, CHUNK, CHUNK), lambda b, i: (0, 0, 0, 0)),
        ],
        out_specs=pl.BlockSpec((1, CHUNK, ATTN_WIDTH), lambda b, i: (b, i, 0)),
        scratch_shapes=[
            pltpu.VMEM((nq, CHUNK, CHUNK), F32),
            pltpu.VMEM((8, CHUNK), F32),
            pltpu.VMEM((Q2_ROWS, CHUNK), BF16),
            pltpu.VMEM((FAR_SPAN * CHUNK, CHUNK), F32),
            pltpu.VMEM((ATTN_HEADS, FAR_SPAN * CHUNK, CHUNK), F32),
            pltpu.VMEM((ATTN_HEADS, FAR_SPAN * CHUNK, CHUNK), BF16),
            pltpu.VMEM((ATTN_HEADS, CHUNK), F32),
            pltpu.VMEM((ATTN_HEADS, CHUNK), F32),
            pltpu.VMEM((VT_ROWS, CHUNK), F32),
        ],
        compiler_params=pltpu.CompilerParams(
            dimension_semantics=("arbitrary", "arbitrary"), vmem_limit_bytes=VMEM_LIMIT),
        name="dsa_attn",
    )(qt, iqt, iwt, k4, vt, ik4, bias)


ROUTE_LANES = LANES
ROUTE_ROWS = 40


def _layer_norm(y, g, b):
    mu = jnp.mean(y, axis=-1, keepdims=True)
    yc = y - mu
    var = jnp.mean(yc * yc, axis=-1, keepdims=True)
    return yc * lax.rsqrt(var + LN_EPS) * g + b


def _mix_kernel(attn_ref, u_ref, halo_ref, x_ref, wpool_ref, pscale_ref, wout_ref, g_ref, b_ref,
                wr_ref, br_ref, h_ref, route_ref, rw_ref, *, tm):
    i = pl.program_id(1)
    halo = jnp.where(i > 0, halo_ref[0], 0.0)
    ue = jnp.concatenate([halo, u_ref[0]], axis=0)
    pos = i * tm + lax.broadcasted_iota(I32, (tm, 1), 0)
    mixed = []
    for g, w in enumerate(POOL_WINDOWS):
        gs = slice(g * POOL_GROUP_DIM, (g + 1) * POOL_GROUP_DIM)
        ch = ue[:, gs]
        win = ch
        step = 1
        while step < w:
            win = win + pltpu.roll(win, step, axis=0)
            step *= 2
        cnt = jnp.minimum(pos + 1, w).astype(F32)
        pooled = win[POOL_HALO:] / cnt - ch[POOL_HALO:]
        mg = jnp.dot(pooled.astype(BF16), wpool_ref[g], preferred_element_type=F32)
        mixed.append((mg * pscale_ref[:, gs]).astype(BF16))
    cat = jnp.concatenate([attn_ref[0]] + mixed, axis=-1)
    mix = jnp.dot(cat, wout_ref[...], preferred_element_type=F32)
    h = _layer_norm(DEEPNORM_ALPHA * x_ref[0] + mix, g_ref[...], b_ref[...])
    h_ref[0] = h
    hb = h.astype(BF16)

    lg = jnp.dot(hb, wr_ref[...], preferred_element_type=F32) + br_ref[...]
    lt = lg.T[0:ROUTE_ROWS, :]
    sub = lax.broadcasted_iota(I32, (ROUTE_ROWS, tm), 0).astype(F32)
    gl = jnp.where(sub >= N_EXPERTS, jnp.where(sub < N_EXPERTS + N_GROUPS, lt, -jnp.inf), -jnp.inf)
    ge = jnp.exp(gl - jnp.max(gl, axis=0, keepdims=True))
    pg = ge / jnp.sum(ge, axis=0, keepdims=True)
    pg_top = jnp.max(pg, axis=0, keepdims=True)
    g_row = jnp.min(jnp.where(pg == pg_top, sub, float(ROUTE_LANES)), axis=0, keepdims=True)
    e_lo = (g_row - N_EXPERTS) * EXPERTS_PER_GROUP
    fl = jnp.where(sub >= e_lo, jnp.where(sub < e_lo + EXPERTS_PER_GROUP, lt, -jnp.inf), -jnp.inf)
    fe = jnp.exp(fl - jnp.max(fl, axis=0, keepdims=True))
    pf = fe / jnp.sum(fe, axis=0, keepdims=True)
    p1 = jnp.max(pf, axis=0, keepdims=True)
    i1 = jnp.min(jnp.where(pf == p1, sub, float(ROUTE_LANES)), axis=0, keepdims=True)
    pr = jnp.where(sub == i1, -1.0, jnp.where(fl == -jnp.inf, -1.0, pf))
    p2 = jnp.max(pr, axis=0, keepdims=True)
    i2 = jnp.min(jnp.where(pr == p2, sub, float(ROUTE_LANES)), axis=0, keepdims=True)
    psum = p1 + p2
    w1 = pg_top * p1 / psum
    w2 = pg_top * p2 / psum
    row8 = lax.broadcasted_iota(I32, (8, tm), 0)
    route_ref[...] = jnp.where(row8 == 0, i1, jnp.where(row8 == 1, i2, 0.0)).astype(I32)
    rw_t = jnp.where(sub == 0.0, w1, jnp.where(sub == 1.0, w2, 0.0))
    rw_t = jnp.concatenate([rw_t, jnp.zeros((ROUTE_LANES - ROUTE_ROWS, tm), F32)], axis=0)
    rw_ref[0] = rw_t.T[:, 0:8]


def _mix_norm_route(attn, u, x, w_pool, pool_scale, w_out, ln_g, ln_b, w_r1, b_r1, w_r2, b_r2, *, tm=512):
    B, S, D = x.shape
    wr = jnp.concatenate([w_r2, w_r1, jnp.zeros((D, ROUTE_LANES - N_EXPERTS - N_GROUPS), w_r1.dtype)],
                         axis=1).astype(BF16)
    br = jnp.concatenate([b_r2, b_r1, jnp.zeros((ROUTE_LANES - N_EXPERTS - N_GROUPS,), b_r1.dtype)])[None, :]
    hpt = tm // POOL_HALO
    return pl.pallas_call(
        functools.partial(_mix_kernel, tm=tm),
        out_shape=(jax.ShapeDtypeStruct((B, S, D), F32),
                   jax.ShapeDtypeStruct((8, B * S), I32),
                   jax.ShapeDtypeStruct((B, S, 8), F32)),
        grid=(B, S // tm),
        in_specs=[
            pl.BlockSpec((1, tm, ATTN_WIDTH), lambda b, i: (b, i, 0)),
            pl.BlockSpec((1, tm, POOL_WIDTH), lambda b, i: (b, i, 0)),
            pl.BlockSpec((1, POOL_HALO, POOL_WIDTH), lambda b, i: (b, jnp.maximum(i * hpt - 1, 0), 0)),
            pl.BlockSpec((1, tm, D), lambda b, i: (b, i, 0)),
            pl.BlockSpec((len(POOL_WINDOWS), POOL_GROUP_DIM, POOL_GROUP_DIM), lambda b, i: (0, 0, 0)),
            pl.BlockSpec((1, POOL_WIDTH), lambda b, i: (0, 0)),
            pl.BlockSpec((D, D), lambda b, i: (0, 0)),
            pl.BlockSpec((1, D), lambda b, i: (0, 0)),
            pl.BlockSpec((1, D), lambda b, i: (0, 0)),
            pl.BlockSpec((D, ROUTE_LANES), lambda b, i: (0, 0)),
            pl.BlockSpec((1, ROUTE_LANES), lambda b, i: (0, 0)),
        ],
        out_specs=(pl.BlockSpec((1, tm, D), lambda b, i: (b, i, 0)),
                   pl.BlockSpec((8, tm), lambda b, i: (0, b * (S // tm) + i)),
                   pl.BlockSpec((1, tm, 8), lambda b, i: (b, i, 0))),
        compiler_params=pltpu.CompilerParams(
            dimension_semantics=("arbitrary", "arbitrary"), vmem_limit_bytes=VMEM_LIMIT),
        name="mix_norm",
    )(attn, u, u, x, w_pool.astype(BF16), pool_scale[None, :], w_out.astype(BF16),
      ln_g[None, :], ln_b[None, :], wr, br)


MOE_TM = 256
PLAN_TB = 1024
GATHER_UNROLL = 8


def _moe_rows(n_tokens):
    return 2 * n_tokens + N_EXPERTS * MOE_TM


def _plan_kernel(route_ref, pos_ref, tile_ref, cnt_ref, off_ref, carry_ref, *, n_tile_lanes):
    ph = pl.program_id(0)
    b = pl.program_id(1)
    tb = route_ref.shape[1]
    esub = lax.broadcasted_iota(I32, (N_EXPERTS, tb), 0)
    a1 = jnp.where(esub == route_ref[0:1, :], 1.0, 0.0)
    a2 = jnp.where(esub == route_ref[1:2, :], 1.0, 0.0)
    a = a1 + a2
    n_here = jnp.sum(a, axis=1, keepdims=True)

    @pl.when(jnp.logical_and(ph == 0, b == 0))
    def _():
        cnt_ref[...] = jnp.zeros(cnt_ref.shape, F32)

    @pl.when(ph == 0)
    def _():
        cnt_ref[...] += n_here

    @pl.when(jnp.logical_and(ph == 1, b == 0))
    def _():
        n_tile = jnp.floor((cnt_ref[...] + (MOE_TM - 1)) * (1.0 / MOE_TM))
        er = lax.broadcasted_iota(I32, (N_EXPERTS, N_EXPERTS), 0)
        ec = lax.broadcasted_iota(I32, (N_EXPERTS, N_EXPERTS), 1)
        before = jnp.where(ec < er, 1.0, 0.0).astype(BF16)
        t_off = jnp.dot(before, n_tile.astype(BF16), preferred_element_type=F32)
        off_ref[...] = t_off * MOE_TM
        carry_ref[...] = jnp.zeros(carry_ref.shape, F32)
        j = lax.broadcasted_iota(I32, (N_EXPERTS, n_tile_lanes), 1).astype(F32)
        owner = jnp.sum(jnp.where(t_off[:, 0:1] <= j, 1.0, 0.0), axis=0, keepdims=True) - 1.0
        total = jnp.sum(n_tile[:, 0:1], axis=0, keepdims=True)
        used = jnp.where(j[0:1, :] < total, 1.0, 0.0)
        row8 = lax.broadcasted_iota(I32, (8, n_tile_lanes), 0)
        tile_ref[...] = jnp.where(row8 == 0, owner, jnp.where(row8 == 1, used, 0.0)).astype(I32)

    @pl.when(ph == 1)
    def _():
        tr = lax.broadcasted_iota(I32, (tb, tb), 0)
        tc = lax.broadcasted_iota(I32, (tb, tb), 1)
        earlier = jnp.where(tr < tc, 1.0, 0.0).astype(BF16)
        seen = jnp.dot(a.astype(BF16), earlier, preferred_element_type=F32)
        dest = seen + carry_ref[:, 0:1] + off_ref[:, 0:1]
        p1 = jnp.sum(a1 * dest, axis=0, keepdims=True)
        p2 = jnp.sum(a2 * dest, axis=0, keepdims=True)
        row8 = lax.broadcasted_iota(I32, (8, tb), 0)
        pos_ref[...] = jnp.where(row8 == 0, p1, jnp.where(row8 == 1, p2, 0.0)).astype(I32)
        carry_ref[...] += n_here


def _moe_plan(route_t):
    _, T = route_t.shape
    n_tiles = _moe_rows(T) // MOE_TM
    n_tile_lanes = -(-n_tiles // LANES) * LANES
    nb = T // PLAN_TB
    return pl.pallas_call(
        functools.partial(_plan_kernel, n_tile_lanes=n_tile_lanes),
        out_shape=(jax.ShapeDtypeStruct((8, T), I32), jax.ShapeDtypeStruct((8, n_tile_lanes), I32)),
        grid=(2, nb),
        in_specs=[pl.BlockSpec((8, PLAN_TB), lambda ph, b: (0, b))],
        out_specs=(pl.BlockSpec((8, PLAN_TB), lambda ph, b: (0, b * ph)),
                   pl.BlockSpec((8, n_tile_lanes), lambda ph, b: (0, 0))),
        scratch_shapes=[pltpu.VMEM((N_EXPERTS, LANES), F32),
                        pltpu.VMEM((N_EXPERTS, LANES), F32),
                        pltpu.VMEM((N_EXPERTS, LANES), F32)],
        compiler_params=pltpu.CompilerParams(dimension_semantics=("arbitrary", "arbitrary")),
        name="moe_plan",
    )(route_t)


def _row_copy(src_ref, src_row, dst_ref, dst_row, sem):
    return pltpu.make_async_copy(src_ref.at[pl.ds(src_row, 1)], dst_ref.at[pl.ds(dst_row, 1)], sem)


def _dispatch_kernel(p1_ref, p2_ref, h_ref, init_hbm, out_hbm, sem, *, tb):
    del init_hbm

    def start(t, carry):
        _row_copy(h_ref, t, out_hbm, p1_ref[t], sem).start()
        _row_copy(h_ref, t, out_hbm, p2_ref[t], sem).start()
        return carry

    lax.fori_loop(0, tb, start, 0, unroll=GATHER_UNROLL)

    def wait(t, carry):
        _row_copy(h_ref, 0, out_hbm, 0, sem).wait()
        _row_copy(h_ref, 0, out_hbm, 0, sem).wait()
        return carry

    lax.fori_loop(0, tb, wait, 0, unroll=GATHER_UNROLL)


def _moe_dispatch(h, pos1, pos2, *, tb=1024):
    T, D = h.shape
    rows = _moe_rows(T)
    return pl.pallas_call(
        functools.partial(_dispatch_kernel, tb=tb),
        out_shape=jax.ShapeDtypeStruct((rows, D), h.dtype),
        grid=(T // tb,),
        in_specs=[pl.BlockSpec((tb,), lambda i: (i,), memory_space=pltpu.SMEM),
                  pl.BlockSpec((tb,), lambda i: (i,), memory_space=pltpu.SMEM),
                  pl.BlockSpec((tb, D), lambda i: (i, 0)),
                  pl.BlockSpec(memory_space=pl.ANY)],
        out_specs=pl.BlockSpec(memory_space=pl.ANY),
        scratch_shapes=[pltpu.SemaphoreType.DMA(())],
        input_output_aliases={3: 0},
        compiler_params=pltpu.CompilerParams(dimension_semantics=("arbitrary",)),
        name="moe_dispatch",
    )(pos1, pos2, h, jnp.zeros((rows, D), h.dtype))


def _ffn_kernel(owner_ref, used_ref, x_ref, wg_ref, wu_ref, wd_ref, o_ref):
    j = pl.program_id(0)

    @pl.when(used_ref[j] == 1)
    def _():
        xb = x_ref[...].astype(BF16)
        gate = jnp.dot(xb, wg_ref[0].astype(BF16), preferred_element_type=F32)
        up = jnp.dot(xb, wu_ref[0].astype(BF16), preferred_element_type=F32)
        a = gate * jax.nn.sigmoid(gate) * up
        o_ref[...] = jnp.dot(a.astype(BF16), wd_ref[0].astype(BF16), preferred_element_type=F32)

    @pl.when(used_ref[j] == 0)
    def _():
        o_ref[...] = jnp.zeros(o_ref.shape, F32)


def _moe_ffn(xs, owner, used, w_gate, w_up, w_down):
    rows, D = xs.shape
    E, _, F = w_gate.shape
    return pl.pallas_call(
        _ffn_kernel,
        out_shape=jax.ShapeDtypeStruct((rows, D), F32),
        grid_spec=pltpu.PrefetchScalarGridSpec(
            num_scalar_prefetch=2,
            grid=(rows // MOE_TM,),
            in_specs=[pl.BlockSpec((MOE_TM, D), lambda j, ow, us: (j, 0)),
                      pl.BlockSpec((1, D, F), lambda j, ow, us: (ow[j], 0, 0)),
                      pl.BlockSpec((1, D, F), lambda j, ow, us: (ow[j], 0, 0)),
                      pl.BlockSpec((1, F, D), lambda j, ow, us: (ow[j], 0, 0))],
            out_specs=pl.BlockSpec((MOE_TM, D), lambda j, ow, us: (j, 0)),
        ),
        compiler_params=pltpu.CompilerParams(dimension_semantics=("arbitrary",), vmem_limit_bytes=VMEM_LIMIT),
        name="moe_ffn",
    )(owner, used, xs, w_gate, w_up, w_down)


def _combine_kernel(p1_ref, p2_ref, p1n_ref, p2n_ref, rw_ref, h_ref, g_ref, b_ref, y_hbm, o_ref,
                    buf_ref, sem, *, tm):
    s = pl.program_id(0)
    slot = s % 2

    def fetch(pa_ref, pb_ref, into):
        def start(t, carry):
            pltpu.make_async_copy(y_hbm.at[pl.ds(pa_ref[t], 1)], buf_ref.at[into, 0, pl.ds(t, 1)],
                                  sem.at[into]).start()
            pltpu.make_async_copy(y_hbm.at[pl.ds(pb_ref[t], 1)], buf_ref.at[into, 1, pl.ds(t, 1)],
                                  sem.at[into]).start()
            return carry
        lax.fori_loop(0, tm, start, 0, unroll=GATHER_UNROLL)

    @pl.when(s == 0)
    def _():
        fetch(p1_ref, p2_ref, 0)

    @pl.when(s + 1 < pl.num_programs(0))
    def _():
        fetch(p1n_ref, p2n_ref, 1 - slot)

    def wait(t, carry):
        for half in range(2):
            pltpu.make_async_copy(y_hbm.at[pl.ds(0, 1)], buf_ref.at[slot, half, pl.ds(0, 1)], sem.at[slot]).wait()
        return carry

    lax.fori_loop(0, tm, wait, 0, unroll=GATHER_UNROLL)
    y = rw_ref[:, 0:1] * buf_ref[slot, 0] + rw_ref[:, 1:2] * buf_ref[slot, 1]
    o_ref[...] = _layer_norm(DEEPNORM_ALPHA * h_ref[...] + y, g_ref[...], b_ref[...])


def _moe_combine(h, y_sorted, pos1, pos2, rw, ln_g, ln_b, *, tm=512):
    T, D = h.shape
    last = T // tm - 1
    smem = lambda imap: pl.BlockSpec((tm,), imap, memory_space=pltpu.SMEM)
    return pl.pallas_call(
        functools.partial(_combine_kernel, tm=tm),
        out_shape=jax.ShapeDtypeStruct((T, D), F32),
        grid=(T // tm,),
        in_specs=[smem(lambda i: (i,)), smem(lambda i: (i,)),
                  smem(lambda i: (jnp.minimum(i + 1, last),)), smem(lambda i: (jnp.minimum(i + 1, last),)),
                  pl.BlockSpec((tm, 8), lambda i: (i, 0)),
                  pl.BlockSpec((tm, D), lambda i: (i, 0)),
                  pl.BlockSpec((1, D), lambda i: (0, 0)),
                  pl.BlockSpec((1, D), lambda i: (0, 0)),
                  pl.BlockSpec(memory_space=pl.ANY)],
        out_specs=pl.BlockSpec((tm, D), lambda i: (i, 0)),
        scratch_shapes=[pltpu.VMEM((2, 2, tm, D), F32), pltpu.SemaphoreType.DMA((2,))],
        compiler_params=pltpu.CompilerParams(dimension_semantics=("arbitrary",), vmem_limit_bytes=VMEM_LIMIT),
        name="moe_combine",
    )(pos1, pos2, pos1, pos2, rw, h, ln_g[None, :], ln_b[None, :], y_sorted)


def _moe_norm(h, route_t, rw, w_gate, w_up, w_down, ln_g, ln_b):
    pos, tiles = _moe_plan(route_t)
    xs = _moe_dispatch(h, pos[0], pos[1])
    n_tiles = xs.shape[0] // MOE_TM
    ys = _moe_ffn(xs, tiles[0, :n_tiles], tiles[1, :n_tiles], w_gate, w_up, w_down)
    return _moe_combine(h, ys, pos[0], pos[1], rw, ln_g, ln_b)


def kernel(x, w_in, w_pool, pool_scale, w_out, rel_bias, ln1_g, ln1_b, w_r1, b_r1, w_r2, b_r2,
           w_gate, w_up, w_down, ln2_g, ln2_b):
    B, S, D = x.shape
    assert w_in.shape[0] == DEPTH == 1 and S % CHUNK == 0
    bias = _bias_tiles(rel_bias)
    qt, vt, iqt, iwt, k, ik, u = _in_proj(x, w_in[0])
    attn = _dsa_attention(qt, iqt, iwt, k, vt, ik, bias)
    h, route_t, rw = _mix_norm_route(attn, u, x, w_pool[0], pool_scale[0], w_out[0], ln1_g[0], ln1_b[0],
                                     w_r1[0], b_r1[0], w_r2[0], b_r2[0])
    out = _moe_norm(h.reshape(B * S, D), route_t, rw.reshape(B * S, 8),
                    w_gate[0], w_up[0], w_down[0], ln2_g[0], ln2_b[0])
    return out.reshape(B, S, D)
```

```python
import functools
import math

import numpy as np
import jax
import jax.numpy as jnp
from jax import lax
from jax.experimental import pallas as pl
from jax.experimental.pallas import tpu as pltpu

F32 = jnp.float32
BF16 = jnp.bfloat16
I32 = jnp.int32

ATTN_HEADS = 8
HEAD_DIM = 64
ATTN_WIDTH = ATTN_HEADS * HEAD_DIM
IDX_HEADS = 8
IDX_DIM = 64
TOPK_MAX = 256
POOL_WINDOWS = (2, 4, 8, 16)
POOL_GROUP_DIM = 128
POOL_WIDTH = len(POOL_WINDOWS) * POOL_GROUP_DIM
POOL_HALO = 16
REL_BUCKETS = 32
REL_MAX_DIST = 128
N_GROUPS = 4
EXPERTS_PER_GROUP = 8
N_EXPERTS = N_GROUPS * EXPERTS_PER_GROUP
LN_EPS = 1e-5
DEPTH = 1
DEEPNORM_ALPHA = (2 * DEPTH) ** 0.25
LOG2E = math.log2(math.e)

LANES = 128
CHUNK = 256
INT_MAX = 2 ** 31 - 1
BF16_ROWS = 16
NEG_BIG = -1e30
VMEM_LIMIT = 56 * 1024 * 1024


def _rel_bucket_table(n):
    max_exact = REL_BUCKETS // 2
    d = np.arange(n)
    nf = np.maximum(d, 1).astype(np.float32)
    ratio = np.log(nf / np.float32(max_exact)) / np.float32(math.log(REL_MAX_DIST / max_exact))
    large = max_exact + (ratio * np.float32(REL_BUCKETS - max_exact)).astype(np.int32)
    large = np.minimum(large, REL_BUCKETS - 1)
    return np.where(d < max_exact, d, large).astype(np.int32)


def _near_bucket_tiles():
    tbl = _rel_bucket_table(2 * CHUNK)
    a = np.arange(CHUNK)[:, None]
    b = np.arange(CHUNK)[None, :]
    tiles = [tbl[np.maximum(delta * CHUNK + b - a, 0)] for delta in (0, 1)]
    return np.stack(tiles).astype(np.int32)


FAR_BUCKET = REL_BUCKETS - 1
assert int(_rel_bucket_table(2 * CHUNK)[CHUNK + 1:].min()) == FAR_BUCKET


def _bias_kernel(rb_ref, bucket_ref, o_ref):
    h = pl.program_id(1)
    bk = bucket_ref[0]
    far = rb_ref[FAR_BUCKET, h]
    acc = jnp.zeros(bk.shape, F32)
    for n in range(REL_BUCKETS):
        acc = jnp.where(bk == n, rb_ref[n, h] - far, acc)
    o_ref[0, 0] = acc * LOG2E


def _bias_tiles(rel_bias):
    buckets = jnp.asarray(_near_bucket_tiles())
    return pl.pallas_call(
        _bias_kernel,
        out_shape=jax.ShapeDtypeStruct((2, ATTN_HEADS, CHUNK, CHUNK), F32),
        grid=(2, ATTN_HEADS),
        in_specs=[pl.BlockSpec(memory_space=pltpu.SMEM),
                  pl.BlockSpec((1, CHUNK, CHUNK), lambda d, h: (d, 0, 0))],
        out_specs=pl.BlockSpec((1, 1, CHUNK, CHUNK), lambda d, h: (d, h, 0, 0)),
        name="bias_tiles",
    )(rel_bias, buckets)


V_SLOT = HEAD_DIM + BF16_ROWS
VT_ROWS = ATTN_HEADS * V_SLOT
WT_Q, WT_V, WT_IQ, WT_IW = 0, ATTN_WIDTH, ATTN_WIDTH + VT_ROWS, 2 * ATTN_WIDTH + VT_ROWS
WT_ROWS = WT_IW + BF16_ROWS
WN_K, WN_U, WN_IK = 0, ATTN_WIDTH, ATTN_WIDTH + POOL_WIDTH
WN_COLS = WN_IK + LANES


def _proj_kernel(x_ref, wt_ref, wn_ref, qt_ref, vt_ref, iqt_ref, iwt_ref, k_ref, ik_ref, u_ref, *, tm):
    xb = x_ref[0].astype(BF16)
    t = lax.dot_general(wt_ref[...], xb, (((1,), (1,)), ((), ())),
                        preferred_element_type=F32)
    qt_ref[0] = (t[WT_Q:WT_V] * (HEAD_DIM ** -0.5 * LOG2E)).astype(BF16)
    iqt_ref[0] = (t[WT_IQ:WT_IW] * (IDX_DIM ** -0.5)).astype(BF16)
    iwt_ref[0] = t[WT_IW:WT_IW + IDX_HEADS] * (IDX_HEADS ** -0.5)
    slot_row = lax.broadcasted_iota(I32, (VT_ROWS, tm), 0) % V_SLOT
    vt = jnp.where(slot_row >= HEAD_DIM, 1.0, t[WT_V:WT_IQ]).astype(BF16)
    for j in range(tm // CHUNK):
        vt_ref[0, j] = vt[:, j * CHUNK:(j + 1) * CHUNK]
    n = jnp.dot(xb, wn_ref[...], preferred_element_type=F32)
    k_ref[0] = n[:, WN_K:WN_U].astype(BF16)
    u_ref[0] = n[:, WN_U:WN_IK]
    ik_ref[0] = n[:, WN_IK:WN_IK + IDX_DIM].astype(BF16)


def _in_proj(x, w_in, *, tm=512):
    B, S, D = x.shape
    q_off, k_off, v_off = 0, ATTN_WIDTH, 2 * ATTN_WIDTH
    iq_off = 3 * ATTN_WIDTH
    ik_off = iq_off + IDX_HEADS * IDX_DIM
    iw_off = ik_off + IDX_DIM
    p_off = iw_off + IDX_HEADS
    wv = w_in[:, v_off:iq_off].T.reshape(ATTN_HEADS, HEAD_DIM, D)
    wv = jnp.pad(wv, ((0, 0), (0, V_SLOT - HEAD_DIM), (0, 0))).reshape(VT_ROWS, D)
    wt = jnp.concatenate([
        w_in[:, q_off:k_off].T, wv, w_in[:, iq_off:ik_off].T, w_in[:, iw_off:p_off].T,
        jnp.zeros((WT_ROWS - WT_IW - IDX_HEADS, D), w_in.dtype)], axis=0).astype(BF16)
    wn = jnp.concatenate([
        w_in[:, k_off:v_off], w_in[:, p_off:], w_in[:, ik_off:iw_off],
        jnp.zeros((D, WN_COLS - WN_IK - IDX_DIM), w_in.dtype)], axis=1).astype(BF16)
    nt = S // tm
    cpt = tm // CHUNK
    outs = pl.pallas_call(
        functools.partial(_proj_kernel, tm=tm),
        out_shape=(
            jax.ShapeDtypeStruct((B, ATTN_WIDTH, S), BF16),
            jax.ShapeDtypeStruct((B, S // CHUNK, VT_ROWS, CHUNK), BF16),
            jax.ShapeDtypeStruct((B, IDX_HEADS * IDX_DIM, S), BF16),
            jax.ShapeDtypeStruct((B, IDX_HEADS, S), F32),
            jax.ShapeDtypeStruct((B, S, ATTN_WIDTH), BF16),
            jax.ShapeDtypeStruct((B, S, IDX_DIM), BF16),
            jax.ShapeDtypeStruct((B, S, POOL_WIDTH), F32),
        ),
        grid=(B, nt),
        in_specs=[pl.BlockSpec((1, tm, D), lambda b, i: (b, i, 0)),
                  pl.BlockSpec((WT_ROWS, D), lambda b, i: (0, 0)),
                  pl.BlockSpec((D, WN_COLS), lambda b, i: (0, 0))],
        out_specs=(
            pl.BlockSpec((1, ATTN_WIDTH, tm), lambda b, i: (b, 0, i)),
            pl.BlockSpec((1, cpt, VT_ROWS, CHUNK), lambda b, i: (b, i, 0, 0)),
            pl.BlockSpec((1, IDX_HEADS * IDX_DIM, tm), lambda b, i: (b, 0, i)),
            pl.BlockSpec((1, IDX_HEADS, tm), lambda b, i: (b, 0, i)),
            pl.BlockSpec((1, tm, ATTN_WIDTH), lambda b, i: (b, i, 0)),
            pl.BlockSpec((1, tm, IDX_DIM), lambda b, i: (b, i, 0)),
            pl.BlockSpec((1, tm, POOL_WIDTH), lambda b, i: (b, i, 0)),
        ),
        compiler_params=pltpu.CompilerParams(
            dimension_semantics=("arbitrary", "arbitrary"), vmem_limit_bytes=VMEM_LIMIT),
        name="in_proj",
    )(x, wt, wn)
    return outs


SNAP_ROUND = 7
SNAP_PERIOD = 3
WARM_ROUNDS = 6
MAX_ROUNDS = SNAP_ROUND + SNAP_PERIOD * 17
ROW_LO, ROW_HI, ROW_CNT, ROW_PROBE = 0, 1, 2, 3
FAR_SPAN = 2


def _order_key(bits):
    return bits ^ ((bits >> 31) & 0x7FFFFFFF)


Q2_ROWS = 2 * ATTN_WIDTH


def _attn_kernel(qt_ref, iqt_ref, iwt_ref, k_ref, vt_ref, ik_ref, bias_ref, o_ref,
                 sc_ref, st_ref, q2t_ref, neg_ref, lg_ref, p_ref, m_ref, al_ref, acc_ref,
                 *, topk):
    i = pl.program_id(1)
    nch = i + 1
    C = CHUNK

    @pl.when(jnp.logical_and(pl.program_id(0) == 0, i == 0))
    def _():
        q2t_ref[...] = jnp.zeros(q2t_ref.shape, BF16)

    for h in range(ATTN_HEADS):
        lo = 2 * HEAD_DIM * h + HEAD_DIM * (h % 2)
        q2t_ref[lo:lo + HEAD_DIM, :] = qt_ref[0, h * HEAD_DIM:(h + 1) * HEAD_DIM, :]
    row = lax.broadcasted_iota(I32, (C, C), 0)
    col = lax.broadcasted_iota(I32, (C, C), 1)

    def fold_rows(op, x):
        return op(x.reshape(C // 8, 8, C), axis=0)

    n_pair = lax.shift_right_logical(nch + 1, 1)

    def score_body(cp, carry):
        smin, smax = carry
        for c in (2 * cp, 2 * cp + 1):
            ikc = ik_ref[0, c]
            s = jnp.zeros((C, C), F32)
            for j in range(IDX_HEADS):
                d = jnp.dot(ikc, iqt_ref[0, j * IDX_DIM:(j + 1) * IDX_DIM, :],
                            preferred_element_type=F32)
                s = s + iwt_ref[0, j:j + 1, :] * jnp.maximum(d, 0.0)
            causal = (c * C + row) <= (i * C + col)
            sc_ref[c] = jnp.where(causal, s, -jnp.inf)
            smin = jnp.minimum(smin, fold_rows(jnp.min, jnp.where(causal, s, jnp.inf)))
            smax = jnp.maximum(smax, fold_rows(jnp.max, jnp.where(causal, s, -jnp.inf)))
        return smin, smax

    smin, smax = lax.fori_loop(0, n_pair, score_body,
                               (jnp.full((8, C), jnp.inf, F32), jnp.full((8, C), -jnp.inf, F32)))
    smin = jnp.min(smin, axis=0, keepdims=True)
    smax = jnp.max(smax, axis=0, keepdims=True)

    pos = i * C + lax.broadcasted_iota(I32, (1, C), 1)
    n_keys = (pos + 1).astype(F32)
    k_eff = jnp.minimum(pos + 1, topk).astype(F32)

    def count_ge(mid):
        def body(cp, acc):
            for c in (2 * cp, 2 * cp + 1):
                acc = acc + fold_rows(jnp.sum, jnp.where(sc_ref[c] >= mid, 1.0, 0.0))
            return acc
        part = lax.fori_loop(0, n_pair, body, jnp.zeros((8, C), F32))
        return jnp.sum(part, axis=0, keepdims=True)

    def load_state():
        return (st_ref[ROW_LO:ROW_LO + 1, :], st_ref[ROW_HI:ROW_HI + 1, :],
                st_ref[ROW_CNT:ROW_CNT + 1, :], st_ref[ROW_PROBE:ROW_PROBE + 1, :])

    def store_state(lo, hi, cnt_lo, probe):
        st_ref[ROW_LO:ROW_LO + 1, :] = lo
        st_ref[ROW_HI:ROW_HI + 1, :] = hi
        st_ref[ROW_CNT:ROW_CNT + 1, :] = cnt_lo
        st_ref[ROW_PROBE:ROW_PROBE + 1, :] = probe

    def open_cols(lo, hi, cnt_lo):
        return jnp.where(cnt_lo > k_eff, jnp.where(hi > lo, 1.0, 0.0), 0.0)

    def bisect(halve_image, lo, hi, cnt_lo, probe):
        klo = _order_key(pltpu.bitcast(lo, I32))
        khi = _order_key(pltpu.bitcast(hi, I32))
        i_mid = pltpu.bitcast(_order_key((klo >> 1) + (khi >> 1) + (klo & khi & 1)), F32)
        mid = jnp.where(halve_image, i_mid, 0.5 * lo + 0.5 * hi)
        probing = probe > lo
        mid = jnp.where(probing, probe, mid)
        ok = open_cols(lo, hi, cnt_lo) * jnp.where(mid > lo, jnp.where(mid < hi, 1.0, 0.0), 0.0)
        cnt = count_ge(mid)
        up = ok * jnp.where(cnt >= k_eff, 1.0, 0.0)
        dn = ok - up
        new_hi = jnp.where(dn + up * jnp.where(probing, 1.0, 0.0) > 0.0, mid, hi)
        return (jnp.where(up > 0.0, mid, lo), new_hi, jnp.where(up > 0.0, cnt, cnt_lo),
                jnp.full((1, C), -jnp.inf, F32))

    def snap(lo, hi, cnt_lo, probe):
        def body(c, carry):
            vmin, vmax = carry
            sc = sc_ref[c]
            vmin = jnp.minimum(vmin, fold_rows(jnp.min, jnp.where(sc >= lo, sc, jnp.inf)))
            vmax = jnp.maximum(vmax, fold_rows(jnp.max, jnp.where(sc < hi, sc, -jnp.inf)))
            return vmin, vmax
        vmin, vmax = lax.fori_loop(0, nch, body, (jnp.full((8, C), jnp.inf, F32),
                                                  jnp.full((8, C), -jnp.inf, F32)))
        vmin = jnp.min(vmin, axis=0, keepdims=True)
        vmax = jnp.max(vmax, axis=0, keepdims=True)
        is_open = open_cols(lo, hi, cnt_lo) > 0.0
        single = vmin >= vmax
        return (jnp.where(is_open, vmin, lo),
                jnp.where(is_open, jnp.where(single, vmin, hi), hi),
                cnt_lo,
                jnp.where(is_open, jnp.where(single, -jnp.inf, vmax), -jnp.inf))

    def search_cond(st):
        rnd, n_open = st
        return jnp.logical_and(rnd < MAX_ROUNDS, n_open > 0.0)

    def search_body(st):
        rnd, _ = st
        late = rnd >= SNAP_ROUND
        do_snap = jnp.logical_and(late, (rnd - SNAP_ROUND) % SNAP_PERIOD == 0)

        @pl.when(do_snap)
        def _():
            store_state(*snap(*load_state()))

        @pl.when(jnp.logical_not(do_snap))
        def _():
            store_state(*bisect(False, *load_state()))

        store_state(*bisect(late, *load_state()))
        lo, hi, cnt_lo, _ = load_state()
        return rnd + 1, jnp.max(open_cols(lo, hi, cnt_lo))

    kmax = _order_key(pltpu.bitcast(smax, I32))
    hi0 = pltpu.bitcast(_order_key(jnp.where(kmax == INT_MAX, kmax, kmax + 1)), F32)
    store_state(smin, hi0, n_keys, jnp.full((1, C), -jnp.inf, F32))

    @pl.when(jnp.max(open_cols(smin, hi0, n_keys)) > 0.0)
    def _():
        def warm(r, carry):
            store_state(*bisect(False, *load_state()))
            store_state(*bisect(False, *load_state()))
            return carry
        lax.fori_loop(0, WARM_ROUNDS, warm, 0)

    lo_w, hi_w, cnt_w, _ = load_state()
    lax.while_loop(search_cond, search_body, (jnp.int32(WARM_ROUNDS), jnp.max(open_cols(lo_w, hi_w, cnt_w))))
    th, _, cnt_th, _ = load_state()

    surplus = cnt_th - k_eff

    @pl.when(jnp.max(surplus) > 0.0)
    def _():
        later = jnp.where(col > row, 1.0, 0.0).astype(BF16)

        def drop_body(r, after):
            cp = n_pair - 1 - r
            for c in (2 * cp + 1, 2 * cp):
                sc = sc_ref[c]
                tied = jnp.where(sc == th, 1.0, 0.0)
                follow = jnp.dot(later, tied.astype(BF16), preferred_element_type=F32) + after
                sc_ref[c] = jnp.where(tied * jnp.where(follow < surplus, 1.0, 0.0) > 0.0, -jnp.inf, sc)
                after = after + jnp.sum(fold_rows(jnp.sum, tied), axis=0, keepdims=True)
            return after

        lax.fori_loop(0, n_pair, drop_body, jnp.zeros((1, C), F32))

    m_ref[...] = jnp.full(m_ref.shape, NEG_BIG, F32)
    acc_ref[...] = jnp.zeros(acc_ref.shape, F32)

    def attend(c, near, span=1):
        n = span * C
        neg = jnp.where(sc_ref[pl.ds(c, span)].reshape(n, C) >= th, 0.0, -jnp.inf)
        if near == 0:
            neg = jnp.where(row <= col, neg, -jnp.inf)
        neg_ref[0:n, :] = neg
        for h in range(ATTN_HEADS):
            p2 = (h // 2) * 2 * HEAD_DIM
            lg = jnp.dot(k_ref[0, pl.ds(c, span), :, p2:p2 + 2 * HEAD_DIM].reshape(n, 2 * HEAD_DIM),
                         q2t_ref[2 * HEAD_DIM * h:2 * HEAD_DIM * (h + 1), :],
                         preferred_element_type=F32)
            if near is not None:
                lg = lg + bias_ref[near, h]
            lg = lg + neg_ref[0:n, :]
            lg_ref[h, 0:n, :] = lg
            m_old = m_ref[h:h + 1, :]
            m_new = jnp.maximum(m_old, jnp.max(lg, axis=0, keepdims=True))
            al_ref[h:h + 1, :] = jnp.exp2(m_old - m_new)
            m_ref[h:h + 1, :] = m_new
        for h in range(ATTN_HEADS):
            p_ref[h, 0:n, :] = jnp.exp2(lg_ref[h, 0:n, :] - m_ref[h:h + 1, :]).astype(BF16)
        for h in range(ATTN_HEADS):
            hs = slice(h * V_SLOT, (h + 1) * V_SLOT)
            pv = jnp.dot(vt_ref[0, c, hs, :], p_ref[h, 0:C, :], preferred_element_type=F32)
            for j in range(1, span):
                pv = pv + jnp.dot(vt_ref[0, c + j, hs, :], p_ref[h, j * C:(j + 1) * C, :],
                                  preferred_element_type=F32)
            acc_ref[hs, :] = al_ref[h:h + 1, :] * acc_ref[hs, :] + pv

    n_far = jnp.maximum(i - 1, 0)

    def far_body(cp, carry):
        attend(2 * cp, None, span=FAR_SPAN)
        return carry

    lax.fori_loop(0, lax.shift_right_logical(n_far, 1), far_body, 0)

    @pl.when(n_far % 2 == 1)
    def _():
        attend(n_far - 1, None)

    @pl.when(i >= 1)
    def _():
        attend(i - 1, 1)

    attend(i, 0)

    out_t = jnp.concatenate(
        [acc_ref[h * V_SLOT:h * V_SLOT + HEAD_DIM, :] / acc_ref[h * V_SLOT + HEAD_DIM:h * V_SLOT + HEAD_DIM + 1, :]
         for h in range(ATTN_HEADS)], axis=0)
    o_ref[0] = out_t.T.astype(o_ref.dtype)


def _dsa_attention(qt, iqt, iwt, k, vt, ik, bias):
    B, S, _ = k.shape
    nq = S // CHUNK
    assert nq % 2 == 0
    topk = min(TOPK_MAX, S // 4)
    k4 = k.reshape(B, nq, CHUNK, ATTN_WIDTH)
    ik4 = ik.reshape(B, nq, CHUNK, IDX_DIM)
    return pl.pallas_call(
        functools.partial(_attn_kernel, topk=topk),
        out_shape=jax.ShapeDtypeStruct((B, S, ATTN_WIDTH), BF16),
        grid=(B, nq),
        in_specs=[
            pl.BlockSpec((1, ATTN_WIDTH, CHUNK), lambda b, i: (b, 0, i)),
            pl.BlockSpec((1, IDX_HEADS * IDX_DIM, CHUNK), lambda b, i: (b, 0, i)),
            pl.BlockSpec((1, IDX_HEADS, CHUNK), lambda b, i: (b, 0, i)),
            pl.BlockSpec((1, nq, CHUNK, ATTN_WIDTH), lambda b, i: (b, 0, 0, 0)),
            pl.BlockSpec((1, nq, VT_ROWS, CHUNK), lambda b, i: (b, 0, 0, 0)),
            pl.BlockSpec((1, nq, CHUNK, IDX_DIM), lambda b, i: (b, 0, 0, 0)),
            pl.BlockSpec((2, ATTN_HEADS, CHUNK, CHUNK), lambda b, i: (0, 0, 0, 0)),
        ],
        out_specs=pl.BlockSpec((1, CHUNK, ATTN_WIDTH), lambda b, i: (b, i, 0)),
        scratch_shapes=[
            pltpu.VMEM((nq, CHUNK, CHUNK), F32),
            pltpu.VMEM((8, CHUNK), F32),
            pltpu.VMEM((Q2_ROWS, CHUNK), BF16),
            pltpu.VMEM((FAR_SPAN * CHUNK, CHUNK), F32),
            pltpu.VMEM((ATTN_HEADS, FAR_SPAN * CHUNK, CHUNK), F32),
            pltpu.VMEM((ATTN_HEADS, FAR_SPAN * CHUNK, CHUNK), BF16),
            pltpu.VMEM((ATTN_HEADS, CHUNK), F32),
            pltpu.VMEM((ATTN_HEADS, CHUNK), F32),
            pltpu.VMEM((VT_ROWS, CHUNK), F32),
        ],
        compiler_params=pltpu.CompilerParams(
            dimension_semantics=("arbitrary", "arbitrary"), vmem_limit_bytes=VMEM_LIMIT),
        name="dsa_attn",
    )(qt, iqt, iwt, k4, vt, ik4, bias)


ROUTE_LANES = LANES
ROUTE_ROWS = 40


def _layer_norm(y, g, b):
    mu = jnp.mean(y, axis=-1, keepdims=True)
    yc = y - mu
    var = jnp.mean(yc * yc, axis=-1, keepdims=True)
    return yc * lax.rsqrt(var + LN_EPS) * g + b


def _mix_kernel(attn_ref, u_ref, halo_ref, x_ref, wpool_ref, pscale_ref, wout_ref, g_ref, b_ref,
                wr_ref, br_ref, h_ref, route_ref, rw_ref, *, tm):
    i = pl.program_id(1)
    halo = jnp.where(i > 0, halo_ref[0], 0.0)
    ue = jnp.concatenate([halo, u_ref[0]], axis=0)
    pos = i * tm + lax.broadcasted_iota(I32, (tm, 1), 0)
    mixed = []
    for g, w in enumerate(POOL_WINDOWS):
        gs = slice(g * POOL_GROUP_DIM, (g + 1) * POOL_GROUP_DIM)
        ch = ue[:, gs]
        win = ch
        step = 1
        while step < w:
            win = win + pltpu.roll(win, step, axis=0)
            step *= 2
        cnt = jnp.minimum(pos + 1, w).astype(F32)
        pooled = win[POOL_HALO:] / cnt - ch[POOL_HALO:]
        mg = jnp.dot(pooled.astype(BF16), wpool_ref[g], preferred_element_type=F32)
        mixed.append((mg * pscale_ref[:, gs]).astype(BF16))
    cat = jnp.concatenate([attn_ref[0]] + mixed, axis=-1)
    mix = jnp.dot(cat, wout_ref[...], preferred_element_type=F32)
    h = _layer_norm(DEEPNORM_ALPHA * x_ref[0] + mix, g_ref[...], b_ref[...])
    h_ref[0] = h
    hb = h.astype(BF16)

    lg = jnp.dot(hb, wr_ref[...], preferred_element_type=F32) + br_ref[...]
    lt = lg.T[0:ROUTE_ROWS, :]
    sub = lax.broadcasted_iota(I32, (ROUTE_ROWS, tm), 0).astype(F32)
    gl = jnp.where(sub >= N_EXPERTS, jnp.where(sub < N_EXPERTS + N_GROUPS, lt, -jnp.inf), -jnp.inf)
    ge = jnp.exp(gl - jnp.max(gl, axis=0, keepdims=True))
    pg = ge / jnp.sum(ge, axis=0, keepdims=True)
    pg_top = jnp.max(pg, axis=0, keepdims=True)
    g_row = jnp.min(jnp.where(pg == pg_top, sub, float(ROUTE_LANES)), axis=0, keepdims=True)
    e_lo = (g_row - N_EXPERTS) * EXPERTS_PER_GROUP
    fl = jnp.where(sub >= e_lo, jnp.where(sub < e_lo + EXPERTS_PER_GROUP, lt, -jnp.inf), -jnp.inf)
    fe = jnp.exp(fl - jnp.max(fl, axis=0, keepdims=True))
    pf = fe / jnp.sum(fe, axis=0, keepdims=True)
    p1 = jnp.max(pf, axis=0, keepdims=True)
    i1 = jnp.min(jnp.where(pf == p1, sub, float(ROUTE_LANES)), axis=0, keepdims=True)
    pr = jnp.where(sub == i1, -1.0, jnp.where(fl == -jnp.inf, -1.0, pf))
    p2 = jnp.max(pr, axis=0, keepdims=True)
    i2 = jnp.min(jnp.where(pr == p2, sub, float(ROUTE_LANES)), axis=0, keepdims=True)
    psum = p1 + p2
    w1 = pg_top * p1 / psum
    w2 = pg_top * p2 / psum
    row8 = lax.broadcasted_iota(I32, (8, tm), 0)
    route_ref[...] = jnp.where(row8 == 0, i1, jnp.where(row8 == 1, i2, 0.0)).astype(I32)
    rw_t = jnp.where(sub == 0.0, w1, jnp.where(sub == 1.0, w2, 0.0))
    rw_t = jnp.concatenate([rw_t, jnp.zeros((ROUTE_LANES - ROUTE_ROWS, tm), F32)], axis=0)
    rw_ref[0] = rw_t.T[:, 0:8]


def _mix_norm_route(attn, u, x, w_pool, pool_scale, w_out, ln_g, ln_b, w_r1, b_r1, w_r2, b_r2, *, tm=512):
    B, S, D = x.shape
    wr = jnp.concatenate([w_r2, w_r1, jnp.zeros((D, ROUTE_LANES - N_EXPERTS - N_GROUPS), w_r1.dtype)],
                         axis=1).astype(BF16)
    br = jnp.concatenate([b_r2, b_r1, jnp.zeros((ROUTE_LANES - N_EXPERTS - N_GROUPS,), b_r1.dtype)])[None, :]
    hpt = tm // POOL_HALO
    return pl.pallas_call(
        functools.partial(_mix_kernel, tm=tm),
        out_shape=(jax.ShapeDtypeStruct((B, S, D), F32),
                   jax.ShapeDtypeStruct((8, B * S), I32),
                   jax.ShapeDtypeStruct((B, S, 8), F32)),
        grid=(B, S // tm),
        in_specs=[
            pl.BlockSpec((1, tm, ATTN_WIDTH), lambda b, i: (b, i, 0)),
            pl.BlockSpec((1, tm, POOL_WIDTH), lambda b, i: (b, i, 0)),
            pl.BlockSpec((1, POOL_HALO, POOL_WIDTH), lambda b, i: (b, jnp.maximum(i * hpt - 1, 0), 0)),
            pl.BlockSpec((1, tm, D), lambda b, i: (b, i, 0)),
            pl.BlockSpec((len(POOL_WINDOWS), POOL_GROUP_DIM, POOL_GROUP_DIM), lambda b, i: (0, 0, 0)),
            pl.BlockSpec((1, POOL_WIDTH), lambda b, i: (0, 0)),
            pl.BlockSpec((D, D), lambda b, i: (0, 0)),
            pl.BlockSpec((1, D), lambda b, i: (0, 0)),
            pl.BlockSpec((1, D), lambda b, i: (0, 0)),
            pl.BlockSpec((D, ROUTE_LANES), lambda b, i: (0, 0)),
            pl.BlockSpec((1, ROUTE_LANES), lambda b, i: (0, 0)),
        ],
        out_specs=(pl.BlockSpec((1, tm, D), lambda b, i: (b, i, 0)),
                   pl.BlockSpec((8, tm), lambda b, i: (0, b * (S // tm) + i)),
                   pl.BlockSpec((1, tm, 8), lambda b, i: (b, i, 0))),
        compiler_params=pltpu.CompilerParams(
            dimension_semantics=("arbitrary", "arbitrary"), vmem_limit_bytes=VMEM_LIMIT),
        name="mix_norm",
    )(attn, u, u, x, w_pool.astype(BF16), pool_scale[None, :], w_out.astype(BF16),
      ln_g[None, :], ln_b[None, :], wr, br)


MOE_TM = 512
PLAN_TB = 1024
GATHER_UNROLL = 8


def _moe_rows(n_tokens):
    return 2 * n_tokens + N_EXPERTS * MOE_TM


def _plan_kernel(route_ref, pos_ref, tile_ref, cnt_ref, off_ref, carry_ref, *, n_tile_lanes):
    ph = pl.program_id(0)
    b = pl.program_id(1)
    tb = route_ref.shape[1]
    esub = lax.broadcasted_iota(I32, (N_EXPERTS, tb), 0)
    a1 = jnp.where(esub == route_ref[0:1, :], 1.0, 0.0)
    a2 = jnp.where(esub == route_ref[1:2, :], 1.0, 0.0)
    a = a1 + a2
    n_here = jnp.sum(a, axis=1, keepdims=True)

    @pl.when(jnp.logical_and(ph == 0, b == 0))
    def _():
        cnt_ref[...] = jnp.zeros(cnt_ref.shape, F32)

    @pl.when(ph == 0)
    def _():
        cnt_ref[...] += n_here

    @pl.when(jnp.logical_and(ph == 1, b == 0))
    def _():
        n_tile = jnp.floor((cnt_ref[...] + (MOE_TM - 1)) * (1.0 / MOE_TM))
        er = lax.broadcasted_iota(I32, (N_EXPERTS, N_EXPERTS), 0)
        ec = lax.broadcasted_iota(I32, (N_EXPERTS, N_EXPERTS), 1)
        before = jnp.where(ec < er, 1.0, 0.0).astype(BF16)
        t_off = jnp.dot(before, n_tile.astype(BF16), preferred_element_type=F32)
        off_ref[...] = t_off * MOE_TM
        carry_ref[...] = jnp.zeros(carry_ref.shape, F32)
        j = lax.broadcasted_iota(I32, (N_EXPERTS, n_tile_lanes), 1).astype(F32)
        owner = jnp.sum(jnp.where(t_off[:, 0:1] <= j, 1.0, 0.0), axis=0, keepdims=True) - 1.0
        total = jnp.sum(n_tile[:, 0:1], axis=0, keepdims=True)
        used = jnp.where(j[0:1, :] < total, 1.0, 0.0)
        starts_next = jnp.sum(jnp.where(t_off[:, 0:1] == j + 1.0, 1.0, 0.0), axis=0, keepdims=True)
        ragged = jnp.where(j[0:1, :] + 1.0 >= total, 1.0, jnp.where(starts_next > 0.0, 1.0, 0.0))
        row8 = lax.broadcasted_iota(I32, (8, n_tile_lanes), 0)
        tile_ref[...] = jnp.where(row8 == 0, owner,
                                  jnp.where(row8 == 1, used, jnp.where(row8 == 2, ragged, 0.0))).astype(I32)

    @pl.when(ph == 1)
    def _():
        tr = lax.broadcasted_iota(I32, (tb, tb), 0)
        tc = lax.broadcasted_iota(I32, (tb, tb), 1)
        earlier = jnp.where(tr < tc, 1.0, 0.0).astype(BF16)
        seen = jnp.dot(a.astype(BF16), earlier, preferred_element_type=F32)
        dest = seen + carry_ref[:, 0:1] + off_ref[:, 0:1]
        p1 = jnp.sum(a1 * dest, axis=0, keepdims=True)
        p2 = jnp.sum(a2 * dest, axis=0, keepdims=True)
        row8 = lax.broadcasted_iota(I32, (8, tb), 0)
        pos_ref[...] = jnp.where(row8 == 0, p1, jnp.where(row8 == 1, p2, 0.0)).astype(I32)
        carry_ref[...] += n_here


def _moe_plan(route_t):
    _, T = route_t.shape
    n_tiles = _moe_rows(T) // MOE_TM
    n_tile_lanes = -(-n_tiles // LANES) * LANES
    nb = T // PLAN_TB
    return pl.pallas_call(
        functools.partial(_plan_kernel, n_tile_lanes=n_tile_lanes),
        out_shape=(jax.ShapeDtypeStruct((8, T), I32), jax.ShapeDtypeStruct((8, n_tile_lanes), I32)),
        grid=(2, nb),
        in_specs=[pl.BlockSpec((8, PLAN_TB), lambda ph, b: (0, b))],
        out_specs=(pl.BlockSpec((8, PLAN_TB), lambda ph, b: (0, b * ph)),
                   pl.BlockSpec((8, n_tile_lanes), lambda ph, b: (0, 0))),
        scratch_shapes=[pltpu.VMEM((N_EXPERTS, LANES), F32),
                        pltpu.VMEM((N_EXPERTS, LANES), F32),
                        pltpu.VMEM((N_EXPERTS, LANES), F32)],
        compiler_params=pltpu.CompilerParams(dimension_semantics=("arbitrary", "arbitrary")),
        name="moe_plan",
    )(route_t)


def _row_copy(src_ref, src_row, dst_ref, dst_row, sem):
    return pltpu.make_async_copy(src_ref.at[pl.ds(src_row, 1)], dst_ref.at[pl.ds(dst_row, 1)], sem)


def _dispatch_kernel(ragged_ref, p1_ref, p2_ref, h_ref, out_hbm, zero_ref, sem, zero_sem, *, tb, n_tiles):
    @pl.when(pl.program_id(0) == 0)
    def _():
        zero_ref[...] = jnp.zeros(zero_ref.shape, zero_ref.dtype)

        def zero_tile(j):
            return pltpu.make_async_copy(zero_ref, out_hbm.at[pl.ds(j * MOE_TM, MOE_TM)], zero_sem)

        for j in range(n_tiles):
            @pl.when(ragged_ref[j] == 1)
            def _():
                zero_tile(j).start()
        for j in range(n_tiles):
            @pl.when(ragged_ref[j] == 1)
            def _():
                zero_tile(j).wait()

    def start(t, carry):
        _row_copy(h_ref, t, out_hbm, p1_ref[t], sem).start()
        _row_copy(h_ref, t, out_hbm, p2_ref[t], sem).start()
        return carry

    lax.fori_loop(0, tb, start, 0, unroll=GATHER_UNROLL)

    def wait(t, carry):
        _row_copy(h_ref, 0, out_hbm, 0, sem).wait()
        _row_copy(h_ref, 0, out_hbm, 0, sem).wait()
        return carry

    lax.fori_loop(0, tb, wait, 0, unroll=GATHER_UNROLL)


def _moe_dispatch(h, pos1, pos2, ragged, *, tb=1024):
    T, D = h.shape
    rows = _moe_rows(T)
    n_tiles = rows // MOE_TM
    return pl.pallas_call(
        functools.partial(_dispatch_kernel, tb=tb, n_tiles=n_tiles),
        out_shape=jax.ShapeDtypeStruct((rows, D), h.dtype),
        grid=(T // tb,),
        in_specs=[pl.BlockSpec(memory_space=pltpu.SMEM),
                  pl.BlockSpec((tb,), lambda i: (i,), memory_space=pltpu.SMEM),
                  pl.BlockSpec((tb,), lambda i: (i,), memory_space=pltpu.SMEM),
                  pl.BlockSpec((tb, D), lambda i: (i, 0))],
        out_specs=pl.BlockSpec(memory_space=pl.ANY),
        scratch_shapes=[pltpu.VMEM((MOE_TM, D), h.dtype), pltpu.SemaphoreType.DMA(()),
                        pltpu.SemaphoreType.DMA(())],
        compiler_params=pltpu.CompilerParams(dimension_semantics=("arbitrary",), vmem_limit_bytes=VMEM_LIMIT),
        name="moe_dispatch",
    )(ragged, pos1, pos2, h)


def _ffn_kernel(owner_ref, used_ref, x_ref, wg_ref, wu_ref, wd_ref, o_ref):
    j = pl.program_id(0)

    @pl.when(used_ref[j] == 1)
    def _():
        xb = x_ref[...].astype(BF16)
        gate = jnp.dot(xb, wg_ref[0].astype(BF16), preferred_element_type=F32)
        up = jnp.dot(xb, wu_ref[0].astype(BF16), preferred_element_type=F32)
        a = gate * jax.nn.sigmoid(gate) * up
        o_ref[...] = jnp.dot(a.astype(BF16), wd_ref[0].astype(BF16), preferred_element_type=F32)

    @pl.when(used_ref[j] == 0)
    def _():
        o_ref[...] = jnp.zeros(o_ref.shape, F32)


def _moe_ffn(xs, owner, used, w_gate, w_up, w_down):
    rows, D = xs.shape
    E, _, F = w_gate.shape
    return pl.pallas_call(
        _ffn_kernel,
        out_shape=jax.ShapeDtypeStruct((rows, D), F32),
        grid_spec=pltpu.PrefetchScalarGridSpec(
            num_scalar_prefetch=2,
            grid=(rows // MOE_TM,),
            in_specs=[pl.BlockSpec((MOE_TM, D), lambda j, ow, us: (j, 0)),
                      pl.BlockSpec((1, D, F), lambda j, ow, us: (ow[j], 0, 0)),
                      pl.BlockSpec((1, D, F), lambda j, ow, us: (ow[j], 0, 0)),
                      pl.BlockSpec((1, F, D), lambda j, ow, us: (ow[j], 0, 0))],
            out_specs=pl.BlockSpec((MOE_TM, D), lambda j, ow, us: (j, 0)),
        ),
        compiler_params=pltpu.CompilerParams(dimension_semantics=("arbitrary",), vmem_limit_bytes=VMEM_LIMIT),
        name="moe_ffn",
    )(owner, used, xs, w_gate, w_up, w_down)


def _combine_kernel(p1_ref, p2_ref, p1n_ref, p2n_ref, rw_ref, h_ref, g_ref, b_ref, y_hbm, o_ref,
                    buf_ref, sem, *, tm):
    s = pl.program_id(0)
    slot = s % 2

    def fetch(pa_ref, pb_ref, into):
        def start(t, carry):
            pltpu.make_async_copy(y_hbm.at[pl.ds(pa_ref[t], 1)], buf_ref.at[into, 0, pl.ds(t, 1)],
                                  sem.at[into]).start()
            pltpu.make_async_copy(y_hbm.at[pl.ds(pb_ref[t], 1)], buf_ref.at[into, 1, pl.ds(t, 1)],
                                  sem.at[into]).start()
            return carry
        lax.fori_loop(0, tm, start, 0, unroll=GATHER_UNROLL)

    @pl.when(s == 0)
    def _():
        fetch(p1_ref, p2_ref, 0)

    @pl.when(s + 1 < pl.num_programs(0))
    def _():
        fetch(p1n_ref, p2n_ref, 1 - slot)

    def wait(t, carry):
        for half in range(2):
            pltpu.make_async_copy(y_hbm.at[pl.ds(0, 1)], buf_ref.at[slot, half, pl.ds(0, 1)], sem.at[slot]).wait()
        return carry

    lax.fori_loop(0, tm, wait, 0, unroll=GATHER_UNROLL)
    y = rw_ref[:, 0:1] * buf_ref[slot, 0] + rw_ref[:, 1:2] * buf_ref[slot, 1]
    o_ref[...] = _layer_norm(DEEPNORM_ALPHA * h_ref[...] + y, g_ref[...], b_ref[...])


def _moe_combine(h, y_sorted, pos1, pos2, rw, ln_g, ln_b, *, tm=512):
    T, D = h.shape
    last = T // tm - 1
    smem = lambda imap: pl.BlockSpec((tm,), imap, memory_space=pltpu.SMEM)
    return pl.pallas_call(
        functools.partial(_combine_kernel, tm=tm),
        out_shape=jax.ShapeDtypeStruct((T, D), F32),
        grid=(T // tm,),
        in_specs=[smem(lambda i: (i,)), smem(lambda i: (i,)),
                  smem(lambda i: (jnp.minimum(i + 1, last),)), smem(lambda i: (jnp.minimum(i + 1, last),)),
                  pl.BlockSpec((tm, 8), lambda i: (i, 0)),
                  pl.BlockSpec((tm, D), lambda i: (i, 0)),
                  pl.BlockSpec((1, D), lambda i: (0, 0)),
                  pl.BlockSpec((1, D), lambda i: (0, 0)),
                  pl.BlockSpec(memory_space=pl.ANY)],
        out_specs=pl.BlockSpec((tm, D), lambda i: (i, 0)),
        scratch_shapes=[pltpu.VMEM((2, 2, tm, D), F32), pltpu.SemaphoreType.DMA((2,))],
        compiler_params=pltpu.CompilerParams(dimension_semantics=("arbitrary",), vmem_limit_bytes=VMEM_LIMIT),
        name="moe_combine",
    )(pos1, pos2, pos1, pos2, rw, h, ln_g[None, :], ln_b[None, :], y_sorted)


def _moe_norm(h, route_t, rw, w_gate, w_up, w_down, ln_g, ln_b):
    pos, tiles = _moe_plan(route_t)
    n_tiles = _moe_rows(h.shape[0]) // MOE_TM
    xs = _moe_dispatch(h, pos[0], pos[1], tiles[2, :n_tiles])
    ys = _moe_ffn(xs, tiles[0, :n_tiles], tiles[1, :n_tiles], w_gate, w_up, w_down)
    return _moe_combine(h, ys, pos[0], pos[1], rw, ln_g, ln_b)


def kernel(x, w_in, w_pool, pool_scale, w_out, rel_bias, ln1_g, ln1_b, w_r1, b_r1, w_r2, b_r2,
           w_gate, w_up, w_down, ln2_g, ln2_b):
    B, S, D = x.shape
    assert w_in.shape[0] == DEPTH == 1 and S % CHUNK == 0
    bias = _bias_tiles(rel_bias)
    qt, vt, iqt, iwt, k, ik, u = _in_proj(x, w_in[0])
    attn = _dsa_attention(qt, iqt, iwt, k, vt, ik, bias)
    h, route_t, rw = _mix_norm_route(attn, u, x, w_pool[0], pool_scale[0], w_out[0], ln1_g[0], ln1_b[0],
                                     w_r1[0], b_r1[0], w_r2[0], b_r2[0])
    out = _moe_norm(h.reshape(B * S, D), route_t, rw.reshape(B * S, 8),
                    w_gate[0], w_up[0], w_down[0], ln2_g[0], ln2_b[0])
    return out.reshape(B, S, D)
```

```python
import functools
import math

import numpy as np
import jax
import jax.numpy as jnp
from jax import lax
from jax.experimental import pallas as pl
from jax.experimental.pallas import tpu as pltpu

F32 = jnp.float32
BF16 = jnp.bfloat16
I32 = jnp.int32

ATTN_HEADS = 8
HEAD_DIM = 64
ATTN_WIDTH = ATTN_HEADS * HEAD_DIM
IDX_HEADS = 8
IDX_DIM = 64
TOPK_MAX = 256
POOL_WINDOWS = (2, 4, 8, 16)
POOL_GROUP_DIM = 128
POOL_WIDTH = len(POOL_WINDOWS) * POOL_GROUP_DIM
POOL_HALO = 16
REL_BUCKETS = 32
REL_MAX_DIST = 128
N_GROUPS = 4
EXPERTS_PER_GROUP = 8
N_EXPERTS = N_GROUPS * EXPERTS_PER_GROUP
LN_EPS = 1e-5
DEPTH = 1
DEEPNORM_ALPHA = (2 * DEPTH) ** 0.25
LOG2E = math.log2(math.e)

LANES = 128
SUBLANES = 8
CHUNK = 256
INT_MAX = 2 ** 31 - 1
BF16_ROWS = 16
NEG_BIG = -1e30
VMEM_LIMIT = 56 * 1024 * 1024


def _rel_bucket_table(n):
    max_exact = REL_BUCKETS // 2
    d = np.arange(n)
    nf = np.maximum(d, 1).astype(np.float32)
    ratio = np.log(nf / np.float32(max_exact)) / np.float32(math.log(REL_MAX_DIST / max_exact))
    large = max_exact + (ratio * np.float32(REL_BUCKETS - max_exact)).astype(np.int32)
    large = np.minimum(large, REL_BUCKETS - 1)
    return np.where(d < max_exact, d, large).astype(np.int32)


def _near_bucket_tiles():
    tbl = _rel_bucket_table(2 * CHUNK)
    a = np.arange(CHUNK)[:, None]
    b = np.arange(CHUNK)[None, :]
    tiles = [tbl[np.maximum(delta * CHUNK + b - a, 0)] for delta in (0, 1)]
    return np.stack(tiles).astype(np.int32)


FAR_BUCKET = REL_BUCKETS - 1
assert int(_rel_bucket_table(2 * CHUNK)[CHUNK + 1:].min()) == FAR_BUCKET


def _bias_kernel(rb_ref, bucket_ref, o_ref):
    h = pl.program_id(1)
    bk = bucket_ref[0]
    far = rb_ref[FAR_BUCKET, h]
    acc = jnp.zeros(bk.shape, F32)
    for n in range(REL_BUCKETS):
        acc = jnp.where(bk == n, rb_ref[n, h] - far, acc)
    o_ref[0, 0] = acc * LOG2E


def _bias_tiles(rel_bias):
    buckets = jnp.asarray(_near_bucket_tiles())
    return pl.pallas_call(
        _bias_kernel,
        out_shape=jax.ShapeDtypeStruct((2, ATTN_HEADS, CHUNK, CHUNK), F32),
        grid=(2, ATTN_HEADS),
        in_specs=[pl.BlockSpec(memory_space=pltpu.SMEM),
                  pl.BlockSpec((1, CHUNK, CHUNK), lambda d, h: (d, 0, 0))],
        out_specs=pl.BlockSpec((1, 1, CHUNK, CHUNK), lambda d, h: (d, h, 0, 0)),
        name="bias_tiles",
    )(rel_bias, buckets)


V_SLOT = HEAD_DIM + BF16_ROWS
VT_ROWS = ATTN_HEADS * V_SLOT
WT_Q, WT_V, WT_IQ, WT_IW = 0, ATTN_WIDTH, ATTN_WIDTH + VT_ROWS, 2 * ATTN_WIDTH + VT_ROWS
WT_ROWS = WT_IW + BF16_ROWS
WN_K, WN_U, WN_IK = 0, ATTN_WIDTH, ATTN_WIDTH + POOL_WIDTH
WN_COLS = WN_IK + LANES


def _proj_kernel(x_ref, wt_ref, wn_ref, qt_ref, vt_ref, iqt_ref, iwt_ref, k_ref, ik_ref, u_ref, *, tm):
    xb = x_ref[0].astype(BF16)
    t = lax.dot_general(wt_ref[...], xb, (((1,), (1,)), ((), ())),
                        preferred_element_type=F32)
    qt_ref[0] = (t[WT_Q:WT_V] * (HEAD_DIM ** -0.5 * LOG2E)).astype(BF16)
    iqt_ref[0] = (t[WT_IQ:WT_IW] * (IDX_DIM ** -0.5)).astype(BF16)
    iwt_ref[0] = t[WT_IW:WT_IW + IDX_HEADS] * (IDX_HEADS ** -0.5)
    slot_row = lax.broadcasted_iota(I32, (VT_ROWS, tm), 0) % V_SLOT
    vt = jnp.where(slot_row >= HEAD_DIM, 1.0, t[WT_V:WT_IQ]).astype(BF16)
    for j in range(tm // CHUNK):
        vt_ref[0, j] = vt[:, j * CHUNK:(j + 1) * CHUNK]
    n = jnp.dot(xb, wn_ref[...], preferred_element_type=F32)
    k_ref[0] = n[:, WN_K:WN_U].astype(BF16)
    u_ref[0] = n[:, WN_U:WN_IK]
    ik_ref[0] = n[:, WN_IK:WN_IK + IDX_DIM].astype(BF16)


def _in_proj(x, w_in, *, tm=512):
    B, S, D = x.shape
    q_off, k_off, v_off = 0, ATTN_WIDTH, 2 * ATTN_WIDTH
    iq_off = 3 * ATTN_WIDTH
    ik_off = iq_off + IDX_HEADS * IDX_DIM
    iw_off = ik_off + IDX_DIM
    p_off = iw_off + IDX_HEADS
    wv = w_in[:, v_off:iq_off].T.reshape(ATTN_HEADS, HEAD_DIM, D)
    wv = jnp.pad(wv, ((0, 0), (0, V_SLOT - HEAD_DIM), (0, 0))).reshape(VT_ROWS, D)
    wt = jnp.concatenate([
        w_in[:, q_off:k_off].T, wv, w_in[:, iq_off:ik_off].T, w_in[:, iw_off:p_off].T,
        jnp.zeros((WT_ROWS - WT_IW - IDX_HEADS, D), w_in.dtype)], axis=0).astype(BF16)
    wn = jnp.concatenate([
        w_in[:, k_off:v_off], w_in[:, p_off:], w_in[:, ik_off:iw_off],
        jnp.zeros((D, WN_COLS - WN_IK - IDX_DIM), w_in.dtype)], axis=1).astype(BF16)
    nt = S // tm
    cpt = tm // CHUNK
    outs = pl.pallas_call(
        functools.partial(_proj_kernel, tm=tm),
        out_shape=(
            jax.ShapeDtypeStruct((B, ATTN_WIDTH, S), BF16),
            jax.ShapeDtypeStruct((B, S // CHUNK, VT_ROWS, CHUNK), BF16),
            jax.ShapeDtypeStruct((B, IDX_HEADS * IDX_DIM, S), BF16),
            jax.ShapeDtypeStruct((B, IDX_HEADS, S), F32),
            jax.ShapeDtypeStruct((B, S, ATTN_WIDTH), BF16),
            jax.ShapeDtypeStruct((B, S, IDX_DIM), BF16),
            jax.ShapeDtypeStruct((B, S, POOL_WIDTH), F32),
        ),
        grid=(B, nt),
        in_specs=[pl.BlockSpec((1, tm, D), lambda b, i: (b, i, 0)),
                  pl.BlockSpec((WT_ROWS, D), lambda b, i: (0, 0)),
                  pl.BlockSpec((D, WN_COLS), lambda b, i: (0, 0))],
        out_specs=(
            pl.BlockSpec((1, ATTN_WIDTH, tm), lambda b, i: (b, 0, i)),
            pl.BlockSpec((1, cpt, VT_ROWS, CHUNK), lambda b, i: (b, i, 0, 0)),
            pl.BlockSpec((1, IDX_HEADS * IDX_DIM, tm), lambda b, i: (b, 0, i)),
            pl.BlockSpec((1, IDX_HEADS, tm), lambda b, i: (b, 0, i)),
            pl.BlockSpec((1, tm, ATTN_WIDTH), lambda b, i: (b, i, 0)),
            pl.BlockSpec((1, tm, IDX_DIM), lambda b, i: (b, i, 0)),
            pl.BlockSpec((1, tm, POOL_WIDTH), lambda b, i: (b, i, 0)),
        ),
        compiler_params=pltpu.CompilerParams(
            dimension_semantics=("arbitrary", "arbitrary"), vmem_limit_bytes=VMEM_LIMIT),
        name="in_proj",
    )(x, wt, wn)
    return outs


SNAP_ROUND = 8
SNAP_PERIOD = 3
WARM_ROUNDS = 7
MAX_ROUNDS = SNAP_ROUND + SNAP_PERIOD * 17
ROW_LO, ROW_HI, ROW_CNT, ROW_PROBE = 0, 1, 2, 3
FAR_SPAN = 2


def _order_key(bits):
    return bits ^ ((bits >> 31) & 0x7FFFFFFF)


Q2_ROWS = 2 * ATTN_WIDTH


def _attn_kernel(qt_ref, iqt_ref, iwt_ref, k_ref, vt_ref, ik_ref, bias_ref, o_ref,
                 sc_ref, st_ref, q2t_ref, neg_ref, lg_ref, p_ref, m_ref, al_ref, acc_ref,
                 *, topk):
    i = pl.program_id(1)
    nch = i + 1
    C = CHUNK

    @pl.when(jnp.logical_and(pl.program_id(0) == 0, i == 0))
    def _():
        q2t_ref[...] = jnp.zeros(q2t_ref.shape, BF16)

    for h in range(ATTN_HEADS):
        lo = 2 * HEAD_DIM * h + HEAD_DIM * (h % 2)
        q2t_ref[lo:lo + HEAD_DIM, :] = qt_ref[0, h * HEAD_DIM:(h + 1) * HEAD_DIM, :]
    row = lax.broadcasted_iota(I32, (C, C), 0)
    col = lax.broadcasted_iota(I32, (C, C), 1)

    def fold_rows(op, x):
        return op(x.reshape(C // SUBLANES, SUBLANES, C), axis=0)

    n_pair = lax.shift_right_logical(nch + 1, 1)

    def score_body(cp, carry):
        smin, smax = carry
        for c in (2 * cp, 2 * cp + 1):
            ikc = ik_ref[0, c]
            s = jnp.zeros((C, C), F32)
            for j in range(IDX_HEADS):
                d = jnp.dot(ikc, iqt_ref[0, j * IDX_DIM:(j + 1) * IDX_DIM, :],
                            preferred_element_type=F32)
                s = s + iwt_ref[0, j:j + 1, :] * jnp.maximum(d, 0.0)
            causal = (c * C + row) <= (i * C + col)
            sc_ref[c] = jnp.where(causal, s, -jnp.inf)
            smin = jnp.minimum(smin, fold_rows(jnp.min, jnp.where(causal, s, jnp.inf)))
            smax = jnp.maximum(smax, fold_rows(jnp.max, jnp.where(causal, s, -jnp.inf)))
        return smin, smax

    smin, smax = lax.fori_loop(0, n_pair, score_body,
                               (jnp.full((SUBLANES, C), jnp.inf, F32), jnp.full((SUBLANES, C), -jnp.inf, F32)))
    smin = jnp.min(smin, axis=0, keepdims=True)
    smax = jnp.max(smax, axis=0, keepdims=True)

    pos = i * C + lax.broadcasted_iota(I32, (1, C), 1)
    n_keys = (pos + 1).astype(F32)
    k_eff = jnp.minimum(pos + 1, topk).astype(F32)

    def count_ge(mid):
        def body(cp, acc):
            for c in (2 * cp, 2 * cp + 1):
                acc = acc + fold_rows(jnp.sum, jnp.where(sc_ref[c] >= mid, 1.0, 0.0))
            return acc
        part = lax.fori_loop(0, n_pair, body, jnp.zeros((SUBLANES, C), F32))
        return jnp.sum(part, axis=0, keepdims=True)

    def load_state():
        return (st_ref[ROW_LO:ROW_LO + 1, :], st_ref[ROW_HI:ROW_HI + 1, :],
                st_ref[ROW_CNT:ROW_CNT + 1, :], st_ref[ROW_PROBE:ROW_PROBE + 1, :])

    def store_state(lo, hi, cnt_lo, probe):
        st_ref[ROW_LO:ROW_LO + 1, :] = lo
        st_ref[ROW_HI:ROW_HI + 1, :] = hi
        st_ref[ROW_CNT:ROW_CNT + 1, :] = cnt_lo
        st_ref[ROW_PROBE:ROW_PROBE + 1, :] = probe

    def open_cols(lo, hi, cnt_lo):
        return jnp.where(cnt_lo > k_eff, jnp.where(hi > lo, 1.0, 0.0), 0.0)

    def bisect(halve_image, lo, hi, cnt_lo, probe):
        klo = _order_key(pltpu.bitcast(lo, I32))
        khi = _order_key(pltpu.bitcast(hi, I32))
        i_mid = pltpu.bitcast(_order_key((klo >> 1) + (khi >> 1) + (klo & khi & 1)), F32)
        mid = jnp.where(halve_image, i_mid, 0.5 * lo + 0.5 * hi)
        probing = probe > lo
        mid = jnp.where(probing, probe, mid)
        ok = open_cols(lo, hi, cnt_lo) * jnp.where(mid > lo, jnp.where(mid < hi, 1.0, 0.0), 0.0)
        cnt = count_ge(mid)
        up = ok * jnp.where(cnt >= k_eff, 1.0, 0.0)
        dn = ok - up
        new_hi = jnp.where(dn + up * jnp.where(probing, 1.0, 0.0) > 0.0, mid, hi)
        return (jnp.where(up > 0.0, mid, lo), new_hi, jnp.where(up > 0.0, cnt, cnt_lo),
                jnp.full((1, C), -jnp.inf, F32))

    def snap(lo, hi, cnt_lo, probe):
        def body(c, carry):
            vmin, vmax = carry
            sc = sc_ref[c]
            vmin = jnp.minimum(vmin, fold_rows(jnp.min, jnp.where(sc >= lo, sc, jnp.inf)))
            vmax = jnp.maximum(vmax, fold_rows(jnp.max, jnp.where(sc < hi, sc, -jnp.inf)))
            return vmin, vmax
        vmin, vmax = lax.fori_loop(0, nch, body, (jnp.full((SUBLANES, C), jnp.inf, F32),
                                                  jnp.full((SUBLANES, C), -jnp.inf, F32)))
        vmin = jnp.min(vmin, axis=0, keepdims=True)
        vmax = jnp.max(vmax, axis=0, keepdims=True)
        is_open = open_cols(lo, hi, cnt_lo) > 0.0
        single = vmin >= vmax
        return (jnp.where(is_open, vmin, lo),
                jnp.where(is_open, jnp.where(single, vmin, hi), hi),
                cnt_lo,
                jnp.where(is_open, jnp.where(single, -jnp.inf, vmax), -jnp.inf))

    def search_cond(st):
        rnd, n_open = st
        return jnp.logical_and(rnd < MAX_ROUNDS, n_open > 0.0)

    def search_body(st):
        rnd, _ = st
        late = rnd >= SNAP_ROUND
        do_snap = jnp.logical_and(late, (rnd - SNAP_ROUND) % SNAP_PERIOD == 0)

        @pl.when(do_snap)
        def _():
            store_state(*snap(*load_state()))

        @pl.when(jnp.logical_not(do_snap))
        def _():
            store_state(*bisect(False, *load_state()))

        store_state(*bisect(late, *load_state()))
        lo, hi, cnt_lo, _ = load_state()
        return rnd + 1, jnp.max(open_cols(lo, hi, cnt_lo))

    kmax = _order_key(pltpu.bitcast(smax, I32))
    hi0 = pltpu.bitcast(_order_key(jnp.where(kmax == INT_MAX, kmax, kmax + 1)), F32)
    store_state(smin, hi0, n_keys, jnp.full((1, C), -jnp.inf, F32))

    @pl.when(jnp.max(open_cols(smin, hi0, n_keys)) > 0.0)
    def _():
        def warm(r, carry):
            store_state(*bisect(False, *load_state()))
            store_state(*bisect(False, *load_state()))
            return carry
        lax.fori_loop(0, WARM_ROUNDS, warm, 0)

    lo_w, hi_w, cnt_w, _ = load_state()
    lax.while_loop(search_cond, search_body, (jnp.int32(WARM_ROUNDS), jnp.max(open_cols(lo_w, hi_w, cnt_w))))
    th, _, cnt_th, _ = load_state()

    surplus = cnt_th - k_eff

    @pl.when(jnp.max(surplus) > 0.0)
    def _():
        later = jnp.where(col > row, 1.0, 0.0).astype(BF16)

        def drop_body(r, after):
            cp = n_pair - 1 - r
            for c in (2 * cp + 1, 2 * cp):
                sc = sc_ref[c]
                tied = jnp.where(sc == th, 1.0, 0.0)
                follow = jnp.dot(later, tied.astype(BF16), preferred_element_type=F32) + after
                sc_ref[c] = jnp.where(tied * jnp.where(follow < surplus, 1.0, 0.0) > 0.0, -jnp.inf, sc)
                after = after + jnp.sum(fold_rows(jnp.sum, tied), axis=0, keepdims=True)
            return after

        lax.fori_loop(0, n_pair, drop_body, jnp.zeros((1, C), F32))

    m_ref[...] = jnp.full(m_ref.shape, NEG_BIG, F32)
    acc_ref[...] = jnp.zeros(acc_ref.shape, F32)

    def attend(c, near, span=1):
        n = span * C
        neg = jnp.where(sc_ref[pl.ds(c, span)].reshape(n, C) >= th, 0.0, -jnp.inf)
        if near is not None:
            diag = jnp.where(row <= col, neg[n - C:, :], -jnp.inf)
            neg = diag if span == 1 else jnp.concatenate([neg[:n - C, :], diag], axis=0)
        neg_ref[0:n, :] = neg
        for h in range(ATTN_HEADS):
            p2 = (h // 2) * 2 * HEAD_DIM
            lg = jnp.dot(k_ref[0, pl.ds(c, span), :, p2:p2 + 2 * HEAD_DIM].reshape(n, 2 * HEAD_DIM),
                         q2t_ref[2 * HEAD_DIM * h:2 * HEAD_DIM * (h + 1), :],
                         preferred_element_type=F32)
            if near is not None:
                lg = lg + jnp.concatenate([bias_ref[t, h] for t in near], axis=0)
            lg = lg + neg_ref[0:n, :]
            lg_ref[h, 0:n, :] = lg
            m_old = m_ref[h:h + 1, :]
            m_new = jnp.maximum(m_old, jnp.max(lg, axis=0, keepdims=True))
            al_ref[h:h + 1, :] = jnp.exp2(m_old - m_new)
            m_ref[h:h + 1, :] = m_new
        for h in range(ATTN_HEADS):
            p_ref[h, 0:n, :] = jnp.exp2(lg_ref[h, 0:n, :] - m_ref[h:h + 1, :]).astype(BF16)
        for h in range(ATTN_HEADS):
            hs = slice(h * V_SLOT, (h + 1) * V_SLOT)
            pv = jnp.dot(vt_ref[0, c, hs, :], p_ref[h, 0:C, :], preferred_element_type=F32)
            for j in range(1, span):
                pv = pv + jnp.dot(vt_ref[0, c + j, hs, :], p_ref[h, j * C:(j + 1) * C, :],
                                  preferred_element_type=F32)
            acc_ref[hs, :] = al_ref[h:h + 1, :] * acc_ref[hs, :] + pv

    n_far = jnp.maximum(i - 1, 0)

    def far_body(cp, carry):
        attend(2 * cp, None, span=FAR_SPAN)
        return carry

    lax.fori_loop(0, lax.shift_right_logical(n_far, 1), far_body, 0)

    @pl.when(n_far % 2 == 1)
    def _():
        attend(n_far - 1, None)

    @pl.when(i >= 1)
    def _():
        attend(i - 1, (1, 0), span=FAR_SPAN)

    @pl.when(i == 0)
    def _():
        attend(0, (0,))

    out_t = jnp.concatenate(
        [acc_ref[h * V_SLOT:h * V_SLOT + HEAD_DIM, :] / acc_ref[h * V_SLOT + HEAD_DIM:h * V_SLOT + HEAD_DIM + 1, :]
         for h in range(ATTN_HEADS)], axis=0)
    o_ref[0] = out_t.T.astype(o_ref.dtype)


def _dsa_attention(qt, iqt, iwt, k, vt, ik, bias):
    B, S, _ = k.shape
    nq = S // CHUNK
    assert nq % 2 == 0
    topk = min(TOPK_MAX, S // 4)
    k4 = k.reshape(B, nq, CHUNK, ATTN_WIDTH)
    ik4 = ik.reshape(B, nq, CHUNK, IDX_DIM)
    return pl.pallas_call(
        functools.partial(_attn_kernel, topk=topk),
        out_shape=jax.ShapeDtypeStruct((B, S, ATTN_WIDTH), BF16),
        grid=(B, nq),
        in_specs=[
            pl.BlockSpec((1, ATTN_WIDTH, CHUNK), lambda b, i: (b, 0, i)),
            pl.BlockSpec((1, IDX_HEADS * IDX_DIM, CHUNK), lambda b, i: (b, 0, i)),
            pl.BlockSpec((1, IDX_HEADS, CHUNK), lambda b, i: (b, 0, i)),
            pl.BlockSpec((1, nq, CHUNK, ATTN_WIDTH), lambda b, i: (b, 0, 0, 0)),
            pl.BlockSpec((1, nq, VT_ROWS, CHUNK), lambda b, i: (b, 0, 0, 0)),
            pl.BlockSpec((1, nq, CHUNK, IDX_DIM), lambda b, i: (b, 0, 0, 0)),
            pl.BlockSpec((2, ATTN_HEADS, CHUNK, CHUNK), lambda b, i: (0, 0, 0, 0)),
        ],
        out_specs=pl.BlockSpec((1, CHUNK, ATTN_WIDTH), lambda b, i: (b, i, 0)),
        scratch_shapes=[
            pltpu.VMEM((nq, CHUNK, CHUNK), F32),
            pltpu.VMEM((SUBLANES, CHUNK), F32),
            pltpu.VMEM((Q2_ROWS, CHUNK), BF16),
            pltpu.VMEM((FAR_SPAN * CHUNK, CHUNK), F32),
            pltpu.VMEM((ATTN_HEADS, FAR_SPAN * CHUNK, CHUNK), F32),
            pltpu.VMEM((ATTN_HEADS, FAR_SPAN * CHUNK, CHUNK), BF16),
            pltpu.VMEM((ATTN_HEADS, CHUNK), F32),
            pltpu.VMEM((ATTN_HEADS, CHUNK), F32),
            pltpu.VMEM((VT_ROWS, CHUNK), F32),
        ],
        compiler_params=pltpu.CompilerParams(
            dimension_semantics=("arbitrary", "arbitrary"), vmem_limit_bytes=VMEM_LIMIT),
        name="dsa_attn",
    )(qt, iqt, iwt, k4, vt, ik4, bias)


ROUTE_LANES = LANES
ROUTE_ROWS = -(-(N_EXPERTS + N_GROUPS) // SUBLANES) * SUBLANES
RW_LANES = SUBLANES


def _layer_norm(y, g, b):
    mu = jnp.mean(y, axis=-1, keepdims=True)
    yc = y - mu
    var = jnp.mean(yc * yc, axis=-1, keepdims=True)
    return yc * lax.rsqrt(var + LN_EPS) * g + b


def _mix_kernel(attn_ref, u_ref, halo_ref, x_ref, wpool_ref, pscale_ref, wout_ref, g_ref, b_ref,
                wr_ref, br_ref, h_ref, route_ref, rw_ref, *, tm):
    i = pl.program_id(1)
    halo = jnp.where(i > 0, halo_ref[0], 0.0)
    ue = jnp.concatenate([halo, u_ref[0]], axis=0)
    pos = i * tm + lax.broadcasted_iota(I32, (tm, 1), 0)
    mixed = []
    for g, w in enumerate(POOL_WINDOWS):
        gs = slice(g * POOL_GROUP_DIM, (g + 1) * POOL_GROUP_DIM)
        ch = ue[:, gs]
        win = ch
        step = 1
        while step < w:
            win = win + pltpu.roll(win, step, axis=0)
            step *= 2
        cnt = jnp.minimum(pos + 1, w).astype(F32)
        pooled = win[POOL_HALO:] / cnt - ch[POOL_HALO:]
        mg = jnp.dot(pooled.astype(BF16), wpool_ref[g], preferred_element_type=F32)
        mixed.append((mg * pscale_ref[:, gs]).astype(BF16))
    cat = jnp.concatenate([attn_ref[0]] + mixed, axis=-1)
    mix = jnp.dot(cat, wout_ref[...], preferred_element_type=F32)
    h = _layer_norm(DEEPNORM_ALPHA * x_ref[0] + mix, g_ref[...], b_ref[...])
    h_ref[0] = h
    hb = h.astype(BF16)

    lg = jnp.dot(hb, wr_ref[...], preferred_element_type=F32) + br_ref[...]
    lt = lg.T[0:ROUTE_ROWS, :]
    sub = lax.broadcasted_iota(I32, (ROUTE_ROWS, tm), 0).astype(F32)
    gl = jnp.where(sub >= N_EXPERTS, jnp.where(sub < N_EXPERTS + N_GROUPS, lt, -jnp.inf), -jnp.inf)
    ge = jnp.exp(gl - jnp.max(gl, axis=0, keepdims=True))
    pg = ge / jnp.sum(ge, axis=0, keepdims=True)
    pg_top = jnp.max(pg, axis=0, keepdims=True)
    g_row = jnp.min(jnp.where(pg == pg_top, sub, float(ROUTE_LANES)), axis=0, keepdims=True)
    e_lo = (g_row - N_EXPERTS) * EXPERTS_PER_GROUP
    fl = jnp.where(sub >= e_lo, jnp.where(sub < e_lo + EXPERTS_PER_GROUP, lt, -jnp.inf), -jnp.inf)
    fe = jnp.exp(fl - jnp.max(fl, axis=0, keepdims=True))
    pf = fe / jnp.sum(fe, axis=0, keepdims=True)
    p1 = jnp.max(pf, axis=0, keepdims=True)
    i1 = jnp.min(jnp.where(pf == p1, sub, float(ROUTE_LANES)), axis=0, keepdims=True)
    pr = jnp.where(sub == i1, -1.0, jnp.where(fl == -jnp.inf, -1.0, pf))
    p2 = jnp.max(pr, axis=0, keepdims=True)
    i2 = jnp.min(jnp.where(pr == p2, sub, float(ROUTE_LANES)), axis=0, keepdims=True)
    psum = p1 + p2
    w1 = pg_top * p1 / psum
    w2 = pg_top * p2 / psum
    row8 = lax.broadcasted_iota(I32, (SUBLANES, tm), 0)
    route_ref[...] = jnp.where(row8 == 0, i1, jnp.where(row8 == 1, i2, 0.0)).astype(I32)
    rw_t = jnp.where(sub == 0.0, w1, jnp.where(sub == 1.0, w2, 0.0))
    rw_t = jnp.concatenate([rw_t, jnp.zeros((ROUTE_LANES - ROUTE_ROWS, tm), F32)], axis=0)
    rw_ref[0] = rw_t.T[:, 0:RW_LANES]


def _mix_norm_route(attn, u, x, w_pool, pool_scale, w_out, ln_g, ln_b, w_r1, b_r1, w_r2, b_r2, *, tm=512):
    B, S, D = x.shape
    wr = jnp.concatenate([w_r2, w_r1, jnp.zeros((D, ROUTE_LANES - N_EXPERTS - N_GROUPS), w_r1.dtype)],
                         axis=1).astype(BF16)
    br = jnp.concatenate([b_r2, b_r1, jnp.zeros((ROUTE_LANES - N_EXPERTS - N_GROUPS,), b_r1.dtype)])[None, :]
    hpt = tm // POOL_HALO
    return pl.pallas_call(
        functools.partial(_mix_kernel, tm=tm),
        out_shape=(jax.ShapeDtypeStruct((B, S, D), F32),
                   jax.ShapeDtypeStruct((SUBLANES, B * S), I32),
                   jax.ShapeDtypeStruct((B, S, RW_LANES), F32)),
        grid=(B, S // tm),
        in_specs=[
            pl.BlockSpec((1, tm, ATTN_WIDTH), lambda b, i: (b, i, 0)),
            pl.BlockSpec((1, tm, POOL_WIDTH), lambda b, i: (b, i, 0)),
            pl.BlockSpec((1, POOL_HALO, POOL_WIDTH), lambda b, i: (b, jnp.maximum(i * hpt - 1, 0), 0)),
            pl.BlockSpec((1, tm, D), lambda b, i: (b, i, 0)),
            pl.BlockSpec((len(POOL_WINDOWS), POOL_GROUP_DIM, POOL_GROUP_DIM), lambda b, i: (0, 0, 0)),
            pl.BlockSpec((1, POOL_WIDTH), lambda b, i: (0, 0)),
            pl.BlockSpec((D, D), lambda b, i: (0, 0)),
            pl.BlockSpec((1, D), lambda b, i: (0, 0)),
            pl.BlockSpec((1, D), lambda b, i: (0, 0)),
            pl.BlockSpec((D, ROUTE_LANES), lambda b, i: (0, 0)),
            pl.BlockSpec((1, ROUTE_LANES), lambda b, i: (0, 0)),
        ],
        out_specs=(pl.BlockSpec((1, tm, D), lambda b, i: (b, i, 0)),
                   pl.BlockSpec((SUBLANES, tm), lambda b, i: (0, b * (S // tm) + i)),
                   pl.BlockSpec((1, tm, RW_LANES), lambda b, i: (b, i, 0))),
        compiler_params=pltpu.CompilerParams(
            dimension_semantics=("arbitrary", "arbitrary"), vmem_limit_bytes=VMEM_LIMIT),
        name="mix_norm",
    )(attn, u, u, x, w_pool.astype(BF16), pool_scale[None, :], w_out.astype(BF16),
      ln_g[None, :], ln_b[None, :], wr, br)


MOE_TM = 512
PLAN_TB = 1024
GATHER_UNROLL = 8


def _moe_rows(n_tokens):
    return 2 * n_tokens + N_EXPERTS * MOE_TM


def _plan_kernel(route_ref, pos_ref, tile_ref, cnt_ref, off_ref, carry_ref, *, n_tile_lanes):
    ph = pl.program_id(0)
    b = pl.program_id(1)
    tb = route_ref.shape[1]
    esub = lax.broadcasted_iota(I32, (N_EXPERTS, tb), 0)
    a1 = jnp.where(esub == route_ref[0:1, :], 1.0, 0.0)
    a2 = jnp.where(esub == route_ref[1:2, :], 1.0, 0.0)
    a = a1 + a2
    n_here = jnp.sum(a, axis=1, keepdims=True)

    @pl.when(jnp.logical_and(ph == 0, b == 0))
    def _():
        cnt_ref[...] = jnp.zeros(cnt_ref.shape, F32)

    @pl.when(ph == 0)
    def _():
        cnt_ref[...] += n_here

    @pl.when(jnp.logical_and(ph == 1, b == 0))
    def _():
        n_tile = jnp.floor((cnt_ref[...] + (MOE_TM - 1)) * (1.0 / MOE_TM))
        er = lax.broadcasted_iota(I32, (N_EXPERTS, N_EXPERTS), 0)
        ec = lax.broadcasted_iota(I32, (N_EXPERTS, N_EXPERTS), 1)
        before = jnp.where(ec < er, 1.0, 0.0).astype(BF16)
        t_off = jnp.dot(before, n_tile.astype(BF16), preferred_element_type=F32)
        off_ref[...] = t_off * MOE_TM
        carry_ref[...] = jnp.zeros(carry_ref.shape, F32)
        j = lax.broadcasted_iota(I32, (N_EXPERTS, n_tile_lanes), 1).astype(F32)
        owner = jnp.sum(jnp.where(t_off[:, 0:1] <= j, 1.0, 0.0), axis=0, keepdims=True) - 1.0
        total = jnp.sum(n_tile[:, 0:1], axis=0, keepdims=True)
        used = jnp.where(j[0:1, :] < total, 1.0, 0.0)
        starts_next = jnp.sum(jnp.where(t_off[:, 0:1] == j + 1.0, 1.0, 0.0), axis=0, keepdims=True)
        ragged = jnp.where(j[0:1, :] + 1.0 >= total, 1.0, jnp.where(starts_next > 0.0, 1.0, 0.0))
        row8 = lax.broadcasted_iota(I32, (SUBLANES, n_tile_lanes), 0)
        tile_ref[...] = jnp.where(row8 == 0, owner,
                                  jnp.where(row8 == 1, used, jnp.where(row8 == 2, ragged, 0.0))).astype(I32)

    @pl.when(ph == 1)
    def _():
        tr = lax.broadcasted_iota(I32, (tb, tb), 0)
        tc = lax.broadcasted_iota(I32, (tb, tb), 1)
        earlier = jnp.where(tr < tc, 1.0, 0.0).astype(BF16)
        seen = jnp.dot(a.astype(BF16), earlier, preferred_element_type=F32)
        dest = seen + carry_ref[:, 0:1] + off_ref[:, 0:1]
        p1 = jnp.sum(a1 * dest, axis=0, keepdims=True)
        p2 = jnp.sum(a2 * dest, axis=0, keepdims=True)
        row8 = lax.broadcasted_iota(I32, (SUBLANES, tb), 0)
        pos_ref[...] = jnp.where(row8 == 0, p1, jnp.where(row8 == 1, p2, 0.0)).astype(I32)
        carry_ref[...] += n_here


def _moe_plan(route_t):
    _, T = route_t.shape
    n_tiles = _moe_rows(T) // MOE_TM
    n_tile_lanes = -(-n_tiles // LANES) * LANES
    nb = T // PLAN_TB
    return pl.pallas_call(
        functools.partial(_plan_kernel, n_tile_lanes=n_tile_lanes),
        out_shape=(jax.ShapeDtypeStruct((SUBLANES, T), I32),
                   jax.ShapeDtypeStruct((SUBLANES, n_tile_lanes), I32)),
        grid=(2, nb),
        in_specs=[pl.BlockSpec((SUBLANES, PLAN_TB), lambda ph, b: (0, b))],
        out_specs=(pl.BlockSpec((SUBLANES, PLAN_TB), lambda ph, b: (0, b * ph)),
                   pl.BlockSpec((SUBLANES, n_tile_lanes), lambda ph, b: (0, 0))),
        scratch_shapes=[pltpu.VMEM((N_EXPERTS, LANES), F32),
                        pltpu.VMEM((N_EXPERTS, LANES), F32),
                        pltpu.VMEM((N_EXPERTS, LANES), F32)],
        compiler_params=pltpu.CompilerParams(dimension_semantics=("arbitrary", "arbitrary")),
        name="moe_plan",
    )(route_t)


def _row_copy(src_ref, src_row, dst_ref, dst_row, sem):
    return pltpu.make_async_copy(src_ref.at[pl.ds(src_row, 1)], dst_ref.at[pl.ds(dst_row, 1)], sem)


def _dispatch_kernel(ragged_ref, p1_ref, p2_ref, h_ref, out_hbm, zero_ref, sem, zero_sem, *, tb, n_tiles):
    @pl.when(pl.program_id(0) == 0)
    def _():
        zero_ref[...] = jnp.zeros(zero_ref.shape, zero_ref.dtype)

        def zero_tile(j):
            return pltpu.make_async_copy(zero_ref, out_hbm.at[pl.ds(j * MOE_TM, MOE_TM)], zero_sem)

        for j in range(n_tiles):
            @pl.when(ragged_ref[j] == 1)
            def _():
                zero_tile(j).start()
        for j in range(n_tiles):
            @pl.when(ragged_ref[j] == 1)
            def _():
                zero_tile(j).wait()

    def start(t, carry):
        _row_copy(h_ref, t, out_hbm, p1_ref[t], sem).start()
        _row_copy(h_ref, t, out_hbm, p2_ref[t], sem).start()
        return carry

    lax.fori_loop(0, tb, start, 0, unroll=GATHER_UNROLL)

    def wait(t, carry):
        _row_copy(h_ref, 0, out_hbm, 0, sem).wait()
        _row_copy(h_ref, 0, out_hbm, 0, sem).wait()
        return carry

    lax.fori_loop(0, tb, wait, 0, unroll=GATHER_UNROLL)


def _moe_dispatch(h, pos1, pos2, ragged, *, tb=1024):
    T, D = h.shape
    rows = _moe_rows(T)
    n_tiles = rows // MOE_TM
    return pl.pallas_call(
        functools.partial(_dispatch_kernel, tb=tb, n_tiles=n_tiles),
        out_shape=jax.ShapeDtypeStruct((rows, D), h.dtype),
        grid=(T // tb,),
        in_specs=[pl.BlockSpec(memory_space=pltpu.SMEM),
                  pl.BlockSpec((tb,), lambda i: (i,), memory_space=pltpu.SMEM),
                  pl.BlockSpec((tb,), lambda i: (i,), memory_space=pltpu.SMEM),
                  pl.BlockSpec((tb, D), lambda i: (i, 0))],
        out_specs=pl.BlockSpec(memory_space=pl.ANY),
        scratch_shapes=[pltpu.VMEM((MOE_TM, D), h.dtype), pltpu.SemaphoreType.DMA(()),
                        pltpu.SemaphoreType.DMA(())],
        compiler_params=pltpu.CompilerParams(dimension_semantics=("arbitrary",), vmem_limit_bytes=VMEM_LIMIT),
        name="moe_dispatch",
    )(ragged, pos1, pos2, h)


def _ffn_kernel(owner_ref, used_ref, x_ref, wg_ref, wu_ref, wd_ref, o_ref):
    j = pl.program_id(0)

    @pl.when(used_ref[j] == 1)
    def _():
        xb = x_ref[...].astype(BF16)
        gate = jnp.dot(xb, wg_ref[0].astype(BF16), preferred_element_type=F32)
        up = jnp.dot(xb, wu_ref[0].astype(BF16), preferred_element_type=F32)
        a = gate * jax.nn.sigmoid(gate) * up
        o_ref[...] = jnp.dot(a.astype(BF16), wd_ref[0].astype(BF16), preferred_element_type=F32)

    @pl.when(used_ref[j] == 0)
    def _():
        o_ref[...] = jnp.zeros(o_ref.shape, F32)


def _moe_ffn(xs, owner, used, w_gate, w_up, w_down):
    rows, D = xs.shape
    E, _, F = w_gate.shape
    return pl.pallas_call(
        _ffn_kernel,
        out_shape=jax.ShapeDtypeStruct((rows, D), F32),
        grid_spec=pltpu.PrefetchScalarGridSpec(
            num_scalar_prefetch=2,
            grid=(rows // MOE_TM,),
            in_specs=[pl.BlockSpec((MOE_TM, D), lambda j, ow, us: (j, 0)),
                      pl.BlockSpec((1, D, F), lambda j, ow, us: (ow[j], 0, 0)),
                      pl.BlockSpec((1, D, F), lambda j, ow, us: (ow[j], 0, 0)),
                      pl.BlockSpec((1, F, D), lambda j, ow, us: (ow[j], 0, 0))],
            out_specs=pl.BlockSpec((MOE_TM, D), lambda j, ow, us: (j, 0)),
        ),
        compiler_params=pltpu.CompilerParams(dimension_semantics=("arbitrary",), vmem_limit_bytes=VMEM_LIMIT),
        name="moe_ffn",
    )(owner, used, xs, w_gate, w_up, w_down)


def _combine_kernel(p1_ref, p2_ref, p1n_ref, p2n_ref, rw_ref, h_ref, g_ref, b_ref, y_hbm, o_ref,
                    buf_ref, sem, *, tm):
    s = pl.program_id(0)
    slot = s % 2

    def fetch(pa_ref, pb_ref, into):
        def start(t, carry):
            pltpu.make_async_copy(y_hbm.at[pl.ds(pa_ref[t], 1)], buf_ref.at[into, 0, pl.ds(t, 1)],
                                  sem.at[into]).start()
            pltpu.make_async_copy(y_hbm.at[pl.ds(pb_ref[t], 1)], buf_ref.at[into, 1, pl.ds(t, 1)],
                                  sem.at[into]).start()
            return carry
        lax.fori_loop(0, tm, start, 0, unroll=GATHER_UNROLL)

    @pl.when(s == 0)
    def _():
        fetch(p1_ref, p2_ref, 0)

    @pl.when(s + 1 < pl.num_programs(0))
    def _():
        fetch(p1n_ref, p2n_ref, 1 - slot)

    def wait(t, carry):
        for half in range(2):
            pltpu.make_async_copy(y_hbm.at[pl.ds(0, 1)], buf_ref.at[slot, half, pl.ds(0, 1)], sem.at[slot]).wait()
        return carry

    lax.fori_loop(0, tm, wait, 0, unroll=GATHER_UNROLL)
    y = rw_ref[:, 0:1] * buf_ref[slot, 0] + rw_ref[:, 1:2] * buf_ref[slot, 1]
    o_ref[...] = _layer_norm(DEEPNORM_ALPHA * h_ref[...] + y, g_ref[...], b_ref[...])


def _moe_combine(h, y_sorted, pos1, pos2, rw, ln_g, ln_b, *, tm=512):
    T, D = h.shape
    last = T // tm - 1
    smem = lambda imap: pl.BlockSpec((tm,), imap, memory_space=pltpu.SMEM)
    return pl.pallas_call(
        functools.partial(_combine_kernel, tm=tm),
        out_shape=jax.ShapeDtypeStruct((T, D), F32),
        grid=(T // tm,),
        in_specs=[smem(lambda i: (i,)), smem(lambda i: (i,)),
                  smem(lambda i: (jnp.minimum(i + 1, last),)), smem(lambda i: (jnp.minimum(i + 1, last),)),
                  pl.BlockSpec((tm, RW_LANES), lambda i: (i, 0)),
                  pl.BlockSpec((tm, D), lambda i: (i, 0)),
                  pl.BlockSpec((1, D), lambda i: (0, 0)),
                  pl.BlockSpec((1, D), lambda i: (0, 0)),
                  pl.BlockSpec(memory_space=pl.ANY)],
        out_specs=pl.BlockSpec((tm, D), lambda i: (i, 0)),
        scratch_shapes=[pltpu.VMEM((2, 2, tm, D), F32), pltpu.SemaphoreType.DMA((2,))],
        compiler_params=pltpu.CompilerParams(dimension_semantics=("arbitrary",), vmem_limit_bytes=VMEM_LIMIT),
        name="moe_combine",
    )(pos1, pos2, pos1, pos2, rw, h, ln_g[None, :], ln_b[None, :], y_sorted)


def _moe_norm(h, route_t, rw, w_gate, w_up, w_down, ln_g, ln_b):
    pos, tiles = _moe_plan(route_t)
    n_tiles = _moe_rows(h.shape[0]) // MOE_TM
    xs = _moe_dispatch(h, pos[0], pos[1], tiles[2, :n_tiles])
    ys = _moe_ffn(xs, tiles[0, :n_tiles], tiles[1, :n_tiles], w_gate, w_up, w_down)
    return _moe_combine(h, ys, pos[0], pos[1], rw, ln_g, ln_b)


def kernel(x, w_in, w_pool, pool_scale, w_out, rel_bias, ln1_g, ln1_b, w_r1, b_r1, w_r2, b_r2,
           w_gate, w_up, w_down, ln2_g, ln2_b):
    B, S, D = x.shape
    assert w_in.shape[0] == DEPTH == 1 and S % CHUNK == 0
    bias = _bias_tiles(rel_bias)
    qt, vt, iqt, iwt, k, ik, u = _in_proj(x, w_in[0])
    attn = _dsa_attention(qt, iqt, iwt, k, vt, ik, bias)
    h, route_t, rw = _mix_norm_route(attn, u, x, w_pool[0], pool_scale[0], w_out[0], ln1_g[0], ln1_b[0],
                                     w_r1[0], b_r1[0], w_r2[0], b_r2[0])
    out = _moe_norm(h.reshape(B * S, D), route_t, rw.reshape(B * S, RW_LANES),
                    w_gate[0], w_up[0], w_down[0], ln2_g[0], ln2_b[0])
    return out.reshape(B, S, D)
```
